```python
import math
import jax, jax.numpy as jnp
from jax import lax
import numpy as np

D_MODEL = 4096
BATCH = 4
SEQ = 4096
DEPTH = 2

SSM_INNER = D_MODEL // 2
SSM_HEAD_DIM = 64
SSM_HEADS = SSM_INNER // SSM_HEAD_DIM
SSM_GROUPS = 4
SSM_STATE = 128
SSM_CONV = 4
SSM_CHUNK = 128
SSM_XBC = SSM_INNER + 2 * SSM_GROUPS * SSM_STATE
FOX_HEAD_DIM = 128
FOX_HEADS = (D_MODEL // 4) // FOX_HEAD_DIM
FOX_WIDTH = FOX_HEADS * FOX_HEAD_DIM
FOX_BLOCK = 128
GDN_HEAD_DIM = 128
GDN_HEADS = (D_MODEL // 4) // GDN_HEAD_DIM
GDN_WIDTH = GDN_HEADS * GDN_HEAD_DIM
GDN_CONV = 4
GDN_CHUNK = 64
D_FF = ((8 * D_MODEL // 3) + 255) // 256 * 256
FFN_CONV = 3
NORM_EPS = 1e-6
N_IN = (SSM_INNER + SSM_XBC + SSM_HEADS + 3 * FOX_WIDTH + FOX_HEADS
        + 3 * GDN_WIDTH + 2 * GDN_HEADS + GDN_WIDTH + 3 * D_MODEL)

kernel_name = 'hybrid_ssd_fox_gdn_block'


def _split_points():
    sizes = [SSM_INNER, SSM_XBC, SSM_HEADS, 3 * FOX_WIDTH, FOX_HEADS, 3 * GDN_WIDTH,
             GDN_HEADS, GDN_HEADS, GDN_WIDTH, D_MODEL, D_MODEL]
    return [int(s) for s in np.cumsum(sizes)]


def rmsnorm(x, w, groups=1):
    xf = x.astype(jnp.float32)
    shp = xf.shape
    xg = xf.reshape(shp[:-1] + (groups, shp[-1] // groups))
    xg = xg * lax.rsqrt(jnp.mean(xg * xg, axis=-1, keepdims=True) + NORM_EPS)
    return (xg.reshape(shp) * w).astype(x.dtype)


def l2norm(t):
    return t * lax.rsqrt(jnp.sum(t * t, axis=-1, keepdims=True) + 1e-6)


def causal_dwconv(u, w, bias=None):
    k_width = w.shape[0]
    seq = u.shape[1]
    up = jnp.pad(u, ((0, 0), (k_width - 1, 0), (0, 0)))
    y = up[:, 0:seq] * w[0]
    for j in range(1, k_width):
        y = y + up[:, j:j + seq] * w[j]
    if bias is not None:
        y = y + bias
    return y


def ssd_chunked(x, dt, a, bm, cm):
    bsz, seq, n_heads, p = x.shape
    g, n = bm.shape[2], bm.shape[3]
    hpg = n_heads // g
    nc, cl = seq // SSM_CHUNK, SSM_CHUNK
    xs = (x * dt[..., None]).reshape(bsz, nc, cl, g, hpg, p)
    a_cs = jnp.cumsum((dt * a).reshape(bsz, nc, cl, g, hpg), axis=2)
    bc = bm.reshape(bsz, nc, cl, g, n)
    cc = cm.reshape(bsz, nc, cl, g, n)
    causal = jnp.tril(jnp.ones((cl, cl), dtype=bool))[None, None, :, :, None, None]
    seg = a_cs[:, :, :, None] - a_cs[:, :, None, :]
    decay = jnp.exp(jnp.where(causal, seg, -jnp.inf))
    cb = jnp.einsum('bclgn,bcsgn->bclsg', cc, bc)
    y_diag = jnp.einsum('bclsg,bclsgi,bcsgip->bclgip', cb, decay, xs)
    decay_to_end = jnp.exp(a_cs[:, :, -1:] - a_cs)
    states = jnp.einsum('bclgn,bclgi,bclgip->bcgipn', bc, decay_to_end, xs)
    chunk_decay = jnp.exp(a_cs[:, :, -1])

    def step(h, inp):
        st, cd = inp
        return h * cd[..., None, None] + st, h

    h0 = jnp.zeros((bsz, g, hpg, p, n), x.dtype)
    _, h_in = lax.scan(step, h0, (jnp.moveaxis(states, 1, 0), jnp.moveaxis(chunk_decay, 1, 0)))
    h_in = jnp.moveaxis(h_in, 0, 1)
    y_off = jnp.einsum('bclgn,bcgipn,bclgi->bclgip', cc, h_in, jnp.exp(a_cs))
    return (y_diag + y_off).reshape(bsz, seq, n_heads, p)


def ssd_mixer(z, xbc, dt_raw, conv_w, conv_b, dt_bias, a_log, d_skip, norm_w):
    bsz, seq, _ = z.shape
    xbc = jax.nn.silu(causal_dwconv(xbc, conv_w, conv_b))
    xs, bm, cm = jnp.split(xbc, [SSM_INNER, SSM_INNER + SSM_GROUPS * SSM_STATE], axis=-1)
    xs = xs.reshape(bsz, seq, SSM_HEADS, SSM_HEAD_DIM).astype(jnp.float32)
    bm = bm.reshape(bsz, seq, SSM_GROUPS, SSM_STATE).astype(jnp.float32)
    cm = cm.reshape(bsz, seq, SSM_GROUPS, SSM_STATE).astype(jnp.float32)
    dt = jax.nn.softplus(dt_raw.astype(jnp.float32) + dt_bias)
    a = -jnp.exp(a_log.astype(jnp.float32))
    y = ssd_chunked(xs, dt, a, bm, cm) + xs * d_skip[:, None]
    y = y.reshape(bsz, seq, SSM_INNER) * jax.nn.silu(z.astype(jnp.float32))
    y = rmsnorm(y, norm_w, groups=SSM_GROUPS)
    return y.astype(z.dtype)


def forgetting_attention(q, k, v, log_f):
    bsz, seq, h, d = q.shape
    nb = seq // FOX_BLOCK
    cum_f = jnp.cumsum(log_f, axis=1)
    key_pos = jnp.arange(seq)
    k_bias = jnp.moveaxis(cum_f, 1, 2)[:, :, None, :]
    q_blocks = jnp.moveaxis(q.reshape(bsz, nb, FOX_BLOCK, h, d), 1, 0)
    f_blocks = jnp.moveaxis(cum_f.reshape(bsz, nb, FOX_BLOCK, h), 1, 0)
    vf = v.astype(jnp.float32)
    scale = d ** -0.5

    def attend(args):
        q_blk, f_blk, blk = args
        s = jnp.einsum('bqhd,bkhd->bhqk', q_blk, k).astype(jnp.float32) * scale
        s = s + jnp.moveaxis(f_blk, 1, 2)[..., None] - k_bias
        q_pos = blk * FOX_BLOCK + jnp.arange(FOX_BLOCK)
        s = jnp.where(key_pos[None, :] <= q_pos[:, None], s, -jnp.inf)
        p = jax.nn.softmax(s, axis=-1)
        return jnp.einsum('bhqk,bkhd->bqhd', p, vf)

    out = lax.map(attend, (q_blocks, f_blocks, jnp.arange(nb)))
    return jnp.moveaxis(out, 0, 1).reshape(bsz, seq, h * d)


def fox_mixer(qkv, f_raw, f_bias):
    bsz, seq, _ = qkv.shape
    q, k, v = [t.reshape(bsz, seq, FOX_HEADS, FOX_HEAD_DIM) for t in jnp.split(qkv, 3, axis=-1)]
    log_f = jax.nn.log_sigmoid(f_raw.astype(jnp.float32) + f_bias)
    return forgetting_attention(q, k, v, log_f).astype(qkv.dtype)


def gated_delta_chunked(q, k, v, g, beta):
    bsz, seq, h, dk = q.shape
    dv = v.shape[-1]
    cl = GDN_CHUNK
    nc = seq // cl

    def to_chunks(t):
        return jnp.moveaxis(t.reshape((bsz, nc, cl, h) + t.shape[3:]), 3, 1)

    q, k, v, g, beta = [to_chunks(t) for t in (q, k, v, g, beta)]
    gc = jnp.cumsum(g, axis=-1)
    lower = jnp.tril(jnp.ones((cl, cl), dtype=bool))
    strict = jnp.tril(jnp.ones((cl, cl), dtype=bool), -1)
    decay = jnp.exp(jnp.where(lower, gc[..., :, None] - gc[..., None, :], -jnp.inf))
    kb = k * beta[..., None]
    a_mat = jnp.where(strict, jnp.einsum('bhcld,bhcsd->bhcls', kb, k) * decay, 0.0)
    rhs = jnp.concatenate([v * beta[..., None], kb * jnp.exp(gc)[..., None]], axis=-1)
    sol = lax.linalg.triangular_solve(a_mat + jnp.eye(cl, dtype=a_mat.dtype), rhs,
                                      left_side=True, lower=True, unit_diagonal=True)
    u, w = sol[..., :dv], sol[..., dv:]
    attn = jnp.einsum('bhcld,bhcsd->bhcls', q, k) * decay
    q_dec = q * jnp.exp(gc)[..., None]
    k_dec = k * jnp.exp(gc[..., -1:] - gc)[..., None]
    chunk_decay = jnp.exp(gc[..., -1])

    def step(state, inp):
        q_c, k_c, u_c, w_c, attn_c, cd = inp
        v_new = u_c - jnp.einsum('bhld,bhde->bhle', w_c, state)
        o = jnp.einsum('bhld,bhde->bhle', q_c, state) + jnp.einsum('bhls,bhse->bhle', attn_c, v_new)
        state = state * cd[..., None, None] + jnp.einsum('bhld,bhle->bhde', k_c, v_new)
        return state, o

    xs = tuple(jnp.moveaxis(t, 2, 0) for t in (q_dec, k_dec, u, w, attn, chunk_decay))
    s0 = jnp.zeros((bsz, h, dk, dv), q.dtype)
    _, o = lax.scan(step, s0, xs)
    o = jnp.moveaxis(o, 0, 2)
    return jnp.transpose(o, (0, 2, 3, 1, 4)).reshape(bsz, seq, h, dv)


def gdn_mixer(qkv, a_raw, b_raw, z, conv_w, dt_bias, a_log, norm_w):
    bsz, seq, _ = qkv.shape
    qkv = jax.nn.silu(causal_dwconv(qkv, conv_w))
    q, k, v = [t.reshape(bsz, seq, GDN_HEADS, GDN_HEAD_DIM).astype(jnp.float32)
               for t in jnp.split(qkv, 3, axis=-1)]
    q = l2norm(q) * GDN_HEAD_DIM ** -0.5
    k = l2norm(k)
    beta = jax.nn.sigmoid(b_raw.astype(jnp.float32))
    g = -jnp.exp(a_log.astype(jnp.float32)) * jax.nn.softplus(a_raw.astype(jnp.float32) + dt_bias)
    o = gated_delta_chunked(q, k, v, g, beta)
    o = rmsnorm(o, norm_w) * jax.nn.silu(z.astype(jnp.float32).reshape(bsz, seq, GDN_HEADS, GDN_HEAD_DIM))
    return o.reshape(bsz, seq, GDN_WIDTH).astype(qkv.dtype)


def conv_ffn(h, w_up, conv_w, conv_b, w_down):
    u = causal_dwconv(h @ w_up, conv_w, conv_b)
    gate, val = jnp.split(u, 2, axis=-1)
    return (jax.nn.silu(gate) * val) @ w_down


def setup_inputs(seed: int = 0) -> dict:
    key = jax.random.key(seed)
    ks = jax.random.split(key, 24)
    f32 = jnp.float32

    def nrm(k, shape, scale):
        return jax.random.normal(k, shape, f32) * scale

    def gain(k, shape):
        return 1.0 + 0.01 * jax.random.normal(k, shape, f32)

    def dt_bias_init(k, shape):
        dt = jnp.exp(jax.random.uniform(k, shape, f32, math.log(1e-3), math.log(1e-1)))
        return dt + jnp.log(-jnp.expm1(-dt))

    def a_log_init(k, shape):
        return jnp.log(jax.random.uniform(k, shape, f32, 1.0, 16.0))

    return {
        'x': jax.random.normal(ks[0], (BATCH, SEQ, D_MODEL), f32),
        'norm_mix': gain(ks[1], (DEPTH, D_MODEL)),
        'w_in': nrm(ks[2], (DEPTH, D_MODEL, N_IN), D_MODEL ** -0.5),
        'ssm_conv_w': nrm(ks[3], (DEPTH, SSM_CONV, SSM_XBC), SSM_CONV ** -0.5),
        'ssm_conv_b': nrm(ks[4], (DEPTH, SSM_XBC), 0.01),
        'ssm_dt_bias': dt_bias_init(ks[5], (DEPTH, SSM_HEADS)),
        'ssm_a_log': a_log_init(ks[6], (DEPTH, SSM_HEADS)),
        'ssm_d': gain(ks[7], (DEPTH, SSM_HEADS)),
        'ssm_norm': gain(ks[8], (DEPTH, SSM_INNER)),
        'fox_f_bias': jax.random.uniform(ks[9], (DEPTH, FOX_HEADS), f32, 1.0, 4.0),
        'gdn_conv_w': nrm(ks[10], (DEPTH, GDN_CONV, 3 * GDN_WIDTH), GDN_CONV ** -0.5),
        'gdn_dt_bias': dt_bias_init(ks[11], (DEPTH, GDN_HEADS)),
        'gdn_a_log': a_log_init(ks[12], (DEPTH, GDN_HEADS)),
        'gdn_norm': gain(ks[13], (DEPTH, GDN_HEAD_DIM)),
        'w_br_ssm': nrm(ks[14], (DEPTH, SSM_INNER, D_MODEL), SSM_INNER ** -0.5),
        'w_br_fox': nrm(ks[15], (DEPTH, FOX_WIDTH, D_MODEL), FOX_WIDTH ** -0.5),
        'w_br_gdn': nrm(ks[16], (DEPTH, GDN_WIDTH, D_MODEL), GDN_WIDTH ** -0.5),
        'w_out': nrm(ks[17], (DEPTH, D_MODEL, D_MODEL), D_MODEL ** -0.5),
        'norm_ffn': gain(ks[18], (DEPTH, D_MODEL)),
        'w_up': nrm(ks[19], (DEPTH, D_MODEL, 2 * D_FF), D_MODEL ** -0.5),
        'ffn_conv_w': nrm(ks[20], (DEPTH, FFN_CONV, 2 * D_FF), FFN_CONV ** -0.5),
        'ffn_conv_b': nrm(ks[21], (DEPTH, 2 * D_FF), 0.01),
        'w_down': nrm(ks[22], (DEPTH, D_FF, D_MODEL), D_FF ** -0.5),
        'norm_final': gain(ks[23], (D_MODEL,)),
    }


def reference(x, norm_mix, w_in, ssm_conv_w, ssm_conv_b, ssm_dt_bias, ssm_a_log, ssm_d,
              ssm_norm, fox_f_bias, gdn_conv_w, gdn_dt_bias, gdn_a_log, gdn_norm,
              w_br_ssm, w_br_fox, w_br_gdn, w_out, norm_ffn, w_up, ffn_conv_w, ffn_conv_b,
              w_down, norm_final):
    points = _split_points()
    for layer in range(DEPTH):
        h = rmsnorm(x, norm_mix[layer])
        proj = h @ w_in[layer]
        (ssm_z, ssm_xbc, ssm_dt, fox_qkv, fox_f, gdn_qkv, gdn_a, gdn_b, gdn_z,
         gate_ssm, gate_fox, gate_gdn) = jnp.split(proj, points, axis=-1)
        y_ssm = ssd_mixer(ssm_z, ssm_xbc, ssm_dt, ssm_conv_w[layer], ssm_conv_b[layer],
                          ssm_dt_bias[layer], ssm_a_log[layer], ssm_d[layer], ssm_norm[layer])
        y_fox = fox_mixer(fox_qkv, fox_f, fox_f_bias[layer])
        y_gdn = gdn_mixer(gdn_qkv, gdn_a, gdn_b, gdn_z, gdn_conv_w[layer], gdn_dt_bias[layer],
                          gdn_a_log[layer], gdn_norm[layer])
        merged = (jax.nn.sigmoid(gate_ssm) * (y_ssm @ w_br_ssm[layer])
                  + jax.nn.sigmoid(gate_fox) * (y_fox @ w_br_fox[layer])
                  + jax.nn.sigmoid(gate_gdn) * (y_gdn @ w_br_gdn[layer]))
        x = x + merged @ w_out[layer]
        h = rmsnorm(x, norm_ffn[layer])
        x = x + conv_ffn(h, w_up[layer], ffn_conv_w[layer], ffn_conv_b[layer], w_down[layer])
    return rmsnorm(x, norm_final)
```

```python
import functools

import jax
import jax.numpy as jnp
from jax import lax
from jax.experimental import pallas as pl
from jax.experimental.pallas import tpu as pltpu

F32 = jnp.float32
BF16 = jnp.bfloat16

D_MODEL = 4096
SSM_INNER = 2048
SSM_HEAD_DIM = 64
SSM_HEADS = 32
SSM_GROUPS = 4
SSM_STATE = 128
SSM_CONV = 4
SSM_XBC = SSM_INNER + 2 * SSM_GROUPS * SSM_STATE
SSM_HPG = SSM_HEADS // SSM_GROUPS
SSM_GW = SSM_HPG * SSM_HEAD_DIM
FOX_HEADS = 8
FOX_HEAD_DIM = 128
FOX_WIDTH = FOX_HEADS * FOX_HEAD_DIM
GDN_HEADS = 8
GDN_HEAD_DIM = 128
GDN_WIDTH = GDN_HEADS * GDN_HEAD_DIM
GDN_CONV = 4
D_FF = 11008
D_FF_PAD = 11264
FFN_CONV = 3
NORM_EPS = 1e-6
N_MAIN = 24576
N_SMALL = 640
LANES = 128
HALO = 16

OFF_SSM_Z = 0
OFF_SSM_X = 2048
OFF_SSM_B = 4096
OFF_SSM_C = 4608
OFF_FOX_Q = 5120
OFF_FOX_K = 6144
OFF_FOX_V = 7168
OFF_GDN_Q = 8192
OFF_GDN_K = 9216
OFF_GDN_V = 10240
OFF_GDN_Z = 11264
OFF_GATE = 12288

SSD_CHUNK = 128
GDN_CHUNK = 128
GDN_HB = 4
FOX_TQ = 512
FOX_TK = 512
VMEM_LIMIT = 56 * 1024 * 1024


def _cparams(sem):
    return pltpu.CompilerParams(dimension_semantics=sem, vmem_limit_bytes=VMEM_LIMIT)


def _sigmoid(x):
    return 1.0 / (1.0 + jnp.exp(-x))


def _silu(x):
    return x * _sigmoid(x)


def _softplus(x):
    return jnp.maximum(x, 0.0) + jnp.log(1.0 + jnp.exp(-jnp.abs(x)))


def _split3(x):
    hi = x.astype(BF16)
    r1 = x - hi.astype(F32)
    mid = r1.astype(BF16)
    lo = (r1 - mid.astype(F32)).astype(BF16)
    return hi, mid, lo


def _dot(a, b):
    return jnp.dot(a, b, preferred_element_type=F32)


def _dot_nt(a, b):
    return lax.dot_general(a, b, (((1,), (1,)), ((), ())), preferred_element_type=F32)


def _dot_sel_r(x, sel):
    hi, mid, lo = _split3(x)
    return (_dot(lo, sel) + _dot(mid, sel)) + _dot(hi, sel)


def _dot_sel_l(sel, x):
    hi, mid, lo = _split3(x)
    return (_dot(sel, lo) + _dot(sel, mid)) + _dot(sel, hi)


def _dot2(a, b):
    ah = a.astype(BF16)
    al = (a - ah.astype(F32)).astype(BF16)
    bh = b.astype(BF16)
    bl = (b - bh.astype(F32)).astype(BF16)
    return (_dot(al, bh) + _dot(ah, bl)) + _dot(ah, bh)


def _lower_tri(n, strict=False):
    r = lax.broadcasted_iota(jnp.int32, (n, n), 0)
    c = lax.broadcasted_iota(jnp.int32, (n, n), 1)
    return (r > c) if strict else (r >= c)


def _rmsnorm_kernel(x_ref, w_ref, o_ref):
    x = x_ref[...]
    ms = jnp.mean(x * x, axis=-1, keepdims=True)
    o_ref[...] = (x * lax.rsqrt(ms + NORM_EPS) * w_ref[...]).astype(o_ref.dtype)


def rmsnorm(x, w, out_dtype, tm=512):
    m, d = x.shape
    return pl.pallas_call(
        _rmsnorm_kernel,
        out_shape=jax.ShapeDtypeStruct((m, d), out_dtype),
        grid=(m // tm,),
        in_specs=[pl.BlockSpec((tm, d), lambda i: (i, 0)),
                  pl.BlockSpec((1, d), lambda i: (0, 0))],
        out_specs=pl.BlockSpec((tm, d), lambda i: (i, 0)),
        compiler_params=_cparams(("parallel",)),
        name="rmsnorm",
    )(x, w.reshape(1, d))


def _matmul_kernel(a_ref, w_ref, o_ref):
    o_ref[...] = _dot(a_ref[...], w_ref[...]).astype(o_ref.dtype)


def matmul(a, w, out_dtype, tm, tn, name):
    m, k = a.shape
    n = w.shape[1]
    return pl.pallas_call(
        _matmul_kernel,
        out_shape=jax.ShapeDtypeStruct((m, n), out_dtype),
        grid=(m // tm, n // tn),
        in_specs=[pl.BlockSpec((tm, k), lambda i, j: (i, 0)),
                  pl.BlockSpec((k, tn), lambda i, j: (0, j))],
        out_specs=pl.BlockSpec((tm, tn), lambda i, j: (i, j)),
        compiler_params=_cparams(("parallel", "arbitrary")),
        name=name,
    )(a, w)


def _ssd_kernel(z_ref, xs_ref, b_ref, c_ref, dt_ref, cwx_ref, cwb_ref, cwc_ref,
                cbx_ref, cbb_ref, cbc_ref, dtb_ref, alog_ref, dskip_ref, nw_ref,
                e512_ref, e1024_ref, o_ref, ext_ref, state_ref):
    t = z_ref.shape[0]
    n = SSM_STATE
    gw = SSM_GW
    chunk = pl.program_id(2)

    @pl.when(chunk == 0)
    def _():
        ext_ref[0:8, :] = jnp.zeros((8, gw + 2 * n), F32)
        state_ref[...] = jnp.zeros_like(state_ref)

    @pl.when(chunk > 0)
    def _():
        ext_ref[0:8, :] = ext_ref[t:t + 8, :]

    ext_ref[8:8 + t, 0:gw] = xs_ref[...].astype(F32)
    ext_ref[8:8 + t, gw:gw + n] = b_ref[...].astype(F32)
    ext_ref[8:8 + t, gw + n:gw + 2 * n] = c_ref[...].astype(F32)

    def conv(lo, hi, w_ref, bias_ref):
        acc = bias_ref[...]
        for j in range(SSM_CONV):
            off = 8 - (SSM_CONV - 1) + j
            acc = acc + w_ref[j:j + 1, :] * ext_ref[off:off + t, lo:hi]
        return _silu(acc)

    xs = conv(0, gw, cwx_ref, cbx_ref)
    bm = conv(gw, gw + n, cwb_ref, cbb_ref)
    cm = conv(gw + n, gw + 2 * n, cwc_ref, cbc_ref)

    dt = _softplus(dt_ref[...] + dtb_ref[...])
    a = -jnp.exp(alog_ref[...])
    causal = _lower_tri(t)
    tri = jnp.where(causal, 1.0, 0.0).astype(BF16)
    a_cs = _dot_sel_l(tri, dt * a)
    a_last = a_cs[t - 1:t, :]
    e512 = e512_ref[...]
    dt_x = _dot_sel_r(dt, e512)
    ea_x = _dot_sel_r(jnp.exp(a_cs), e512)
    eb_x = _dot_sel_r(jnp.exp(a_last - a_cs), e512)
    a_col = _dot_sel_r(a_cs, e1024_ref[...])
    a_row = a_cs.T

    xdt = xs * dt_x
    xdt_b = xdt.astype(BF16)
    cm_b = cm.astype(BF16)
    cb = _dot_nt(cm_b, bm.astype(BF16))
    lane = lax.broadcasted_iota(jnp.int32, (t, 2 * SSM_HEAD_DIM), 1)
    pieces = []
    for p in range(SSM_HPG // 2):
        x_pair = xdt_b[:, p * 128:(p + 1) * 128]
        res = []
        for i in (2 * p, 2 * p + 1):
            seg = a_col[:, i * 128:(i + 1) * 128] - a_row[i:i + 1, :]
            decay = jnp.exp(jnp.where(causal, seg, -jnp.inf))
            res.append(_dot((cb * decay).astype(BF16), x_pair))
        pieces.append(jnp.where(lane < SSM_HEAD_DIM, res[0], res[1]))
    y_diag = jnp.concatenate(pieces, axis=1)

    state = state_ref[...]
    y_off = ea_x * _dot(cm_b, state.astype(BF16))
    state_ref[...] = state * ea_x[t - 1:t, :] + _dot(bm.T.astype(BF16), (eb_x * xdt).astype(BF16))

    y = y_diag + y_off + xs * dskip_ref[...]
    y = y * _silu(z_ref[...].astype(F32))
    ms = jnp.mean(y * y, axis=-1, keepdims=True)
    o_ref[...] = (y * lax.rsqrt(ms + NORM_EPS) * nw_ref[...]).astype(o_ref.dtype)


def ssd_mixer(main, small, conv_w, conv_b, dt_bias, a_log, d_skip, norm_w, bsz, seq):
    t = SSD_CHUNK
    nc = seq // t
    g = SSM_GROUPS
    gw = SSM_GW
    n = SSM_STATE

    def row(b, gi, c):
        return b * nc + c

    def pad_heads(v):
        return jnp.pad(v.reshape(g, 1, SSM_HPG), ((0, 0), (0, 0), (0, LANES - SSM_HPG)))

    heads = jnp.arange(LANES)[:, None]
    e512 = (heads == (jnp.arange(gw)[None, :] // SSM_HEAD_DIM)).astype(BF16)
    e1024 = (heads == (jnp.arange(SSM_HPG * LANES)[None, :] // LANES)).astype(BF16)
    conv_b2 = conv_b.reshape(1, SSM_XBC)
    xo, bo, co = 0, SSM_INNER // n, (SSM_INNER + g * n) // n

    in_specs = [
        pl.BlockSpec((t, gw), lambda b, gi, c: (row(b, gi, c), OFF_SSM_Z // gw + gi)),
        pl.BlockSpec((t, gw), lambda b, gi, c: (row(b, gi, c), OFF_SSM_X // gw + gi)),
        pl.BlockSpec((t, n), lambda b, gi, c: (row(b, gi, c), OFF_SSM_B // n + gi)),
        pl.BlockSpec((t, n), lambda b, gi, c: (row(b, gi, c), OFF_SSM_C // n + gi)),
        pl.BlockSpec((t, LANES), lambda b, gi, c: (row(b, gi, c), gi)),
        pl.BlockSpec((SSM_CONV, gw), lambda b, gi, c: (0, gi)),
        pl.BlockSpec((SSM_CONV, n), lambda b, gi, c: (0, bo + gi)),
        pl.BlockSpec((SSM_CONV, n), lambda b, gi, c: (0, co + gi)),
        pl.BlockSpec((1, gw), lambda b, gi, c: (0, gi)),
        pl.BlockSpec((1, n), lambda b, gi, c: (0, bo + gi)),
        pl.BlockSpec((1, n), lambda b, gi, c: (0, co + gi)),
        pl.BlockSpec((None, 1, LANES), lambda b, gi, c: (gi, 0, 0)),
        pl.BlockSpec((None, 1, LANES), lambda b, gi, c: (gi, 0, 0)),
        pl.BlockSpec((1, gw), lambda b, gi, c: (0, gi)),
        pl.BlockSpec((1, gw), lambda b, gi, c: (0, gi)),
        pl.BlockSpec((LANES, gw), lambda b, gi, c: (0, 0)),
        pl.BlockSpec((LANES, SSM_HPG * LANES), lambda b, gi, c: (0, 0)),
    ]
    return pl.pallas_call(
        _ssd_kernel,
        out_shape=jax.ShapeDtypeStruct((bsz * seq, SSM_INNER), BF16),
        grid=(bsz, g, nc),
        in_specs=in_specs,
        out_specs=pl.BlockSpec((t, gw), lambda b, gi, c: (row(b, gi, c), gi)),
        scratch_shapes=[pltpu.VMEM((t + 8, gw + 2 * n), F32), pltpu.VMEM((n, gw), F32)],
        compiler_params=_cparams(("parallel", "parallel", "arbitrary")),
        name="ssd_mixer",
    )(main, main, main, main, small, conv_w, conv_w, conv_w, conv_b2, conv_b2, conv_b2,
      pad_heads(dt_bias), pad_heads(a_log),
      jnp.repeat(d_skip, SSM_HEAD_DIM).reshape(1, SSM_INNER), norm_w.reshape(1, SSM_INNER),
      e512, e1024)


def _fox_prep_kernel(f_ref, fb_ref, o_ref, carry_ref):
    t = f_ref.shape[0]

    @pl.when(pl.program_id(1) == 0)
    def _():
        carry_ref[...] = jnp.zeros_like(carry_ref)

    log_f = -_softplus(-(f_ref[...] + fb_ref[...]))
    tri = jnp.where(_lower_tri(t), 1.0, 0.0).astype(BF16)
    cs = _dot_sel_l(tri, log_f) + carry_ref[0:1, :]
    carry_ref[...] = jnp.broadcast_to(cs[t - 1:t, :], carry_ref.shape)
    o_ref[...] = cs.T[0:FOX_HEADS, :]


def fox_cumulative_log_f(small, f_bias, bsz, seq, t=512):
    nc = seq // t
    fb = jnp.pad(f_bias, (0, LANES - FOX_HEADS)).reshape(1, LANES)
    return pl.pallas_call(
        _fox_prep_kernel,
        out_shape=jax.ShapeDtypeStruct((bsz, FOX_HEADS, seq), F32),
        grid=(bsz, nc),
        in_specs=[pl.BlockSpec((t, LANES), lambda b, c: (b * nc + c, SSM_GROUPS)),
                  pl.BlockSpec((1, LANES), lambda b, c: (0, 0))],
        out_specs=pl.BlockSpec((None, FOX_HEADS, t), lambda b, c: (b, 0, c)),
        scratch_shapes=[pltpu.VMEM((8, LANES), F32)],
        compiler_params=_cparams(("parallel", "arbitrary")),
        name="fox_cum_log_f",
    )(small, fb)


def _fox_kernel(q_ref, k_ref, v_ref, cf_ref, o_ref):
    tq = q_ref.shape[0]
    tk = FOX_TK
    d = FOX_HEAD_DIM
    qi = pl.program_id(2)
    q = q_ref[...]
    scale = d ** -0.5
    q_pos = qi * tq + lax.broadcasted_iota(jnp.int32, (tq, tk), 0)
    k_off = lax.broadcasted_iota(jnp.int32, (tq, tk), 1)

    def body(j, carry):
        m, l, acc = carry
        start = pl.multiple_of(j * tk, tk)
        k = k_ref[pl.ds(start, tk), :]
        v = v_ref[pl.ds(start, tk), :]
        s = _dot_nt(q, k) * scale - cf_ref[pl.ds(j, 1), :]
        s = jnp.where(k_off + j * tk <= q_pos, s, -jnp.inf)
        m_new = jnp.maximum(m, jnp.max(s, axis=-1, keepdims=True))
        alpha = jnp.exp(m - m_new)
        p = jnp.exp(s - m_new)
        l = alpha * l + jnp.sum(p, axis=-1, keepdims=True)
        acc = alpha * acc + _dot(p.astype(BF16), v)
        return m_new, l, acc

    n_kv = (qi * tq) // tk + tq // tk
    m0 = jnp.full((tq, 1), -jnp.inf, F32)
    l0 = jnp.zeros((tq, 1), F32)
    acc0 = jnp.zeros((tq, d), F32)
    _, l, acc = lax.fori_loop(0, n_kv, body, (m0, l0, acc0))
    o_ref[...] = (acc / l).astype(o_ref.dtype)


def fox_mixer(main, cum_f, bsz, seq):
    tq = FOX_TQ
    nq = seq // tq
    d = FOX_HEAD_DIM
    cf = cum_f.reshape(bsz * FOX_HEADS, seq // FOX_TK, FOX_TK)
    return pl.pallas_call(
        _fox_kernel,
        out_shape=jax.ShapeDtypeStruct((bsz * seq, FOX_WIDTH), BF16),
        grid=(bsz, FOX_HEADS, nq),
        in_specs=[pl.BlockSpec((tq, d), lambda b, h, i: (b * nq + i, OFF_FOX_Q // d + h)),
                  pl.BlockSpec((seq, d), lambda b, h, i: (b, OFF_FOX_K // d + h)),
                  pl.BlockSpec((seq, d), lambda b, h, i: (b, OFF_FOX_V // d + h)),
                  pl.BlockSpec((None, seq // FOX_TK, FOX_TK), lambda b, h, i: (b * FOX_HEADS + h, 0, 0))],
        out_specs=pl.BlockSpec((tq, d), lambda b, h, i: (b * nq + i, h)),
        compiler_params=_cparams(("parallel", "parallel", "arbitrary")),
        name="fox_attention",
    )(main, main, main, cf)


def _gdn_kernel(q_ref, k_ref, v_ref, z_ref, sm_ref, cwq_ref, cwk_ref, cwv_ref, dtb_ref, alog_ref,
                nw_ref, rep_ref, o_ref, ext_ref, state_ref):
    t = q_ref.shape[0]
    d = GDN_HEAD_DIM
    hb = GDN_HB
    w = hb * d
    chunk = pl.program_id(2)

    @pl.when(chunk == 0)
    def _():
        ext_ref[0:8, :] = jnp.zeros((8, 3 * w), F32)
        state_ref[...] = jnp.zeros_like(state_ref)

    @pl.when(chunk > 0)
    def _():
        ext_ref[0:8, :] = ext_ref[t:t + 8, :]

    ext_ref[8:8 + t, 0:w] = q_ref[...].astype(F32)
    ext_ref[8:8 + t, w:2 * w] = k_ref[...].astype(F32)
    ext_ref[8:8 + t, 2 * w:3 * w] = v_ref[...].astype(F32)

    def conv(lo, hi, w_ref):
        acc = None
        for j in range(GDN_CONV):
            off = 8 - (GDN_CONV - 1) + j
            term = w_ref[j:j + 1, :] * ext_ref[off:off + t, lo:hi]
            acc = term if acc is None else acc + term
        return _silu(acc)

    qc = conv(0, w, cwq_ref)
    kc = conv(w, 2 * w, cwk_ref)
    vc = conv(2 * w, 3 * w, cwv_ref)

    sm = sm_ref[...]
    lane = lax.broadcasted_iota(jnp.int32, sm.shape, 1)
    g_all = -jnp.exp(alog_ref[...]) * _softplus(sm + dtb_ref[...])
    mix = jnp.where(lane < 16, g_all, _sigmoid(sm))
    rep = _dot_sel_r(mix, rep_ref[...])

    lower = _lower_tri(t)
    strict = _lower_tri(t, strict=True)
    tri = jnp.where(lower, 1.0, 0.0).astype(BF16)
    eye = jnp.where(lower, 1.0, 0.0) - jnp.where(strict, 1.0, 0.0)
    z = z_ref[...].astype(F32)

    outs = []
    for h in range(hb):
        sl = slice(h * d, (h + 1) * d)
        q = qc[:, sl]
        k = kc[:, sl]
        v = vc[:, sl]
        q = q * lax.rsqrt(jnp.sum(q * q, axis=-1, keepdims=True) + 1e-6) * (d ** -0.5)
        k = k * lax.rsqrt(jnp.sum(k * k, axis=-1, keepdims=True) + 1e-6)
        g_rep = rep[:, h * d:(h + 1) * d]
        beta = rep[:, (hb + h) * d:(hb + h + 1) * d]
        gc = _dot_sel_l(tri, g_rep)
        decay = jnp.exp(jnp.where(lower, gc - gc.T, -jnp.inf))
        kb = k * beta
        k_b = k.astype(BF16)
        a_mat = jnp.where(strict, _dot_nt(kb.astype(BF16), k_b) * decay, 0.0)
        t_inv = eye - a_mat
        pw = a_mat
        for _ in range(t.bit_length() - 2):
            pw = _dot2(pw, pw)
            t_inv = t_inv + _dot2(t_inv, pw)
        eg = jnp.exp(gc)
        rhs = jnp.concatenate([v * beta, kb * eg], axis=1)
        sol = _dot2(t_inv, rhs)
        u = sol[:, :d]
        wm = sol[:, d:]
        attn = _dot_nt(q.astype(BF16), k_b) * decay
        s_old = state_ref[h]
        s_b = s_old.astype(BF16)
        v_new = u - _dot(wm.astype(BF16), s_b)
        v_new_b = v_new.astype(BF16)
        o = _dot((q * eg).astype(BF16), s_b) + _dot(attn.astype(BF16), v_new_b)
        g_last = gc[t - 1:t, :]
        k_dec = k * jnp.exp(g_last - gc)
        state_ref[h] = s_old * jnp.exp(g_last) + _dot(k_dec.T.astype(BF16), v_new_b)
        ms = jnp.mean(o * o, axis=-1, keepdims=True)
        outs.append(o * lax.rsqrt(ms + NORM_EPS) * nw_ref[...] * _silu(z[:, sl]))
    o_ref[...] = jnp.concatenate(outs, axis=1).astype(o_ref.dtype)


def gdn_mixer(main, small, conv_w, dt_bias, a_log, norm_w, bsz, seq):
    t = GDN_CHUNK
    nc = seq // t
    hb = GDN_HB
    d = GDN_HEAD_DIM
    w = hb * d
    ng = GDN_HEADS // hb

    def row(b, hg, c):
        return b * nc + c

    def at_lanes(v, start):
        return jnp.pad(v, (start, LANES - start - GDN_HEADS)).reshape(1, LANES)

    src = jnp.arange(LANES)[None, :, None]
    col_head = (jnp.arange(2 * w) // d)[None, None, :]
    hg_idx = jnp.arange(ng)[:, None, None]
    want = jnp.where(col_head < hb, 8 + hg_idx * hb + col_head, 16 + hg_idx * hb + (col_head - hb))
    rep = (src == want).astype(BF16)

    in_specs = [
        pl.BlockSpec((t, w), lambda b, hg, c: (row(b, hg, c), OFF_GDN_Q // w + hg)),
        pl.BlockSpec((t, w), lambda b, hg, c: (row(b, hg, c), OFF_GDN_K // w + hg)),
        pl.BlockSpec((t, w), lambda b, hg, c: (row(b, hg, c), OFF_GDN_V // w + hg)),
        pl.BlockSpec((t, w), lambda b, hg, c: (row(b, hg, c), OFF_GDN_Z // w + hg)),
        pl.BlockSpec((t, LANES), lambda b, hg, c: (row(b, hg, c), SSM_GROUPS)),
        pl.BlockSpec((GDN_CONV, w), lambda b, hg, c: (0, hg)),
        pl.BlockSpec((GDN_CONV, w), lambda b, hg, c: (0, ng + hg)),
        pl.BlockSpec((GDN_CONV, w), lambda b, hg, c: (0, 2 * ng + hg)),
        pl.BlockSpec((1, LANES), lambda b, hg, c: (0, 0)),
        pl.BlockSpec((1, LANES), lambda b, hg, c: (0, 0)),
        pl.BlockSpec((1, d), lambda b, hg, c: (0, 0)),
        pl.BlockSpec((None, LANES, 2 * w), lambda b, hg, c: (hg, 0, 0)),
    ]
    return pl.pallas_call(
        _gdn_kernel,
        out_shape=jax.ShapeDtypeStruct((bsz * seq, GDN_WIDTH), BF16),
        grid=(bsz, ng, nc),
        in_specs=in_specs,
        out_specs=pl.BlockSpec((t, w), lambda b, hg, c: (row(b, hg, c), hg)),
        scratch_shapes=[pltpu.VMEM((t + 8, 3 * w), F32), pltpu.VMEM((hb, d, d), F32)],
        compiler_params=_cparams(("parallel", "parallel", "arbitrary")),
        name="gdn_mixer",
    )(main, main, main, main, small, conv_w, conv_w, conv_w,
      at_lanes(dt_bias, 8), at_lanes(a_log, 8), norm_w.reshape(1, d), rep)


def _merge_kernel(ys_ref, yf_ref, yg_ref, ws_ref, wf_ref, wg_ref, gs_ref, gf_ref, gg_ref, o_ref):
    acc = _sigmoid(gs_ref[...].astype(F32)) * _dot(ys_ref[...], ws_ref[...])
    acc = acc + _sigmoid(gf_ref[...].astype(F32)) * _dot(yf_ref[...], wf_ref[...])
    acc = acc + _sigmoid(gg_ref[...].astype(F32)) * _dot(yg_ref[...], wg_ref[...])
    o_ref[...] = acc.astype(o_ref.dtype)


def merge_branches(y_ssm, y_fox, y_gdn, w_ssm, w_fox, w_gdn, main, tm=1024, tn=512):
    m = y_ssm.shape[0]
    n = D_MODEL
    gate_blk = OFF_GATE // tn
    per_gate = D_MODEL // tn

    def lhs(width):
        return pl.BlockSpec((tm, width), lambda i, j: (i, 0))

    def rhs(width):
        return pl.BlockSpec((width, tn), lambda i, j: (0, j))

    def gate(idx):
        return pl.BlockSpec((tm, tn), lambda i, j: (i, gate_blk + idx * per_gate + j))

    return pl.pallas_call(
        _merge_kernel,
        out_shape=jax.ShapeDtypeStruct((m, n), BF16),
        grid=(m // tm, n // tn),
        in_specs=[lhs(SSM_INNER), lhs(FOX_WIDTH), lhs(GDN_WIDTH),
                  rhs(SSM_INNER), rhs(FOX_WIDTH), rhs(GDN_WIDTH),
                  gate(0), gate(1), gate(2)],
        out_specs=pl.BlockSpec((tm, tn), lambda i, j: (i, j)),
        compiler_params=_cparams(("parallel", "arbitrary")),
        name="merge_branches",
    )(y_ssm, y_fox, y_gdn, w_ssm, w_fox, w_gdn, main, main, main)


def _residual_matmul_kernel(a_ref, w_ref, x_ref, o_ref):
    kk = pl.program_id(2)

    @pl.when(kk == 0)
    def _():
        o_ref[...] = x_ref[...] + _dot(a_ref[...], w_ref[...])

    @pl.when(kk > 0)
    def _():
        o_ref[...] += _dot(a_ref[...], w_ref[...])


def residual_matmul(a, w, x, tm, tn, tk, name):
    m, k = a.shape
    n = w.shape[1]
    return pl.pallas_call(
        _residual_matmul_kernel,
        out_shape=jax.ShapeDtypeStruct((m, n), F32),
        grid=(m // tm, n // tn, k // tk),
        in_specs=[pl.BlockSpec((tm, tk), lambda i, j, kk: (i, kk)),
                  pl.BlockSpec((tk, tn), lambda i, j, kk: (kk, j)),
                  pl.BlockSpec((tm, tn), lambda i, j, kk: (i, j))],
        out_specs=pl.BlockSpec((tm, tn), lambda i, j, kk: (i, j)),
        compiler_params=_cparams(("parallel", "parallel", "arbitrary")),
        name=name,
    )(a, w, x)


def _ffn_up_kernel(halo_ref, h_ref, wg_ref, wv_ref, cwg_ref, cwv_ref, cbg_ref, cbv_ref, o_ref,
                   lhs_ref, ug_ref, uv_ref, *, tiles_per_seq):
    tm = h_ref.shape[0]
    i = pl.program_id(0)

    @pl.when(pl.program_id(1) == 0)
    def _():
        first = (i % tiles_per_seq) == 0
        halo = halo_ref[...]
        lhs_ref[0:HALO, :] = jnp.where(first, jnp.zeros_like(halo), halo)
        lhs_ref[HALO:HALO + tm, :] = h_ref[...]

    lhs = lhs_ref[...]
    ug_ref[...] = _dot(lhs, wg_ref[...])
    uv_ref[...] = _dot(lhs, wv_ref[...])

    def conv(u_ref, w_ref, b_ref):
        acc = b_ref[...]
        for j in range(FFN_CONV):
            off = HALO - (FFN_CONV - 1) + j
            acc = acc + w_ref[j:j + 1, :] * u_ref[off:off + tm, :]
        return acc

    gate = conv(ug_ref, cwg_ref, cbg_ref)
    val = conv(uv_ref, cwv_ref, cbv_ref)
    o_ref[...] = (_silu(gate) * val).astype(o_ref.dtype)


def ffn_up(h, w_up, conv_w, conv_b, seq, tm=1024, tn=512):
    m, k = h.shape
    nb = D_FF_PAD // tn
    kern = functools.partial(_ffn_up_kernel, tiles_per_seq=seq // tm)
    return pl.pallas_call(
        kern,
        out_shape=jax.ShapeDtypeStruct((m, D_FF_PAD), BF16),
        grid=(m // tm, nb),
        in_specs=[pl.BlockSpec((HALO, k), lambda i, j: (jnp.maximum(i * (tm // HALO) - 1, 0), 0)),
                  pl.BlockSpec((tm, k), lambda i, j: (i, 0)),
                  pl.BlockSpec((k, tn), lambda i, j: (0, j)),
                  pl.BlockSpec((k, tn), lambda i, j: (0, nb + j)),
                  pl.BlockSpec((FFN_CONV, tn), lambda i, j: (0, j)),
                  pl.BlockSpec((FFN_CONV, tn), lambda i, j: (0, nb + j)),
                  pl.BlockSpec((1, tn), lambda i, j: (0, j)),
                  pl.BlockSpec((1, tn), lambda i, j: (0, nb + j))],
        out_specs=pl.BlockSpec((tm, tn), lambda i, j: (i, j)),
        scratch_shapes=[pltpu.VMEM((HALO + tm, k), BF16),
                        pltpu.VMEM((HALO + tm, tn), F32),
                        pltpu.VMEM((HALO + tm, tn), F32)],
        compiler_params=_cparams(("parallel", "arbitrary")),
        name="ffn_up",
    )(h, h, w_up, w_up, conv_w, conv_w, conv_b, conv_b)


def _split_w_in(w_in):
    sizes = [SSM_INNER, SSM_XBC, SSM_HEADS, 3 * FOX_WIDTH, FOX_HEADS, 3 * GDN_WIDTH,
             GDN_HEADS, GDN_HEADS, GDN_WIDTH, D_MODEL, D_MODEL, D_MODEL]
    offs = [0]
    for s in sizes:
        offs.append(offs[-1] + s)
    seg = [w_in[:, offs[i]:offs[i + 1]] for i in range(len(sizes))]
    (ssm_z, ssm_xbc, ssm_dt, fox_qkv, fox_f, gdn_qkv, gdn_a, gdn_b, gdn_z, g_ssm, g_fox, g_gdn) = seg
    main = jnp.concatenate([ssm_z, ssm_xbc, fox_qkv, gdn_qkv, gdn_z, g_ssm, g_fox, g_gdn],
                           axis=1).astype(BF16)
    d = w_in.shape[0]
    blocks = []
    for gi in range(SSM_GROUPS):
        blocks.append(jnp.pad(ssm_dt[:, gi * SSM_HPG:(gi + 1) * SSM_HPG], ((0, 0), (0, LANES - SSM_HPG))))
    last = jnp.concatenate([fox_f, gdn_a, gdn_b], axis=1)
    blocks.append(jnp.pad(last, ((0, 0), (0, LANES - last.shape[1]))))
    small = jnp.concatenate(blocks, axis=1).astype(BF16)
    return main, small


def _pad_ffn(w_up, conv_w, conv_b, w_down):
    pad = D_FF_PAD - D_FF

    def pad_cols(a):
        g, v = a[..., :D_FF], a[..., D_FF:]
        cfg = [(0, 0)] * (a.ndim - 1) + [(0, pad)]
        return jnp.concatenate([jnp.pad(g, cfg), jnp.pad(v, cfg)], axis=-1)

    return (pad_cols(w_up).astype(BF16), pad_cols(conv_w), pad_cols(conv_b.reshape(1, -1)),
            jnp.pad(w_down, ((0, pad), (0, 0))).astype(BF16))


def kernel(x, norm_mix, w_in, ssm_conv_w, ssm_conv_b, ssm_dt_bias, ssm_a_log, ssm_d, ssm_norm,
           fox_f_bias, gdn_conv_w, gdn_dt_bias, gdn_a_log, gdn_norm, w_br_ssm, w_br_fox, w_br_gdn,
           w_out, norm_ffn, w_up, ffn_conv_w, ffn_conv_b, w_down, norm_final):
    bsz, seq, d = x.shape
    depth = w_in.shape[0]
    m = bsz * seq
    xf = x.reshape(m, d)
    for layer in range(depth):
        w_main, w_small = _split_w_in(w_in[layer])
        h = rmsnorm(xf, norm_mix[layer], BF16)
        main = matmul(h, w_main, BF16, 1024, 1024, "in_proj_main")
        small = matmul(h, w_small, F32, 1024, N_SMALL, "in_proj_small")
        y_ssm = ssd_mixer(main, small, ssm_conv_w[layer], ssm_conv_b[layer], ssm_dt_bias[layer],
                          ssm_a_log[layer], ssm_d[layer], ssm_norm[layer], bsz, seq)
        cum_f = fox_cumulative_log_f(small, fox_f_bias[layer], bsz, seq)
        y_fox = fox_mixer(main, cum_f, bsz, seq)
        y_gdn = gdn_mixer(main, small, gdn_conv_w[layer], gdn_dt_bias[layer], gdn_a_log[layer],
                          gdn_norm[layer], bsz, seq)
        merged = merge_branches(y_ssm, y_fox, y_gdn, w_br_ssm[layer].astype(BF16),
                                w_br_fox[layer].astype(BF16), w_br_gdn[layer].astype(BF16), main)
        xf = residual_matmul(merged, w_out[layer].astype(BF16), xf, 1024, 512, d, "out_proj")
        h2 = rmsnorm(xf, norm_ffn[layer], BF16)
        w_up_p, cw_p, cb_p, w_down_p = _pad_ffn(w_up[layer], ffn_conv_w[layer], ffn_conv_b[layer],
                                                w_down[layer])
        act = ffn_up(h2, w_up_p, cw_p, cb_p, seq)
        xf = residual_matmul(act, w_down_p, xf, 1024, 1024, D_FF_PAD // 4, "ffn_down")
    return rmsnorm(xf, norm_final, F32).reshape(bsz, seq, d)
```

```python
import functools

import jax
import jax.numpy as jnp
from jax import lax
from jax.experimental import pallas as pl
from jax.experimental.pallas import tpu as pltpu

F32 = jnp.float32
BF16 = jnp.bfloat16

D_MODEL = 4096
SSM_INNER = 2048
SSM_HEAD_DIM = 64
SSM_HEADS = 32
SSM_GROUPS = 4
SSM_STATE = 128
SSM_CONV = 4
SSM_XBC = SSM_INNER + 2 * SSM_GROUPS * SSM_STATE
SSM_HPG = SSM_HEADS // SSM_GROUPS
SSM_GW = SSM_HPG * SSM_HEAD_DIM
FOX_HEADS = 8
FOX_HEAD_DIM = 128
FOX_WIDTH = FOX_HEADS * FOX_HEAD_DIM
GDN_HEADS = 8
GDN_HEAD_DIM = 128
GDN_WIDTH = GDN_HEADS * GDN_HEAD_DIM
GDN_CONV = 4
D_FF = 11008
D_FF_PAD = 11264
FFN_CONV = 3
NORM_EPS = 1e-6
LOG2_E = 1.4426950408889634
N_MAIN = 24576
N_SMALL = 640
LANES = 128
HALO = 16

OFF_SSM_Z = 0
OFF_SSM_X = 2048
OFF_SSM_B = 4096
OFF_SSM_C = 4608
OFF_FOX_Q = 5120
OFF_FOX_K = 6144
OFF_FOX_V = 7168
OFF_GDN_Q = 8192
OFF_GDN_K = 9216
OFF_GDN_V = 10240
OFF_GDN_Z = 11264
OFF_GATE = 12288

SSD_CHUNK = 128
GDN_CHUNK = 128
GDN_HB = 8
FOX_TQ = 512
FOX_TK = 512
VMEM_LIMIT = 56 * 1024 * 1024


def _cparams(sem):
    return pltpu.CompilerParams(dimension_semantics=sem, vmem_limit_bytes=VMEM_LIMIT)


def _sigmoid(x):
    return 1.0 / (1.0 + jnp.exp(-x))


def _silu(x):
    return x * _sigmoid(x)


def _softplus(x):
    return jnp.maximum(x, 0.0) + jnp.log(1.0 + jnp.exp(-jnp.abs(x)))


def _split3(x):
    hi = x.astype(BF16)
    r1 = x - hi.astype(F32)
    mid = r1.astype(BF16)
    lo = (r1 - mid.astype(F32)).astype(BF16)
    return hi, mid, lo


def _dot(a, b):
    return jnp.dot(a, b, preferred_element_type=F32)


def _dot_nt(a, b):
    return lax.dot_general(a, b, (((1,), (1,)), ((), ())), preferred_element_type=F32)


def _dot_sel_r(x, sel):
    hi, mid, lo = _split3(x)
    return (_dot(lo, sel) + _dot(mid, sel)) + _dot(hi, sel)


def _dot_sel_l(sel, x):
    hi, mid, lo = _split3(x)
    return (_dot(sel, lo) + _dot(sel, mid)) + _dot(sel, hi)


def _dot2(a, b):
    ah = a.astype(BF16)
    al = (a - ah.astype(F32)).astype(BF16)
    bh = b.astype(BF16)
    bl = (b - bh.astype(F32)).astype(BF16)
    return (_dot(al, bh) + _dot(ah, bl)) + _dot(ah, bh)


def _lower_tri(n, strict=False):
    r = lax.broadcasted_iota(jnp.int32, (n, n), 0)
    c = lax.broadcasted_iota(jnp.int32, (n, n), 1)
    return (r > c) if strict else (r >= c)


def _rmsnorm_kernel(x_ref, w_ref, o_ref):
    x = x_ref[...]
    ms = jnp.mean(x * x, axis=-1, keepdims=True)
    o_ref[...] = (x * lax.rsqrt(ms + NORM_EPS) * w_ref[...]).astype(o_ref.dtype)


def rmsnorm(x, w, out_dtype, tm=512):
    m, d = x.shape
    return pl.pallas_call(
        _rmsnorm_kernel,
        out_shape=jax.ShapeDtypeStruct((m, d), out_dtype),
        grid=(m // tm,),
        in_specs=[pl.BlockSpec((tm, d), lambda i: (i, 0)),
                  pl.BlockSpec((1, d), lambda i: (0, 0))],
        out_specs=pl.BlockSpec((tm, d), lambda i: (i, 0)),
        compiler_params=_cparams(("parallel",)),
        name="rmsnorm",
    )(x, w.reshape(1, d))


def _matmul_kernel(a_ref, w_ref, o_ref):
    o_ref[...] = _dot(a_ref[...], w_ref[...]).astype(o_ref.dtype)


def matmul(a, w, out_dtype, tm, tn, name):
    m, k = a.shape
    n = w.shape[1]
    return pl.pallas_call(
        _matmul_kernel,
        out_shape=jax.ShapeDtypeStruct((m, n), out_dtype),
        grid=(m // tm, n // tn),
        in_specs=[pl.BlockSpec((tm, k), lambda i, j: (i, 0)),
                  pl.BlockSpec((k, tn), lambda i, j: (0, j))],
        out_specs=pl.BlockSpec((tm, tn), lambda i, j: (i, j)),
        compiler_params=_cparams(("parallel", "arbitrary")),
        name=name,
    )(a, w)


def _ssd_kernel(z_ref, xs_ref, b_ref, c_ref, dt_ref, cwx_ref, cwb_ref, cwc_ref,
                cbx_ref, cbb_ref, cbc_ref, dtb_ref, alog_ref, dskip_ref, nw_ref,
                e512_ref, e1024_ref, o_ref, ext_ref, state_ref):
    t = z_ref.shape[0]
    n = SSM_STATE
    gw = SSM_GW
    chunk = pl.program_id(2)

    @pl.when(chunk == 0)
    def _():
        ext_ref[0:8, :] = jnp.zeros((8, gw + 2 * n), F32)
        state_ref[...] = jnp.zeros_like(state_ref)

    @pl.when(chunk > 0)
    def _():
        ext_ref[0:8, :] = ext_ref[t:t + 8, :]

    ext_ref[8:8 + t, 0:gw] = xs_ref[...].astype(F32)
    ext_ref[8:8 + t, gw:gw + n] = b_ref[...].astype(F32)
    ext_ref[8:8 + t, gw + n:gw + 2 * n] = c_ref[...].astype(F32)

    def conv(lo, hi, w_ref, bias_ref):
        acc = bias_ref[...]
        for j in range(SSM_CONV):
            off = 8 - (SSM_CONV - 1) + j
            acc = acc + w_ref[j:j + 1, :] * ext_ref[off:off + t, lo:hi]
        return _silu(acc)

    xs = conv(0, gw, cwx_ref, cbx_ref)
    bm = conv(gw, gw + n, cwb_ref, cbb_ref)
    cm = conv(gw + n, gw + 2 * n, cwc_ref, cbc_ref)

    dt = _softplus(dt_ref[...] + dtb_ref[...])
    a = -jnp.exp(alog_ref[...])
    causal = _lower_tri(t)
    tri = jnp.where(causal, 1.0, 0.0).astype(BF16)
    a_cs = _dot_sel_l(tri, dt * a)
    a_last = a_cs[t - 1:t, :]
    e512 = e512_ref[...]
    dt_x = _dot_sel_r(dt, e512)
    ea_x = _dot_sel_r(jnp.exp(a_cs), e512)
    eb_x = _dot_sel_r(jnp.exp(a_last - a_cs), e512)
    a_col = _dot_sel_r(a_cs, e1024_ref[...])
    a_row = a_cs.T

    xdt = xs * dt_x
    xdt_b = xdt.astype(BF16)
    cm_b = cm.astype(BF16)
    cb = _dot_nt(cm_b, bm.astype(BF16))
    lane = lax.broadcasted_iota(jnp.int32, (t, 2 * SSM_HEAD_DIM), 1)
    pieces = []
    for p in range(SSM_HPG // 2):
        x_pair = xdt_b[:, p * 128:(p + 1) * 128]
        res = []
        for i in (2 * p, 2 * p + 1):
            seg = a_col[:, i * 128:(i + 1) * 128] - a_row[i:i + 1, :]
            decay = jnp.exp(jnp.where(causal, seg, -jnp.inf))
            res.append(_dot((cb * decay).astype(BF16), x_pair))
        pieces.append(jnp.where(lane < SSM_HEAD_DIM, res[0], res[1]))
    y_diag = jnp.concatenate(pieces, axis=1)

    state = state_ref[...]
    y_off = ea_x * _dot(cm_b, state.astype(BF16))
    state_ref[...] = state * ea_x[t - 1:t, :] + _dot(bm.T.astype(BF16), (eb_x * xdt).astype(BF16))

    y = y_diag + y_off + xs * dskip_ref[...]
    y = y * _silu(z_ref[...].astype(F32))
    ms = jnp.mean(y * y, axis=-1, keepdims=True)
    o_ref[...] = (y * lax.rsqrt(ms + NORM_EPS) * nw_ref[...]).astype(o_ref.dtype)


def ssd_mixer(main, small, conv_w, conv_b, dt_bias, a_log, d_skip, norm_w, bsz, seq):
    t = SSD_CHUNK
    nc = seq // t
    g = SSM_GROUPS
    gw = SSM_GW
    n = SSM_STATE

    def row(b, gi, c):
        return b * nc + c

    def pad_heads(v):
        return jnp.pad(v.reshape(g, 1, SSM_HPG), ((0, 0), (0, 0), (0, LANES - SSM_HPG)))

    heads = jnp.arange(LANES)[:, None]
    e512 = (heads == (jnp.arange(gw)[None, :] // SSM_HEAD_DIM)).astype(BF16)
    e1024 = (heads == (jnp.arange(SSM_HPG * LANES)[None, :] // LANES)).astype(BF16)
    conv_b2 = conv_b.reshape(1, SSM_XBC)
    xo, bo, co = 0, SSM_INNER // n, (SSM_INNER + g * n) // n

    in_specs = [
        pl.BlockSpec((t, gw), lambda b, gi, c: (row(b, gi, c), OFF_SSM_Z // gw + gi)),
        pl.BlockSpec((t, gw), lambda b, gi, c: (row(b, gi, c), OFF_SSM_X // gw + gi)),
        pl.BlockSpec((t, n), lambda b, gi, c: (row(b, gi, c), OFF_SSM_B // n + gi)),
        pl.BlockSpec((t, n), lambda b, gi, c: (row(b, gi, c), OFF_SSM_C // n + gi)),
        pl.BlockSpec((t, LANES), lambda b, gi, c: (row(b, gi, c), gi)),
        pl.BlockSpec((SSM_CONV, gw), lambda b, gi, c: (0, gi)),
        pl.BlockSpec((SSM_CONV, n), lambda b, gi, c: (0, bo + gi)),
        pl.BlockSpec((SSM_CONV, n), lambda b, gi, c: (0, co + gi)),
        pl.BlockSpec((1, gw), lambda b, gi, c: (0, gi)),
        pl.BlockSpec((1, n), lambda b, gi, c: (0, bo + gi)),
        pl.BlockSpec((1, n), lambda b, gi, c: (0, co + gi)),
        pl.BlockSpec((None, 1, LANES), lambda b, gi, c: (gi, 0, 0)),
        pl.BlockSpec((None, 1, LANES), lambda b, gi, c: (gi, 0, 0)),
        pl.BlockSpec((1, gw), lambda b, gi, c: (0, gi)),
        pl.BlockSpec((1, gw), lambda b, gi, c: (0, gi)),
        pl.BlockSpec((LANES, gw), lambda b, gi, c: (0, 0)),
        pl.BlockSpec((LANES, SSM_HPG * LANES), lambda b, gi, c: (0, 0)),
    ]
    return pl.pallas_call(
        _ssd_kernel,
        out_shape=jax.ShapeDtypeStruct((bsz * seq, SSM_INNER), BF16),
        grid=(bsz, g, nc),
        in_specs=in_specs,
        out_specs=pl.BlockSpec((t, gw), lambda b, gi, c: (row(b, gi, c), gi)),
        scratch_shapes=[pltpu.VMEM((t + 8, gw + 2 * n), F32), pltpu.VMEM((n, gw), F32)],
        compiler_params=_cparams(("parallel", "parallel", "arbitrary")),
        name="ssd_mixer",
    )(main, main, main, main, small, conv_w, conv_w, conv_w, conv_b2, conv_b2, conv_b2,
      pad_heads(dt_bias), pad_heads(a_log),
      jnp.repeat(d_skip, SSM_HEAD_DIM).reshape(1, SSM_INNER), norm_w.reshape(1, SSM_INNER),
      e512, e1024)


def _fox_prep_kernel(f_ref, fb_ref, o_ref, carry_ref):
    t = f_ref.shape[0]

    @pl.when(pl.program_id(1) == 0)
    def _():
        carry_ref[...] = jnp.zeros_like(carry_ref)

    log_f = -_softplus(-(f_ref[...] + fb_ref[...]))
    tri = jnp.where(_lower_tri(t), 1.0, 0.0).astype(BF16)
    cs = _dot_sel_l(tri, log_f) + carry_ref[0:1, :]
    carry_ref[...] = jnp.broadcast_to(cs[t - 1:t, :], carry_ref.shape)
    o_ref[...] = cs.T[0:FOX_HEADS, :]


def fox_cumulative_log_f(small, f_bias, bsz, seq, t=512):
    nc = seq // t
    fb = jnp.pad(f_bias, (0, LANES - FOX_HEADS)).reshape(1, LANES)
    return pl.pallas_call(
        _fox_prep_kernel,
        out_shape=jax.ShapeDtypeStruct((bsz, FOX_HEADS, seq), F32),
        grid=(bsz, nc),
        in_specs=[pl.BlockSpec((t, LANES), lambda b, c: (b * nc + c, SSM_GROUPS)),
                  pl.BlockSpec((1, LANES), lambda b, c: (0, 0))],
        out_specs=pl.BlockSpec((None, FOX_HEADS, t), lambda b, c: (b, 0, c)),
        scratch_shapes=[pltpu.VMEM((8, LANES), F32)],
        compiler_params=_cparams(("parallel", "arbitrary")),
        name="fox_cum_log_f",
    )(small, fb)


def _fox_kernel(q_ref, k_ref, v_ref, cf_ref, o_ref):
    tq = q_ref.shape[0]
    tk = FOX_TK
    d = FOX_HEAD_DIM
    qi = pl.program_id(2)
    q = q_ref[...]
    scale2 = (d ** -0.5) * LOG2_E

    def step(j, carry, masked):
        m, l, acc = carry
        start = pl.multiple_of(j * tk, tk)
        k = k_ref[pl.ds(start, tk), :]
        v = v_ref[pl.ds(start, tk), :]
        s = _dot_nt(q, k) * scale2 - cf_ref[pl.ds(j, 1), :] * LOG2_E
        if masked:
            s = jnp.where(_lower_tri(tq), s, -jnp.inf)
        m_new = jnp.maximum(m, jnp.max(s, axis=-1, keepdims=True))
        alpha = jnp.exp2(m - m_new)
        p = jnp.exp2(s - m_new)
        l = alpha * l + jnp.sum(p, axis=-1, keepdims=True)
        acc = alpha * acc + _dot(p.astype(BF16), v)
        return m_new, l, acc

    m0 = jnp.full((tq, 1), -jnp.inf, F32)
    l0 = jnp.zeros((tq, 1), F32)
    acc0 = jnp.zeros((tq, d), F32)
    carry = lax.fori_loop(0, qi, lambda j, c: step(j, c, False), (m0, l0, acc0))
    _, l, acc = step(qi, carry, True)
    o_ref[...] = (acc / l).astype(o_ref.dtype)


def fox_mixer(main, cum_f, bsz, seq):
    tq = FOX_TQ
    assert FOX_TQ == FOX_TK
    nq = seq // tq
    d = FOX_HEAD_DIM
    cf = cum_f.reshape(bsz * FOX_HEADS, seq // FOX_TK, FOX_TK)
    return pl.pallas_call(
        _fox_kernel,
        out_shape=jax.ShapeDtypeStruct((bsz * seq, FOX_WIDTH), BF16),
        grid=(bsz, FOX_HEADS, nq),
        in_specs=[pl.BlockSpec((tq, d), lambda b, h, i: (b * nq + i, OFF_FOX_Q // d + h)),
                  pl.BlockSpec((seq, d), lambda b, h, i: (b, OFF_FOX_K // d + h)),
                  pl.BlockSpec((seq, d), lambda b, h, i: (b, OFF_FOX_V // d + h)),
                  pl.BlockSpec((None, seq // FOX_TK, FOX_TK), lambda b, h, i: (b * FOX_HEADS + h, 0, 0))],
        out_specs=pl.BlockSpec((tq, d), lambda b, h, i: (b * nq + i, h)),
        compiler_params=_cparams(("parallel", "parallel", "arbitrary")),
        name="fox_attention",
    )(main, main, main, cf)


INV_BASE = 32


def _unit_lower_inverse(a, blk_xor, eye):
    t = a[0].shape[0]
    n = range(len(a))
    pw = [jnp.where(blk_xor == 0, ai, 0.0) for ai in a]
    x = [eye - p for p in pw]
    for _ in range(INV_BASE.bit_length() - 2):
        pw = [_dot2(p, p) for p in pw]
        x = [x[i] + _dot2(x[i], pw[i]) for i in n]
    level = 1
    while INV_BASE * level < t:
        sel = (blk_xor >= level) & (blk_xor < 2 * level)
        xb = [xi.astype(BF16) for xi in x]
        xo = [_dot(xb[i], jnp.where(sel, a[i], 0.0).astype(BF16)).astype(BF16) for i in n]
        x = [x[i] - _dot(xo[i], xb[i]) for i in n]
        level *= 2
    return x


def _gdn_kernel(q_ref, k_ref, v_ref, z_ref, sm_ref, cwq_ref, cwk_ref, cwv_ref, dtb_ref, alog_ref,
                nw_ref, rep_ref, o_ref, ext_ref, state_ref):
    t = q_ref.shape[0]
    d = GDN_HEAD_DIM
    hb = GDN_HB
    w = hb * d
    chunk = pl.program_id(2)

    @pl.when(chunk == 0)
    def _():
        ext_ref[0:8, :] = jnp.zeros((8, 3 * w), F32)
        state_ref[...] = jnp.zeros_like(state_ref)

    @pl.when(chunk > 0)
    def _():
        ext_ref[0:8, :] = ext_ref[t:t + 8, :]

    ext_ref[8:8 + t, 0:w] = q_ref[...].astype(F32)
    ext_ref[8:8 + t, w:2 * w] = k_ref[...].astype(F32)
    ext_ref[8:8 + t, 2 * w:3 * w] = v_ref[...].astype(F32)

    def conv(lo, hi, w_ref):
        acc = None
        for j in range(GDN_CONV):
            off = 8 - (GDN_CONV - 1) + j
            term = w_ref[j:j + 1, :] * ext_ref[off:off + t, lo:hi]
            acc = term if acc is None else acc + term
        return _silu(acc)

    qc = conv(0, w, cwq_ref)
    kc = conv(w, 2 * w, cwk_ref)
    vc = conv(2 * w, 3 * w, cwv_ref)

    sm = sm_ref[...]
    lane = lax.broadcasted_iota(jnp.int32, sm.shape, 1)
    g_all = -jnp.exp(alog_ref[...]) * _softplus(sm + dtb_ref[...])
    mix = jnp.where(lane < 16, g_all, _sigmoid(sm))
    rep = _dot_sel_r(mix, rep_ref[...])

    lower = _lower_tri(t)
    strict = _lower_tri(t, strict=True)
    tri = jnp.where(lower, 1.0, 0.0).astype(BF16)
    eye = jnp.where(lower, 1.0, 0.0) - jnp.where(strict, 1.0, 0.0)
    rows = lax.broadcasted_iota(jnp.int32, (t, t), 0)
    cols = lax.broadcasted_iota(jnp.int32, (t, t), 1)
    shift = INV_BASE.bit_length() - 1
    blk_xor = jnp.right_shift(rows, shift) ^ jnp.right_shift(cols, shift)
    z = z_ref[...].astype(F32)
    gc_all =_dot_sel_l(tri, rep[:, 0:w])

    hs = range(hb)
    sls = [slice(h * d, (h + 1) * d) for h in hs]

    def l2n(x):
        return x * lax.rsqrt(jnp.sum(x * x, axis=-1, keepdims=True) + 1e-6)

    q = [l2n(qc[:, sl]) * (d ** -0.5) for sl in sls]
    k = [l2n(kc[:, sl]) for sl in sls]
    beta = [rep[:, (hb + h) * d:(hb + h + 1) * d] for h in hs]
    gc = [gc_all[:, sl] for sl in sls]
    decay = [jnp.exp(jnp.where(lower, g - g.T, -jnp.inf)) for g in gc]
    eg = [jnp.exp(g) for g in gc]
    kb = [k[h] * beta[h] for h in hs]
    k_b = [x.astype(BF16) for x in k]
    a_mat = [jnp.where(strict, _dot_nt(kb[h].astype(BF16), k_b[h]) * decay[h], 0.0) for h in hs]
    attn = [(_dot_nt(q[h].astype(BF16), k_b[h]) * decay[h]).astype(BF16) for h in hs]
    rhs = [jnp.concatenate([vc[:, sls[h]] * beta[h], kb[h] * eg[h]], axis=1).astype(BF16) for h in hs]
    t_inv = _unit_lower_inverse(a_mat, blk_xor, eye)
    sol = [_dot(t_inv[h].astype(BF16), rhs[h]) for h in hs]
    s_old = [state_ref[h] for h in hs]
    s_b = [s.astype(BF16) for s in s_old]
    v_new = [(sol[h][:, :d] - _dot(sol[h][:, d:].astype(BF16), s_b[h])).astype(BF16) for h in hs]
    o = [_dot((q[h] * eg[h]).astype(BF16), s_b[h]) + _dot(attn[h], v_new[h]) for h in hs]
    g_last = [g[t - 1:t, :] for g in gc]
    k_dec = [(k[h] * jnp.exp(g_last[h] - gc[h])).T.astype(BF16) for h in hs]
    for h in hs:
        state_ref[h] = s_old[h] * jnp.exp(g_last[h]) + _dot(k_dec[h], v_new[h])
    nw = nw_ref[...]
    outs = [o[h] * lax.rsqrt(jnp.mean(o[h] * o[h], axis=-1, keepdims=True) + NORM_EPS) * nw
            * _silu(z[:, sls[h]]) for h in hs]
    o_ref[...] = jnp.concatenate(outs, axis=1).astype(o_ref.dtype)


def gdn_mixer(main, small, conv_w, dt_bias, a_log, norm_w, bsz, seq):
    t = GDN_CHUNK
    nc = seq // t
    hb = GDN_HB
    d = GDN_HEAD_DIM
    w = hb * d
    ng = GDN_HEADS // hb

    def row(b, hg, c):
        return b * nc + c

    def at_lanes(v, start):
        return jnp.pad(v, (start, LANES - start - GDN_HEADS)).reshape(1, LANES)

    src = jnp.arange(LANES)[None, :, None]
    col_head = (jnp.arange(2 * w) // d)[None, None, :]
    hg_idx = jnp.arange(ng)[:, None, None]
    want = jnp.where(col_head < hb, 8 + hg_idx * hb + col_head, 16 + hg_idx * hb + (col_head - hb))
    rep = (src == want).astype(BF16)

    in_specs = [
        pl.BlockSpec((t, w), lambda b, hg, c: (row(b, hg, c), OFF_GDN_Q // w + hg)),
        pl.BlockSpec((t, w), lambda b, hg, c: (row(b, hg, c), OFF_GDN_K // w + hg)),
        pl.BlockSpec((t, w), lambda b, hg, c: (row(b, hg, c), OFF_GDN_V // w + hg)),
        pl.BlockSpec((t, w), lambda b, hg, c: (row(b, hg, c), OFF_GDN_Z // w + hg)),
        pl.BlockSpec((t, LANES), lambda b, hg, c: (row(b, hg, c), SSM_GROUPS)),
        pl.BlockSpec((GDN_CONV, w), lambda b, hg, c: (0, hg)),
        pl.BlockSpec((GDN_CONV, w), lambda b, hg, c: (0, ng + hg)),
        pl.BlockSpec((GDN_CONV, w), lambda b, hg, c: (0, 2 * ng + hg)),
        pl.BlockSpec((1, LANES), lambda b, hg, c: (0, 0)),
        pl.BlockSpec((1, LANES), lambda b, hg, c: (0, 0)),
        pl.BlockSpec((1, d), lambda b, hg, c: (0, 0)),
        pl.BlockSpec((None, LANES, 2 * w), lambda b, hg, c: (hg, 0, 0)),
    ]
    return pl.pallas_call(
        _gdn_kernel,
        out_shape=jax.ShapeDtypeStruct((bsz * seq, GDN_WIDTH), BF16),
        grid=(bsz, ng, nc),
        in_specs=in_specs,
        out_specs=pl.BlockSpec((t, w), lambda b, hg, c: (row(b, hg, c), hg)),
        scratch_shapes=[pltpu.VMEM((t + 8, 3 * w), F32), pltpu.VMEM((hb, d, d), F32)],
        compiler_params=_cparams(("parallel", "parallel", "arbitrary")),
        name="gdn_mixer",
    )(main, main, main, main, small, conv_w, conv_w, conv_w,
      at_lanes(dt_bias, 8), at_lanes(a_log, 8), norm_w.reshape(1, d), rep)


def _merge_kernel(ys_ref, yf_ref, yg_ref, ws_ref, wf_ref, wg_ref, gs_ref, gf_ref, gg_ref, o_ref):
    acc = _sigmoid(gs_ref[...].astype(F32)) * _dot(ys_ref[...], ws_ref[...])
    acc = acc + _sigmoid(gf_ref[...].astype(F32)) * _dot(yf_ref[...], wf_ref[...])
    acc = acc + _sigmoid(gg_ref[...].astype(F32)) * _dot(yg_ref[...], wg_ref[...])
    o_ref[...] = acc.astype(o_ref.dtype)


def merge_branches(y_ssm, y_fox, y_gdn, w_ssm, w_fox, w_gdn, main, tm=1024, tn=512):
    m = y_ssm.shape[0]
    n = D_MODEL
    gate_blk = OFF_GATE // tn
    per_gate = D_MODEL // tn

    def lhs(width):
        return pl.BlockSpec((tm, width), lambda i, j: (i, 0))

    def rhs(width):
        return pl.BlockSpec((width, tn), lambda i, j: (0, j))

    def gate(idx):
        return pl.BlockSpec((tm, tn), lambda i, j: (i, gate_blk + idx * per_gate + j))

    return pl.pallas_call(
        _merge_kernel,
        out_shape=jax.ShapeDtypeStruct((m, n), BF16),
        grid=(m // tm, n // tn),
        in_specs=[lhs(SSM_INNER), lhs(FOX_WIDTH), lhs(GDN_WIDTH),
                  rhs(SSM_INNER), rhs(FOX_WIDTH), rhs(GDN_WIDTH),
                  gate(0), gate(1), gate(2)],
        out_specs=pl.BlockSpec((tm, tn), lambda i, j: (i, j)),
        compiler_params=_cparams(("parallel", "arbitrary")),
        name="merge_branches",
    )(y_ssm, y_fox, y_gdn, w_ssm, w_fox, w_gdn, main, main, main)


def _residual_matmul_kernel(a_ref, w_ref, x_ref, o_ref):
    kk = pl.program_id(2)

    @pl.when(kk == 0)
    def _():
        o_ref[...] = x_ref[...] + _dot(a_ref[...], w_ref[...])

    @pl.when(kk > 0)
    def _():
        o_ref[...] += _dot(a_ref[...], w_ref[...])


def residual_matmul(a, w, x, tm, tn, tk, name):
    m, k = a.shape
    n = w.shape[1]
    return pl.pallas_call(
        _residual_matmul_kernel,
        out_shape=jax.ShapeDtypeStruct((m, n), F32),
        grid=(m // tm, n // tn, k // tk),
        in_specs=[pl.BlockSpec((tm, tk), lambda i, j, kk: (i, kk)),
                  pl.BlockSpec((tk, tn), lambda i, j, kk: (kk, j)),
                  pl.BlockSpec((tm, tn), lambda i, j, kk: (i, j))],
        out_specs=pl.BlockSpec((tm, tn), lambda i, j, kk: (i, j)),
        compiler_params=_cparams(("parallel", "parallel", "arbitrary")),
        name=name,
    )(a, w, x)


def _ffn_up_kernel(h_ref, wg_ref, wv_ref, cwg_ref, cwv_ref, cbg_ref, cbv_ref, o_ref,
                   ug_ref, uv_ref, tg_ref, tv_ref, *, tiles_per_seq):
    tm = h_ref.shape[0]
    i = pl.program_id(0)
    j = pl.program_id(1)

    @pl.when(i == 0)
    def _():
        tg_ref[j] = jnp.zeros(tg_ref.shape[1:], F32)
        tv_ref[j] = jnp.zeros(tv_ref.shape[1:], F32)

    first = (i % tiles_per_seq) == 0
    h = h_ref[...]

    def conv_product(u_ref, t_ref, w_ref, cw_ref, cb_ref):
        u_ref[0:HALO, :] = jnp.where(first, 0.0, t_ref[j])
        u_ref[HALO:HALO + tm, :] = _dot(h, w_ref[...])
        t_ref[j] = u_ref[tm:tm + HALO, :]
        acc = cb_ref[...]
        for tap in range(FFN_CONV):
            off = HALO - (FFN_CONV - 1) + tap
            acc = acc + cw_ref[tap:tap + 1, :] * u_ref[off:off + tm, :]
        return acc

    gate = conv_product(ug_ref, tg_ref, wg_ref, cwg_ref, cbg_ref)
    val = conv_product(uv_ref, tv_ref, wv_ref, cwv_ref, cbv_ref)
    o_ref[...] = (_silu(gate) * val).astype(o_ref.dtype)


def ffn_up(h, w_gate, w_val, cw_gate, cw_val, cb_gate, cb_val, seq, tm=1024, tn=512):
    m, k = h.shape
    nb = D_FF_PAD // tn
    kern = functools.partial(_ffn_up_kernel, tiles_per_seq=seq // tm)

    def col_spec(rows):
        return pl.BlockSpec((rows, tn), lambda i, j: (0, j))

    return pl.pallas_call(
        kern,
        out_shape=jax.ShapeDtypeStruct((m, D_FF_PAD), BF16),
        grid=(m // tm, nb),
        in_specs=[pl.BlockSpec((tm, k), lambda i, j: (i, 0)),
                  col_spec(k), col_spec(k), col_spec(FFN_CONV), col_spec(FFN_CONV), col_spec(1), col_spec(1)],
        out_specs=pl.BlockSpec((tm, tn), lambda i, j: (i, j)),
        scratch_shapes=[pltpu.VMEM((HALO + tm, tn), F32)] * 2
                       + [pltpu.VMEM((nb, HALO, tn), F32)] * 2,
        compiler_params=_cparams(("arbitrary", "arbitrary")),
        name="ffn_up",
    )(h, w_gate, w_val, cw_gate, cw_val, cb_gate, cb_val)


def _split_w_in(w_in):
    sizes = [SSM_INNER, SSM_XBC, SSM_HEADS, 3 * FOX_WIDTH, FOX_HEADS, 3 * GDN_WIDTH,
             GDN_HEADS, GDN_HEADS, GDN_WIDTH, D_MODEL, D_MODEL, D_MODEL]
    offs = [0]
    for s in sizes:
        offs.append(offs[-1] + s)
    seg = [w_in[:, offs[i]:offs[i + 1]] for i in range(len(sizes))]
    (_, _, ssm_dt, fox_qkv, fox_f, gdn_qkv, gdn_a, gdn_b, _, _, _, _) = seg
    main = jnp.concatenate([w_in[:, :offs[2]].astype(BF16), fox_qkv.astype(BF16), gdn_qkv.astype(BF16),
                            w_in[:, offs[8]:].astype(BF16)], axis=1)
    blocks = []
    for gi in range(SSM_GROUPS):
        blocks.append(jnp.pad(ssm_dt[:, gi * SSM_HPG:(gi + 1) * SSM_HPG], ((0, 0), (0, LANES - SSM_HPG))))
    last = jnp.concatenate([fox_f, gdn_a, gdn_b], axis=1)
    blocks.append(jnp.pad(last, ((0, 0), (0, LANES - last.shape[1]))))
    small = jnp.concatenate(blocks, axis=1).astype(BF16)
    return main, small


def _pad_ffn(w_up, conv_w, conv_b, w_down):
    pad = D_FF_PAD - D_FF

    def halves(a, dtype):
        return tuple(jnp.pad(p.astype(dtype), ((0, 0), (0, pad))) for p in (a[:, :D_FF], a[:, D_FF:]))

    return (halves(w_up, BF16), halves(conv_w, F32), halves(conv_b.reshape(1, -1), F32),
            jnp.pad(w_down.astype(BF16), ((0, pad), (0, 0))))


def kernel(x, norm_mix, w_in, ssm_conv_w, ssm_conv_b, ssm_dt_bias, ssm_a_log, ssm_d, ssm_norm,
           fox_f_bias, gdn_conv_w, gdn_dt_bias, gdn_a_log, gdn_norm, w_br_ssm, w_br_fox, w_br_gdn,
           w_out, norm_ffn, w_up, ffn_conv_w, ffn_conv_b, w_down, norm_final):
    bsz, seq, d = x.shape
    depth = w_in.shape[0]
    m = bsz * seq
    xf = x.reshape(m, d)
    for layer in range(depth):
        w_main, w_small = _split_w_in(w_in[layer])
        h = rmsnorm(xf, norm_mix[layer], BF16)
        main = matmul(h, w_main, BF16, 1024, 1024, "in_proj_main")
        small = matmul(h, w_small, F32, 1024, N_SMALL, "in_proj_small")
        y_ssm = ssd_mixer(main, small, ssm_conv_w[layer], ssm_conv_b[layer], ssm_dt_bias[layer],
                          ssm_a_log[layer], ssm_d[layer], ssm_norm[layer], bsz, seq)
        cum_f = fox_cumulative_log_f(small, fox_f_bias[layer], bsz, seq)
        y_fox = fox_mixer(main, cum_f, bsz, seq)
        y_gdn = gdn_mixer(main, small, gdn_conv_w[layer], gdn_dt_bias[layer], gdn_a_log[layer],
                          gdn_norm[layer], bsz, seq)
        merged = merge_branches(y_ssm, y_fox, y_gdn, w_br_ssm[layer].astype(BF16),
                                w_br_fox[layer].astype(BF16), w_br_gdn[layer].astype(BF16), main)
        xf = residual_matmul(merged, w_out[layer].astype(BF16), xf, 1024, 512, d, "out_proj")
        h2 = rmsnorm(xf, norm_ffn[layer], BF16)
        w_up_p, cw_p, cb_p, w_down_p = _pad_ffn(w_up[layer], ffn_conv_w[layer], ffn_conv_b[layer],
                                                w_down[layer])
        act = ffn_up(h2, *w_up_p, *cw_p, *cb_p, seq)
        xf = residual_matmul(act, w_down_p, xf, 1024, 1024, D_FF_PAD // 4, "ffn_down")
    return rmsnorm(xf, norm_final, F32).reshape(bsz, seq, d)
```

```python
import functools

import jax
import jax.numpy as jnp
from jax import lax
from jax.experimental import pallas as pl
from jax.experimental.pallas import tpu as pltpu

F32 = jnp.float32
BF16 = jnp.bfloat16

D_MODEL = 4096
SSM_INNER = 2048
SSM_HEAD_DIM = 64
SSM_HEADS = 32
SSM_GROUPS = 4
SSM_STATE = 128
SSM_CONV = 4
SSM_XBC = SSM_INNER + 2 * SSM_GROUPS * SSM_STATE
SSM_HPG = SSM_HEADS // SSM_GROUPS
SSM_GW = SSM_HPG * SSM_HEAD_DIM
FOX_HEADS = 8
FOX_HEAD_DIM = 128
FOX_WIDTH = FOX_HEADS * FOX_HEAD_DIM
GDN_HEADS = 8
GDN_HEAD_DIM = 128
GDN_WIDTH = GDN_HEADS * GDN_HEAD_DIM
GDN_CONV = 4
D_FF = 11008
FFN_CONV = 3
NORM_EPS = 1e-6
LOG2_E = 1.4426950408889634
N_MAIN = 24576
LANES = 128
HALO = 16

OFF_SSM_Z = 0
OFF_SSM_X = 2048
OFF_SSM_B = 4096
OFF_SSM_C = 4608
OFF_FOX_Q = 5120
OFF_FOX_K = 6144
OFF_FOX_V = 7168
OFF_GDN_Q = 8192
OFF_GDN_K = 9216
OFF_GDN_V = 10240
OFF_GDN_Z = 11264
OFF_GATE = 12288
MAIN_SEGMENTS = ((0, 0), (OFF_FOX_Q, SSM_HEADS), (OFF_GDN_Q, SSM_HEADS + FOX_HEADS),
                 (OFF_GDN_Z, SSM_HEADS + FOX_HEADS + 2 * GDN_HEADS))
SMALL_BLOCKS = (OFF_FOX_Q // LANES, (OFF_GDN_Q + SSM_HEADS) // LANES,
                (OFF_GDN_Z + SSM_HEADS + FOX_HEADS) // LANES)
LANE_SSM_DT = 0
LANE_FOX_F = SSM_HEADS
LANE_GDN_A = SSM_HEADS + FOX_HEADS
LANE_GDN_B = SSM_HEADS + FOX_HEADS + GDN_HEADS
N_SMALL = len(SMALL_BLOCKS) * LANES

SSD_CHUNK = 128
GDN_CHUNK = 128
GDN_HB = 8
FOX_TQ = 512
FOX_TK = 512
TM = 1024
VMEM_LIMIT = 56 * 1024 * 1024


def _cparams(sem):
    return pltpu.CompilerParams(dimension_semantics=sem, vmem_limit_bytes=VMEM_LIMIT)


def _sigmoid(x):
    return 1.0 / (1.0 + jnp.exp(-x))


def _silu(x):
    return x * _sigmoid(x)


def _softplus(x):
    return jnp.maximum(x, 0.0) + jnp.log(1.0 + jnp.exp(-jnp.abs(x)))


def _split3(x):
    hi = x.astype(BF16)
    r1 = x - hi.astype(F32)
    mid = r1.astype(BF16)
    lo = (r1 - mid.astype(F32)).astype(BF16)
    return hi, mid, lo


def _dot(a, b):
    return jnp.dot(a, b, preferred_element_type=F32)


def _dot_nt(a, b):
    return lax.dot_general(a, b, (((1,), (1,)), ((), ())), preferred_element_type=F32)


def _dot_sel_r(x, sel):
    hi, mid, lo = _split3(x)
    return (_dot(lo, sel) + _dot(mid, sel)) + _dot(hi, sel)


def _dot_sel_l(sel, x):
    hi, mid, lo = _split3(x)
    return (_dot(sel, lo) + _dot(sel, mid)) + _dot(sel, hi)


def _dot2(a, b):
    ah = a.astype(BF16)
    al = (a - ah.astype(F32)).astype(BF16)
    bh = b.astype(BF16)
    bl = (b - bh.astype(F32)).astype(BF16)
    return (_dot(al, bh) + _dot(ah, bl)) + _dot(ah, bh)


def _lower_tri(n, strict=False):
    r = lax.broadcasted_iota(jnp.int32, (n, n), 0)
    c = lax.broadcasted_iota(jnp.int32, (n, n), 1)
    return (r > c) if strict else (r >= c)


def _at_lanes(v, start):
    return jnp.pad(v, (start, LANES - start - v.shape[0])).reshape(1, LANES)


def _rmsnorm_kernel(x_ref, w_ref, o_ref):
    x = x_ref[...]
    ms = jnp.mean(x * x, axis=-1, keepdims=True)
    o_ref[...] = (x * lax.rsqrt(ms + NORM_EPS) * w_ref[...]).astype(o_ref.dtype)


def rmsnorm(x, w, out_dtype, tm=512):
    m, d = x.shape
    return pl.pallas_call(
        _rmsnorm_kernel,
        out_shape=jax.ShapeDtypeStruct((m, d), out_dtype),
        grid=(m // tm,),
        in_specs=[pl.BlockSpec((tm, d), lambda i: (i, 0)),
                  pl.BlockSpec((1, d), lambda i: (0, 0))],
        out_specs=pl.BlockSpec((tm, d), lambda i: (i, 0)),
        compiler_params=_cparams(("parallel",)),
        name="rmsnorm",
    )(x, w.reshape(1, d))


CAST_ROWS = 256


def _in_proj_kernel(a_ref, w_ref, wx_ref, o_ref, wb_ref, *, seg_blocks):
    j = pl.program_id(0)
    k, tn = w_ref.shape

    for lo, hi, shift in seg_blocks:
        @pl.when((pl.program_id(1) == 0) & (j >= lo) & (j < hi))
        def _(shift=shift):
            def body(r, carry):
                r0 = pl.multiple_of(r * CAST_ROWS, CAST_ROWS)
                w = w_ref[pl.ds(r0, CAST_ROWS), :]
                if shift:
                    both = jnp.concatenate([w, wx_ref[pl.ds(r0, CAST_ROWS), :]], axis=1)
                    w = both[:, shift:shift + tn]
                wb_ref[pl.ds(r0, CAST_ROWS), :] = w.astype(BF16)
                return carry
            lax.fori_loop(0, k // CAST_ROWS, body, 0)

    o_ref[...] = _dot(a_ref[...], wb_ref[...]).astype(o_ref.dtype)


def in_proj_main(h, w_in, layer, tn=512):
    m, k = h.shape
    starts = [s for s, _ in MAIN_SEGMENTS] + [N_MAIN]
    seg_blocks = tuple((starts[i] // tn, starts[i + 1] // tn, MAIN_SEGMENTS[i][1])
                       for i in range(len(MAIN_SEGMENTS)))
    assert all(s % tn == 0 for s in starts)
    return pl.pallas_call(
        functools.partial(_in_proj_kernel, seg_blocks=seg_blocks),
        out_shape=jax.ShapeDtypeStruct((m, N_MAIN), BF16),
        grid=(N_MAIN // tn, m // TM),
        in_specs=[pl.BlockSpec((TM, k), lambda j, i: (i, 0)),
                  pl.BlockSpec((None, k, tn), lambda j, i: (layer, 0, j)),
                  pl.BlockSpec((None, k, LANES), lambda j, i: (layer, 0, (j + 1) * (tn // LANES)))],
        out_specs=pl.BlockSpec((TM, tn), lambda j, i: (i, j)),
        scratch_shapes=[pltpu.VMEM((k, tn), BF16)],
        compiler_params=_cparams(("arbitrary", "arbitrary")),
        name="in_proj_main",
    )(h, w_in, w_in)


def _in_proj_small_kernel(a_ref, w0_ref, w1_ref, w2_ref, o_ref):
    w = jnp.concatenate([w0_ref[...], w1_ref[...], w2_ref[...]], axis=1).astype(BF16)
    o_ref[...] = _dot(a_ref[...], w)


def in_proj_small(h, w_in, layer):
    m, k = h.shape

    def w_spec(blk):
        return pl.BlockSpec((None, k, LANES), lambda i: (layer, 0, blk))

    return pl.pallas_call(
        _in_proj_small_kernel,
        out_shape=jax.ShapeDtypeStruct((m, N_SMALL), F32),
        grid=(m // TM,),
        in_specs=[pl.BlockSpec((TM, k), lambda i: (i, 0))] + [w_spec(b) for b in SMALL_BLOCKS],
        out_specs=pl.BlockSpec((TM, N_SMALL), lambda i: (i, 0)),
        compiler_params=_cparams(("parallel",)),
        name="in_proj_small",
    )(h, w_in, w_in, w_in)


def _ssd_kernel(z_ref, xs_ref, b_ref, c_ref, dt_ref, cw_ref, cb_ref, dtb_ref, alog_ref, dskip_ref,
                nw_ref, ex_ref, ecol_ref, o_ref, ext_ref, state_ref):
    t = z_ref.shape[0]
    n = SSM_STATE
    gw = SSM_GW
    g = SSM_GROUPS
    chunk = pl.program_id(1)

    @pl.when(chunk == 0)
    def _():
        ext_ref[0:8, :] = jnp.zeros((8, SSM_XBC), F32)
        state_ref[...] = jnp.zeros_like(state_ref)

    @pl.when(chunk > 0)
    def _():
        ext_ref[0:8, :] = ext_ref[t:t + 8, :]

    ext_ref[8:8 + t, 0:SSM_INNER] = xs_ref[...].astype(F32)
    ext_ref[8:8 + t, SSM_INNER:SSM_INNER + g * n] = b_ref[...].astype(F32)
    ext_ref[8:8 + t, SSM_INNER + g * n:SSM_XBC] = c_ref[...].astype(F32)

    def conv(lo, hi):
        acc = cb_ref[:, lo:hi]
        for j in range(SSM_CONV):
            off = 8 - (SSM_CONV - 1) + j
            acc = acc + cw_ref[j:j + 1, lo:hi] * ext_ref[off:off + t, lo:hi]
        return _silu(acc)

    dt = _softplus(dt_ref[...] + dtb_ref[...])
    a = -jnp.exp(alog_ref[...])
    causal = _lower_tri(t)
    tri = jnp.where(causal, 1.0, 0.0).astype(BF16)
    a_cs = _dot_sel_l(tri, dt * a)
    a_last = a_cs[t - 1:t, :]
    ex = ex_ref[...]
    dt_x = _dot_sel_r(dt, ex)
    ea_x = _dot_sel_r(jnp.exp(a_cs), ex)
    eb_x = _dot_sel_r(jnp.exp(a_last - a_cs), ex)
    a_col = _dot_sel_r(a_cs, ecol_ref[...])
    a_row = a_cs.T
    lane = lax.broadcasted_iota(jnp.int32, (t, 2 * SSM_HEAD_DIM), 1)
    z = z_ref[...].astype(F32)

    outs = []
    for gi in range(g):
        cols = slice(gi * gw, (gi + 1) * gw)
        xs = conv(gi * gw, (gi + 1) * gw)
        bm = conv(SSM_INNER + gi * n, SSM_INNER + (gi + 1) * n)
        cm = conv(SSM_INNER + (g + gi) * n, SSM_INNER + (g + gi + 1) * n)
        xdt = xs * dt_x[:, cols]
        xdt_b = xdt.astype(BF16)
        cm_b = cm.astype(BF16)
        cb = _dot_nt(cm_b, bm.astype(BF16))
        pieces = []
        for p in range(SSM_HPG // 2):
            x_pair = xdt_b[:, p * 128:(p + 1) * 128]
            res = []
            for i in (2 * p, 2 * p + 1):
                hd = gi * SSM_HPG + i
                seg = a_col[:, hd * 128:(hd + 1) * 128] - a_row[hd:hd + 1, :]
                decay = jnp.exp(jnp.where(causal, seg, -jnp.inf))
                res.append(_dot((cb * decay).astype(BF16), x_pair))
            pieces.append(jnp.where(lane < SSM_HEAD_DIM, res[0], res[1]))
        y_diag = jnp.concatenate(pieces, axis=1)

        state = state_ref[gi]
        y_off = ea_x[:, cols] * _dot(cm_b, state.astype(BF16))
        state_ref[gi] = (state * ea_x[t - 1:t, cols]
                         + _dot(bm.T.astype(BF16), (eb_x[:, cols] * xdt).astype(BF16)))

        y = y_diag + y_off + xs * dskip_ref[:, cols]
        y = y * _silu(z[:, cols])
        ms = jnp.mean(y * y, axis=-1, keepdims=True)
        outs.append(y * lax.rsqrt(ms + NORM_EPS) * nw_ref[:, cols])
    o_ref[...] = jnp.concatenate(outs, axis=1).astype(o_ref.dtype)


def ssd_mixer(main, small, conv_w, conv_b, dt_bias, a_log, d_skip, norm_w, bsz, seq):
    t = SSD_CHUNK
    nc = seq // t
    gn = SSM_GROUPS * SSM_STATE

    def row(b, c):
        return b * nc + c

    heads = jnp.arange(LANES)[:, None]
    e_x = (heads == (jnp.arange(SSM_INNER)[None, :] // SSM_HEAD_DIM)).astype(BF16)
    e_col = (heads == (jnp.arange(SSM_HEADS * LANES)[None, :] // LANES)).astype(BF16)

    def full(shape):
        return pl.BlockSpec(shape, lambda b, c: (0,) * len(shape))

    in_specs = [
        pl.BlockSpec((t, SSM_INNER), lambda b, c: (row(b, c), OFF_SSM_Z // SSM_INNER)),
        pl.BlockSpec((t, SSM_INNER), lambda b, c: (row(b, c), OFF_SSM_X // SSM_INNER)),
        pl.BlockSpec((t, gn), lambda b, c: (row(b, c), OFF_SSM_B // gn)),
        pl.BlockSpec((t, gn), lambda b, c: (row(b, c), OFF_SSM_C // gn)),
        pl.BlockSpec((t, LANES), lambda b, c: (row(b, c), 0)),
        full((SSM_CONV, SSM_XBC)), full((1, SSM_XBC)), full((1, LANES)), full((1, LANES)),
        full((1, SSM_INNER)), full((1, SSM_INNER)), full((LANES, SSM_INNER)), full((LANES, SSM_HEADS * LANES)),
    ]
    return pl.pallas_call(
        _ssd_kernel,
        out_shape=jax.ShapeDtypeStruct((bsz * seq, SSM_INNER), BF16),
        grid=(bsz, nc),
        in_specs=in_specs,
        out_specs=pl.BlockSpec((t, SSM_INNER), lambda b, c: (row(b, c), 0)),
        scratch_shapes=[pltpu.VMEM((t + 8, SSM_XBC), F32),
                        pltpu.VMEM((SSM_GROUPS, SSM_STATE, SSM_GW), F32)],
        compiler_params=_cparams(("parallel", "arbitrary")),
        name="ssd_mixer",
    )(main, main, main, main, small, conv_w, conv_b.reshape(1, SSM_XBC),
      _at_lanes(dt_bias, LANE_SSM_DT), _at_lanes(a_log, LANE_SSM_DT),
      jnp.repeat(d_skip, SSM_HEAD_DIM).reshape(1, SSM_INNER), norm_w.reshape(1, SSM_INNER), e_x, e_col)


def _fox_prep_kernel(f_ref, fb_ref, o_ref, carry_ref):
    t = f_ref.shape[0]

    @pl.when(pl.program_id(1) == 0)
    def _():
        carry_ref[...] = jnp.zeros_like(carry_ref)

    log_f = -_softplus(-(f_ref[...] + fb_ref[...]))
    tri = jnp.where(_lower_tri(t), 1.0, 0.0).astype(BF16)
    cs = _dot_sel_l(tri, log_f) + carry_ref[0:1, :]
    carry_ref[...] = jnp.broadcast_to(cs[t - 1:t, :], carry_ref.shape)
    o_ref[...] = cs.T[LANE_FOX_F:LANE_FOX_F + FOX_HEADS, :]


def fox_cumulative_log_f(small, f_bias, bsz, seq, t=512):
    nc = seq // t
    return pl.pallas_call(
        _fox_prep_kernel,
        out_shape=jax.ShapeDtypeStruct((bsz, FOX_HEADS, seq), F32),
        grid=(bsz, nc),
        in_specs=[pl.BlockSpec((t, LANES), lambda b, c: (b * nc + c, 1)),
                  pl.BlockSpec((1, LANES), lambda b, c: (0, 0))],
        out_specs=pl.BlockSpec((None, FOX_HEADS, t), lambda b, c: (b, 0, c)),
        scratch_shapes=[pltpu.VMEM((8, LANES), F32)],
        compiler_params=_cparams(("parallel", "arbitrary")),
        name="fox_cum_log_f",
    )(small, _at_lanes(f_bias, LANE_FOX_F))


def _fox_kernel(q_ref, k_ref, v_ref, cf_ref, o_ref):
    tq = q_ref.shape[0]
    tk = FOX_TK
    d = FOX_HEAD_DIM
    qi = pl.program_id(2)
    q = q_ref[...]
    scale2 = (d ** -0.5) * LOG2_E

    def step(j, carry, masked):
        m, l, acc = carry
        start = pl.multiple_of(j * tk, tk)
        k = k_ref[pl.ds(start, tk), :]
        v = v_ref[pl.ds(start, tk), :]
        s = _dot_nt(q, k) * scale2 - cf_ref[pl.ds(j, 1), :] * LOG2_E
        if masked:
            s = jnp.where(_lower_tri(tq), s, -jnp.inf)
        m_new = jnp.maximum(m, jnp.max(s, axis=-1, keepdims=True))
        alpha = jnp.exp2(m - m_new)
        p = jnp.exp2(s - m_new)
        l = alpha * l + jnp.sum(p, axis=-1, keepdims=True)
        acc = alpha * acc + _dot(p.astype(BF16), v)
        return m_new, l, acc

    m0 = jnp.full((tq, 1), -jnp.inf, F32)
    l0 = jnp.zeros((tq, 1), F32)
    acc0 = jnp.zeros((tq, d), F32)
    carry = lax.fori_loop(0, qi, lambda j, c: step(j, c, False), (m0, l0, acc0))
    _, l, acc = step(qi, carry, True)
    o_ref[...] = (acc / l).astype(o_ref.dtype)


def fox_mixer(main, cum_f, bsz, seq):
    tq = FOX_TQ
    assert FOX_TQ == FOX_TK
    nq = seq // tq
    d = FOX_HEAD_DIM
    cf = cum_f.reshape(bsz * FOX_HEADS, seq // FOX_TK, FOX_TK)
    return pl.pallas_call(
        _fox_kernel,
        out_shape=jax.ShapeDtypeStruct((bsz * seq, FOX_WIDTH), BF16),
        grid=(bsz, FOX_HEADS, nq),
        in_specs=[pl.BlockSpec((tq, d), lambda b, h, i: (b * nq + i, OFF_FOX_Q // d + h)),
                  pl.BlockSpec((seq, d), lambda b, h, i: (b, OFF_FOX_K // d + h)),
                  pl.BlockSpec((seq, d), lambda b, h, i: (b, OFF_FOX_V // d + h)),
                  pl.BlockSpec((None, seq // FOX_TK, FOX_TK), lambda b, h, i: (b * FOX_HEADS + h, 0, 0))],
        out_specs=pl.BlockSpec((tq, d), lambda b, h, i: (b * nq + i, h)),
        compiler_params=_cparams(("parallel", "parallel", "arbitrary")),
        name="fox_attention",
    )(main, main, main, cf)


INV_BASE = 32


def _unit_lower_inverse(a, blk_xor, eye):
    t = a[0].shape[0]
    n = range(len(a))
    pw = [jnp.where(blk_xor == 0, ai, 0.0) for ai in a]
    x = [eye - p for p in pw]
    for _ in range(INV_BASE.bit_length() - 2):
        pw = [_dot2(p, p) for p in pw]
        x = [x[i] + _dot2(x[i], pw[i]) for i in n]
    level = 1
    while INV_BASE * level < t:
        sel = (blk_xor >= level) & (blk_xor < 2 * level)
        xb = [xi.astype(BF16) for xi in x]
        xo = [_dot(xb[i], jnp.where(sel, a[i], 0.0).astype(BF16)).astype(BF16) for i in n]
        x = [x[i] - _dot(xo[i], xb[i]) for i in n]
        level *= 2
    return x


def _gdn_kernel(q_ref, k_ref, v_ref, z_ref, sm_ref, cwq_ref, cwk_ref, cwv_ref, dtb_ref, alog_ref,
                nw_ref, rep_ref, o_ref, ext_ref, state_ref):
    t = q_ref.shape[0]
    d = GDN_HEAD_DIM
    hb = GDN_HB
    w = hb * d
    chunk = pl.program_id(2)

    @pl.when(chunk == 0)
    def _():
        ext_ref[0:8, :] = jnp.zeros((8, 3 * w), F32)
        state_ref[...] = jnp.zeros_like(state_ref)

    @pl.when(chunk > 0)
    def _():
        ext_ref[0:8, :] = ext_ref[t:t + 8, :]

    ext_ref[8:8 + t, 0:w] = q_ref[...].astype(F32)
    ext_ref[8:8 + t, w:2 * w] = k_ref[...].astype(F32)
    ext_ref[8:8 + t, 2 * w:3 * w] = v_ref[...].astype(F32)

    def conv(lo, hi, w_ref):
        acc = None
        for j in range(GDN_CONV):
            off = 8 - (GDN_CONV - 1) + j
            term = w_ref[j:j + 1, :] * ext_ref[off:off + t, lo:hi]
            acc = term if acc is None else acc + term
        return _silu(acc)

    qc = conv(0, w, cwq_ref)
    kc = conv(w, 2 * w, cwk_ref)
    vc = conv(2 * w, 3 * w, cwv_ref)

    sm = sm_ref[...]
    lane = lax.broadcasted_iota(jnp.int32, sm.shape, 1)
    g_all = -jnp.exp(alog_ref[...]) * _softplus(sm + dtb_ref[...])
    mix = jnp.where(lane < LANE_GDN_B, g_all, _sigmoid(sm))
    rep = _dot_sel_r(mix, rep_ref[...])

    lower = _lower_tri(t)
    strict = _lower_tri(t, strict=True)
    tri = jnp.where(lower, 1.0, 0.0).astype(BF16)
    eye = jnp.where(lower, 1.0, 0.0) - jnp.where(strict, 1.0, 0.0)
    rows = lax.broadcasted_iota(jnp.int32, (t, t), 0)
    cols = lax.broadcasted_iota(jnp.int32, (t, t), 1)
    shift = INV_BASE.bit_length() - 1
    blk_xor = jnp.right_shift(rows, shift) ^ jnp.right_shift(cols, shift)
    z = z_ref[...].astype(F32)
    gc_all = _dot_sel_l(tri, rep[:, 0:w])

    hs = range(hb)
    sls = [slice(h * d, (h + 1) * d) for h in hs]

    def l2n(x):
        return x * lax.rsqrt(jnp.sum(x * x, axis=-1, keepdims=True) + 1e-6)

    q = [l2n(qc[:, sl]) * (d ** -0.5) for sl in sls]
    k = [l2n(kc[:, sl]) for sl in sls]
    beta = [rep[:, (hb + h) * d:(hb + h + 1) * d] for h in hs]
    gc = [gc_all[:, sl] for sl in sls]
    decay = [jnp.exp(jnp.where(lower, g - g.T, -jnp.inf)) for g in gc]
    eg = [jnp.exp(g) for g in gc]
    kb = [k[h] * beta[h] for h in hs]
    k_b = [x.astype(BF16) for x in k]
    a_mat = [jnp.where(strict, _dot_nt(kb[h].astype(BF16), k_b[h]) * decay[h], 0.0) for h in hs]
    attn = [(_dot_nt(q[h].astype(BF16), k_b[h]) * decay[h]).astype(BF16) for h in hs]
    rhs = [jnp.concatenate([vc[:, sls[h]] * beta[h], kb[h] * eg[h]], axis=1).astype(BF16) for h in hs]
    t_inv = _unit_lower_inverse(a_mat, blk_xor, eye)
    sol = [_dot(t_inv[h].astype(BF16), rhs[h]) for h in hs]
    s_old = [state_ref[h] for h in hs]
    s_b = [s.astype(BF16) for s in s_old]
    v_new = [(sol[h][:, :d] - _dot(sol[h][:, d:].astype(BF16), s_b[h])).astype(BF16) for h in hs]
    o = [_dot((q[h] * eg[h]).astype(BF16), s_b[h]) + _dot(attn[h], v_new[h]) for h in hs]
    g_last = [g[t - 1:t, :] for g in gc]
    k_dec = [(k[h] * jnp.exp(g_last[h] - gc[h])).T.astype(BF16) for h in hs]
    for h in hs:
        state_ref[h] = s_old[h] * jnp.exp(g_last[h]) + _dot(k_dec[h], v_new[h])
    nw = nw_ref[...]
    outs = [o[h] * lax.rsqrt(jnp.mean(o[h] * o[h], axis=-1, keepdims=True) + NORM_EPS) * nw
            * _silu(z[:, sls[h]]) for h in hs]
    o_ref[...] = jnp.concatenate(outs, axis=1).astype(o_ref.dtype)


def gdn_mixer(main, small, conv_w, dt_bias, a_log, norm_w, bsz, seq):
    t = GDN_CHUNK
    nc = seq // t
    hb = GDN_HB
    d = GDN_HEAD_DIM
    w = hb * d
    ng = GDN_HEADS // hb

    def row(b, hg, c):
        return b * nc + c

    src = jnp.arange(LANES)[None, :, None]
    col_head = (jnp.arange(2 * w) // d)[None, None, :]
    hg_idx = jnp.arange(ng)[:, None, None]
    want = jnp.where(col_head < hb, LANE_GDN_A + hg_idx * hb + col_head,
                     LANE_GDN_B + hg_idx * hb + (col_head - hb))
    rep = (src == want).astype(BF16)

    in_specs = [
        pl.BlockSpec((t, w), lambda b, hg, c: (row(b, hg, c), OFF_GDN_Q // w + hg)),
        pl.BlockSpec((t, w), lambda b, hg, c: (row(b, hg, c), OFF_GDN_K // w + hg)),
        pl.BlockSpec((t, w), lambda b, hg, c: (row(b, hg, c), OFF_GDN_V // w + hg)),
        pl.BlockSpec((t, w), lambda b, hg, c: (row(b, hg, c), OFF_GDN_Z // w + hg)),
        pl.BlockSpec((t, LANES), lambda b, hg, c: (row(b, hg, c), 2)),
        pl.BlockSpec((GDN_CONV, w), lambda b, hg, c: (0, hg)),
        pl.BlockSpec((GDN_CONV, w), lambda b, hg, c: (0, ng + hg)),
        pl.BlockSpec((GDN_CONV, w), lambda b, hg, c: (0, 2 * ng + hg)),
        pl.BlockSpec((1, LANES), lambda b, hg, c: (0, 0)),
        pl.BlockSpec((1, LANES), lambda b, hg, c: (0, 0)),
        pl.BlockSpec((1, d), lambda b, hg, c: (0, 0)),
        pl.BlockSpec((None, LANES, 2 * w), lambda b, hg, c: (hg, 0, 0)),
    ]
    return pl.pallas_call(
        _gdn_kernel,
        out_shape=jax.ShapeDtypeStruct((bsz * seq, GDN_WIDTH), BF16),
        grid=(bsz, ng, nc),
        in_specs=in_specs,
        out_specs=pl.BlockSpec((t, w), lambda b, hg, c: (row(b, hg, c), hg)),
        scratch_shapes=[pltpu.VMEM((t + 8, 3 * w), F32), pltpu.VMEM((hb, d, d), F32)],
        compiler_params=_cparams(("parallel", "parallel", "arbitrary")),
        name="gdn_mixer",
    )(main, main, main, main, small, conv_w, conv_w, conv_w,
      _at_lanes(dt_bias, LANE_GDN_A), _at_lanes(a_log, LANE_GDN_A), norm_w.reshape(1, d), rep)


def _merge_kernel(ys_ref, yf_ref, yg_ref, ws_ref, wf_ref, wg_ref, gs_ref, gf_ref, gg_ref, o_ref,
                  wsb_ref, wfb_ref, wgb_ref):
    @pl.when(pl.program_id(1) == 0)
    def _():
        wsb_ref[...] = ws_ref[...].astype(BF16)
        wfb_ref[...] = wf_ref[...].astype(BF16)
        wgb_ref[...] = wg_ref[...].astype(BF16)

    acc = _sigmoid(gs_ref[...].astype(F32)) * _dot(ys_ref[...], wsb_ref[...])
    acc = acc + _sigmoid(gf_ref[...].astype(F32)) * _dot(yf_ref[...], wfb_ref[...])
    acc = acc + _sigmoid(gg_ref[...].astype(F32)) * _dot(yg_ref[...], wgb_ref[...])
    o_ref[...] = acc.astype(o_ref.dtype)


def merge_branches(y_ssm, y_fox, y_gdn, w_ssm, w_fox, w_gdn, main, layer, tm=TM, tn=512):
    m = y_ssm.shape[0]
    n = D_MODEL
    gate_blk = OFF_GATE // tn
    per_gate = D_MODEL // tn

    def lhs(width):
        return pl.BlockSpec((tm, width), lambda j, i: (i, 0))

    def rhs(width):
        return pl.BlockSpec((None, width, tn), lambda j, i: (layer, 0, j))

    def gate(idx):
        return pl.BlockSpec((tm, tn), lambda j, i: (i, gate_blk + idx * per_gate + j))

    return pl.pallas_call(
        _merge_kernel,
        out_shape=jax.ShapeDtypeStruct((m, n), BF16),
        grid=(n // tn, m // tm),
        in_specs=[lhs(SSM_INNER), lhs(FOX_WIDTH), lhs(GDN_WIDTH),
                  rhs(SSM_INNER), rhs(FOX_WIDTH), rhs(GDN_WIDTH),
                  gate(0), gate(1), gate(2)],
        out_specs=pl.BlockSpec((tm, tn), lambda j, i: (i, j)),
        scratch_shapes=[pltpu.VMEM((SSM_INNER, tn), BF16), pltpu.VMEM((FOX_WIDTH, tn), BF16),
                        pltpu.VMEM((GDN_WIDTH, tn), BF16)],
        compiler_params=_cparams(("arbitrary", "arbitrary")),
        name="merge_branches",
    )(y_ssm, y_fox, y_gdn, w_ssm, w_fox, w_gdn, main, main, main)


def _out_proj_kernel(a_ref, w_ref, x_ref, o_ref, wb_ref):
    @pl.when(pl.program_id(1) == 0)
    def _():
        wb_ref[...] = w_ref[...].astype(BF16)

    o_ref[...] = x_ref[...] + _dot(a_ref[...], wb_ref[...])


def out_proj(a, w, x, layer, tm=TM, tn=512):
    m, k = a.shape
    n = w.shape[2]
    return pl.pallas_call(
        _out_proj_kernel,
        out_shape=jax.ShapeDtypeStruct((m, n), F32),
        grid=(n // tn, m // tm),
        in_specs=[pl.BlockSpec((tm, k), lambda j, i: (i, 0)),
                  pl.BlockSpec((None, k, tn), lambda j, i: (layer, 0, j)),
                  pl.BlockSpec((tm, tn), lambda j, i: (i, j))],
        out_specs=pl.BlockSpec((tm, tn), lambda j, i: (i, j)),
        scratch_shapes=[pltpu.VMEM((k, tn), BF16)],
        compiler_params=_cparams(("arbitrary", "arbitrary")),
        name="out_proj",
    )(a, w, x)


def _residual_matmul_kernel(a_ref, w_ref, x_ref, o_ref):
    kk = pl.program_id(2)

    @pl.when(kk == 0)
    def _():
        o_ref[...] = x_ref[...] + _dot(a_ref[...], w_ref[...])

    @pl.when(kk > 0)
    def _():
        o_ref[...] += _dot(a_ref[...], w_ref[...])


def ffn_down(a, w, x, layer, tm=TM, tn=512, tk=D_FF // 2):
    m, k = a.shape
    n = w.shape[2]
    return pl.pallas_call(
        _residual_matmul_kernel,
        out_shape=jax.ShapeDtypeStruct((m, n), F32),
        grid=(m // tm, n // tn, k // tk),
        in_specs=[pl.BlockSpec((tm, tk), lambda i, j, kk: (i, kk)),
                  pl.BlockSpec((None, tk, tn), lambda i, j, kk: (layer, kk, j)),
                  pl.BlockSpec((tm, tn), lambda i, j, kk: (i, j))],
        out_specs=pl.BlockSpec((tm, tn), lambda i, j, kk: (i, j)),
        compiler_params=_cparams(("parallel", "parallel", "arbitrary")),
        name="ffn_down",
    )(a, w, x)


def _ffn_up_kernel(h_ref, wg_ref, wv_ref, cwg_ref, cwv_ref, cbg_ref, cbv_ref, o_ref, wb_ref, u_ref,
                   *, tiles_per_seq):
    tm = h_ref.shape[0]
    tn = o_ref.shape[1]
    i = pl.program_id(1)

    @pl.when(i == 0)
    def _():
        wb_ref[:, 0:tn] = wg_ref[...].astype(BF16)
        wb_ref[:, tn:2 * tn] = wv_ref[...].astype(BF16)

    first = (i % tiles_per_seq) == 0

    @pl.when(first)
    def _():
        u_ref[0:HALO, :] = jnp.zeros((HALO, 2 * tn), F32)

    @pl.when(jnp.logical_not(first))
    def _():
        u_ref[0:HALO, :] = u_ref[tm:tm + HALO, :]

    u_ref[HALO:HALO + tm, :] = _dot(h_ref[...], wb_ref[...])

    def conv(lo, w_ref, b_ref):
        acc = b_ref[...]
        for tap in range(FFN_CONV):
            off = HALO - (FFN_CONV - 1) + tap
            acc = acc + w_ref[tap:tap + 1, :] * u_ref[off:off + tm, lo:lo + tn]
        return acc

    gate = conv(0, cwg_ref, cbg_ref)
    val = conv(tn, cwv_ref, cbv_ref)
    o_ref[...] = (_silu(gate) * val).astype(o_ref.dtype)


def ffn_up(h, w_up, conv_w, conv_b, layer, seq, tm=TM, tn=256):
    m, k = h.shape
    nb = D_FF // tn
    kern = functools.partial(_ffn_up_kernel, tiles_per_seq=seq // tm)

    def cols(rows, half):
        return pl.BlockSpec((None, rows, tn), lambda j, i: (layer, 0, half * nb + j))

    return pl.pallas_call(
        kern,
        out_shape=jax.ShapeDtypeStruct((m, D_FF), BF16),
        grid=(nb, m // tm),
        in_specs=[pl.BlockSpec((tm, k), lambda j, i: (i, 0)),
                  cols(k, 0), cols(k, 1), cols(FFN_CONV, 0), cols(FFN_CONV, 1), cols(1, 0), cols(1, 1)],
        out_specs=pl.BlockSpec((tm, tn), lambda j, i: (i, j)),
        scratch_shapes=[pltpu.VMEM((k, 2 * tn), BF16), pltpu.VMEM((HALO + tm, 2 * tn), F32)],
        compiler_params=_cparams(("arbitrary", "arbitrary")),
        name="ffn_up",
    )(h, w_up, w_up, conv_w, conv_w, conv_b, conv_b)


def kernel(x, norm_mix, w_in, ssm_conv_w, ssm_conv_b, ssm_dt_bias, ssm_a_log, ssm_d, ssm_norm,
           fox_f_bias, gdn_conv_w, gdn_dt_bias, gdn_a_log, gdn_norm, w_br_ssm, w_br_fox, w_br_gdn,
           w_out, norm_ffn, w_up, ffn_conv_w, ffn_conv_b, w_down, norm_final):
    bsz, seq, d = x.shape
    depth = w_in.shape[0]
    m = bsz * seq
    xf = x.reshape(m, d)
    w_down_b = w_down.astype(BF16)
    ffn_conv_b3 = ffn_conv_b.reshape(depth, 1, 2 * D_FF)
    for layer in range(depth):
        h = rmsnorm(xf, norm_mix[layer], BF16)
        main = in_proj_main(h, w_in, layer)
        small = in_proj_small(h, w_in, layer)
        y_ssm = ssd_mixer(main, small, ssm_conv_w[layer], ssm_conv_b[layer], ssm_dt_bias[layer],
                          ssm_a_log[layer], ssm_d[layer], ssm_norm[layer], bsz, seq)
        cum_f = fox_cumulative_log_f(small, fox_f_bias[layer], bsz, seq)
        y_fox = fox_mixer(main, cum_f, bsz, seq)
        y_gdn = gdn_mixer(main, small, gdn_conv_w[layer], gdn_dt_bias[layer], gdn_a_log[layer],
                          gdn_norm[layer], bsz, seq)
        merged = merge_branches(y_ssm, y_fox, y_gdn, w_br_ssm, w_br_fox, w_br_gdn, main, layer)
        xf = out_proj(merged, w_out, xf, layer)
        h2 = rmsnorm(xf, norm_ffn[layer], BF16)
        act = ffn_up(h2, w_up, ffn_conv_w, ffn_conv_b3, layer, seq)
        xf = ffn_down(act, w_down_b, xf, layer)
    return rmsnorm(xf, norm_final, F32).reshape(bsz, seq, d)
```

```python
import functools

import jax
import jax.numpy as jnp
from jax import lax
from jax.experimental import pallas as pl
from jax.experimental.pallas import tpu as pltpu

F32 = jnp.float32
BF16 = jnp.bfloat16

D_MODEL = 4096
SSM_INNER = 2048
SSM_HEAD_DIM = 64
SSM_HEADS = 32
SSM_GROUPS = 4
SSM_STATE = 128
SSM_CONV = 4
SSM_XBC = SSM_INNER + 2 * SSM_GROUPS * SSM_STATE
SSM_HPG = SSM_HEADS // SSM_GROUPS
SSM_GW = SSM_HPG * SSM_HEAD_DIM
FOX_HEADS = 8
FOX_HEAD_DIM = 128
FOX_WIDTH = FOX_HEADS * FOX_HEAD_DIM
GDN_HEADS = 8
GDN_HEAD_DIM = 128
GDN_WIDTH = GDN_HEADS * GDN_HEAD_DIM
GDN_CONV = 4
D_FF = 11008
D_FF_PAD = 11264
FFN_CONV = 3
NORM_EPS = 1e-6
LOG2_E = 1.4426950408889634
N_MAIN = 24576
LANES = 128
HALO = 16

OFF_SSM_Z = 0
OFF_SSM_X = 2048
OFF_SSM_B = 4096
OFF_SSM_C = 4608
OFF_FOX_Q = 5120
OFF_FOX_K = 6144
OFF_FOX_V = 7168
OFF_GDN_Q = 8192
OFF_GDN_K = 9216
OFF_GDN_V = 10240
OFF_GDN_Z = 11264
OFF_GATE = 12288
MAIN_SEGMENTS = ((0, 0), (OFF_FOX_Q, SSM_HEADS), (OFF_GDN_Q, SSM_HEADS + FOX_HEADS),
                 (OFF_GDN_Z, SSM_HEADS + FOX_HEADS + 2 * GDN_HEADS))
SMALL_BLOCKS = (OFF_FOX_Q // LANES, (OFF_GDN_Q + SSM_HEADS) // LANES,
                (OFF_GDN_Z + SSM_HEADS + FOX_HEADS) // LANES)
LANE_SSM_DT = 0
LANE_FOX_F = SSM_HEADS
LANE_GDN_A = SSM_HEADS + FOX_HEADS
LANE_GDN_B = SSM_HEADS + FOX_HEADS + GDN_HEADS
N_SMALL = len(SMALL_BLOCKS) * LANES

SSD_CHUNK = 128
GDN_CHUNK = 128
GDN_HB = 8
FOX_TQ = 512
FOX_TK = 512
TM = 1024
VMEM_LIMIT = 56 * 1024 * 1024


def _cparams(sem):
    return pltpu.CompilerParams(dimension_semantics=sem, vmem_limit_bytes=VMEM_LIMIT)


def _sigmoid(x):
    return 1.0 / (1.0 + jnp.exp(-x))


def _silu(x):
    return x * _sigmoid(x)


def _softplus(x):
    return jnp.maximum(x, 0.0) + jnp.log(1.0 + jnp.exp(-jnp.abs(x)))


def _split3(x):
    hi = x.astype(BF16)
    r1 = x - hi.astype(F32)
    mid = r1.astype(BF16)
    lo = (r1 - mid.astype(F32)).astype(BF16)
    return hi, mid, lo


def _dot(a, b):
    return jnp.dot(a, b, preferred_element_type=F32)


def _dot_nt(a, b):
    return lax.dot_general(a, b, (((1,), (1,)), ((), ())), preferred_element_type=F32)


def _dot_sel_r(x, sel):
    hi, mid, lo = _split3(x)
    return (_dot(lo, sel) + _dot(mid, sel)) + _dot(hi, sel)


def _dot_sel_l(sel, x):
    hi, mid, lo = _split3(x)
    return (_dot(sel, lo) + _dot(sel, mid)) + _dot(sel, hi)


def _dot2(a, b):
    ah = a.astype(BF16)
    al = (a - ah.astype(F32)).astype(BF16)
    bh = b.astype(BF16)
    bl = (b - bh.astype(F32)).astype(BF16)
    return (_dot(al, bh) + _dot(ah, bl)) + _dot(ah, bh)


def _lower_tri(n, strict=False):
    r = lax.broadcasted_iota(jnp.int32, (n, n), 0)
    c = lax.broadcasted_iota(jnp.int32, (n, n), 1)
    return (r > c) if strict else (r >= c)


def _at_lanes(v, start):
    return jnp.pad(v, (start, LANES - start - v.shape[0])).reshape(1, LANES)


def _rmsnorm_kernel(x_ref, w_ref, o_ref):
    x = x_ref[...]
    ms = jnp.mean(x * x, axis=-1, keepdims=True)
    o_ref[...] = (x * lax.rsqrt(ms + NORM_EPS) * w_ref[...]).astype(o_ref.dtype)


def rmsnorm(x, w, out_dtype, tm=512):
    m, d = x.shape
    return pl.pallas_call(
        _rmsnorm_kernel,
        out_shape=jax.ShapeDtypeStruct((m, d), out_dtype),
        grid=(m // tm,),
        in_specs=[pl.BlockSpec((tm, d), lambda i: (i, 0)),
                  pl.BlockSpec((1, d), lambda i: (0, 0))],
        out_specs=pl.BlockSpec((tm, d), lambda i: (i, 0)),
        compiler_params=_cparams(("parallel",)),
        name="rmsnorm",
    )(x, w.reshape(1, d))


CAST_ROWS = 256


def _in_proj_kernel(a_ref, w_ref, wx_ref, o_ref, wb_ref, *, seg_blocks):
    j = pl.program_id(0)
    k, tn = w_ref.shape

    for lo, hi, shift in seg_blocks:
        @pl.when((pl.program_id(1) == 0) & (j >= lo) & (j < hi))
        def _(shift=shift):
            def body(r, carry):
                r0 = pl.multiple_of(r * CAST_ROWS, CAST_ROWS)
                w = w_ref[pl.ds(r0, CAST_ROWS), :]
                if shift:
                    both = jnp.concatenate([w, wx_ref[pl.ds(r0, CAST_ROWS), :]], axis=1)
                    w = both[:, shift:shift + tn]
                wb_ref[pl.ds(r0, CAST_ROWS), :] = w
                return carry
            lax.fori_loop(0, k // CAST_ROWS, body, 0)

    o_ref[...] = _dot(a_ref[...], wb_ref[...]).astype(o_ref.dtype)


def in_proj_main(h, w_in, layer, tm=2 * TM, tn=512):
    m, k = h.shape
    starts = [s for s, _ in MAIN_SEGMENTS] + [N_MAIN]
    seg_blocks = tuple((starts[i] // tn, starts[i + 1] // tn, MAIN_SEGMENTS[i][1])
                       for i in range(len(MAIN_SEGMENTS)))
    assert all(s % tn == 0 for s in starts)
    return pl.pallas_call(
        functools.partial(_in_proj_kernel, seg_blocks=seg_blocks),
        out_shape=jax.ShapeDtypeStruct((m, N_MAIN), BF16),
        grid=(N_MAIN // tn, m // tm),
        in_specs=[pl.BlockSpec((tm, k), lambda j, i: (i, 0)),
                  pl.BlockSpec((None, k, tn), lambda j, i: (layer, 0, j)),
                  pl.BlockSpec((None, k, LANES), lambda j, i: (layer, 0, (j + 1) * (tn // LANES)))],
        out_specs=pl.BlockSpec((tm, tn), lambda j, i: (i, j)),
        scratch_shapes=[pltpu.VMEM((k, tn), BF16)],
        compiler_params=_cparams(("arbitrary", "arbitrary")),
        name="in_proj_main",
    )(h, w_in, w_in)


def _in_proj_small_kernel(a_ref, w0_ref, w1_ref, w2_ref, o_ref):
    w = jnp.concatenate([w0_ref[...], w1_ref[...], w2_ref[...]], axis=1)
    o_ref[...] = _dot(a_ref[...], w)


def in_proj_small(h, w_in, layer):
    m, k = h.shape

    def w_spec(blk):
        return pl.BlockSpec((None, k, LANES), lambda i: (layer, 0, blk))

    return pl.pallas_call(
        _in_proj_small_kernel,
        out_shape=jax.ShapeDtypeStruct((m, N_SMALL), F32),
        grid=(m // TM,),
        in_specs=[pl.BlockSpec((TM, k), lambda i: (i, 0))] + [w_spec(b) for b in SMALL_BLOCKS],
        out_specs=pl.BlockSpec((TM, N_SMALL), lambda i: (i, 0)),
        compiler_params=_cparams(("parallel",)),
        name="in_proj_small",
    )(h, w_in, w_in, w_in)


def _ssd_kernel(z_ref, xs_ref, b_ref, c_ref, dt_ref, cw_ref, cb_ref, dtb_ref, alog_ref, dskip_ref,
                nw_ref, ex_ref, ecol_ref, o_ref, ext_ref, state_ref):
    t = z_ref.shape[0]
    n = SSM_STATE
    gw = SSM_GW
    g = SSM_GROUPS
    chunk = pl.program_id(1)

    @pl.when(chunk == 0)
    def _():
        ext_ref[0:8, :] = jnp.zeros((8, SSM_XBC), F32)
        state_ref[...] = jnp.zeros_like(state_ref)

    @pl.when(chunk > 0)
    def _():
        ext_ref[0:8, :] = ext_ref[t:t + 8, :]

    ext_ref[8:8 + t, 0:SSM_INNER] = xs_ref[...].astype(F32)
    ext_ref[8:8 + t, SSM_INNER:SSM_INNER + g * n] = b_ref[...].astype(F32)
    ext_ref[8:8 + t, SSM_INNER + g * n:SSM_XBC] = c_ref[...].astype(F32)

    def conv(lo, hi):
        acc = cb_ref[:, lo:hi]
        for j in range(SSM_CONV):
            off = 8 - (SSM_CONV - 1) + j
            acc = acc + cw_ref[j:j + 1, lo:hi] * ext_ref[off:off + t, lo:hi]
        return _silu(acc)

    dt = _softplus(dt_ref[...] + dtb_ref[...])
    a = -jnp.exp(alog_ref[...])
    causal = _lower_tri(t)
    tri = jnp.where(causal, 1.0, 0.0).astype(BF16)
    a_cs = _dot_sel_l(tri, dt * a)
    a_last = a_cs[t - 1:t, :]
    ex = ex_ref[...]
    dt_x = _dot_sel_r(dt, ex)
    ea_x = _dot_sel_r(jnp.exp(a_cs), ex)
    eb_x = _dot_sel_r(jnp.exp(a_last - a_cs), ex)
    a_col = _dot_sel_r(a_cs, ecol_ref[...])
    a_row = a_cs.T
    lane = lax.broadcasted_iota(jnp.int32, (t, 2 * SSM_HEAD_DIM), 1)
    z = z_ref[...].astype(F32)

    outs = []
    for gi in range(g):
        cols = slice(gi * gw, (gi + 1) * gw)
        xs = conv(gi * gw, (gi + 1) * gw)
        bm = conv(SSM_INNER + gi * n, SSM_INNER + (gi + 1) * n)
        cm = conv(SSM_INNER + (g + gi) * n, SSM_INNER + (g + gi + 1) * n)
        xdt = xs * dt_x[:, cols]
        xdt_b = xdt.astype(BF16)
        cm_b = cm.astype(BF16)
        cb = _dot_nt(cm_b, bm.astype(BF16))
        pieces = []
        for p in range(SSM_HPG // 2):
            x_pair = xdt_b[:, p * 128:(p + 1) * 128]
            res = []
            for i in (2 * p, 2 * p + 1):
                hd = gi * SSM_HPG + i
                seg = a_col[:, hd * 128:(hd + 1) * 128] - a_row[hd:hd + 1, :]
                decay = jnp.exp(jnp.where(causal, seg, -jnp.inf))
                res.append(_dot((cb * decay).astype(BF16), x_pair))
            pieces.append(jnp.where(lane < SSM_HEAD_DIM, res[0], res[1]))
        y_diag = jnp.concatenate(pieces, axis=1)

        state = state_ref[gi]
        y_off = ea_x[:, cols] * _dot(cm_b, state.astype(BF16))
        state_ref[gi] = (state * ea_x[t - 1:t, cols]
                         + _dot(bm.T.astype(BF16), (eb_x[:, cols] * xdt).astype(BF16)))

        y = y_diag + y_off + xs * dskip_ref[:, cols]
        y = y * _silu(z[:, cols])
        ms = jnp.mean(y * y, axis=-1, keepdims=True)
        outs.append(y * lax.rsqrt(ms + NORM_EPS) * nw_ref[:, cols])
    o_ref[...] = jnp.concatenate(outs, axis=1).astype(o_ref.dtype)


def ssd_mixer(main, small, conv_w, conv_b, dt_bias, a_log, d_skip, norm_w, bsz, seq):
    t = SSD_CHUNK
    nc = seq // t
    gn = SSM_GROUPS * SSM_STATE

    def row(b, c):
        return b * nc + c

    heads = jnp.arange(LANES)[:, None]
    e_x = (heads == (jnp.arange(SSM_INNER)[None, :] // SSM_HEAD_DIM)).astype(BF16)
    e_col = (heads == (jnp.arange(SSM_HEADS * LANES)[None, :] // LANES)).astype(BF16)

    def full(shape):
        return pl.BlockSpec(shape, lambda b, c: (0,) * len(shape))

    in_specs = [
        pl.BlockSpec((t, SSM_INNER), lambda b, c: (row(b, c), OFF_SSM_Z // SSM_INNER)),
        pl.BlockSpec((t, SSM_INNER), lambda b, c: (row(b, c), OFF_SSM_X // SSM_INNER)),
        pl.BlockSpec((t, gn), lambda b, c: (row(b, c), OFF_SSM_B // gn)),
        pl.BlockSpec((t, gn), lambda b, c: (row(b, c), OFF_SSM_C // gn)),
        pl.BlockSpec((t, LANES), lambda b, c: (row(b, c), 0)),
        full((SSM_CONV, SSM_XBC)), full((1, SSM_XBC)), full((1, LANES)), full((1, LANES)),
        full((1, SSM_INNER)), full((1, SSM_INNER)), full((LANES, SSM_INNER)), full((LANES, SSM_HEADS * LANES)),
    ]
    return pl.pallas_call(
        _ssd_kernel,
        out_shape=jax.ShapeDtypeStruct((bsz * seq, SSM_INNER), BF16),
        grid=(bsz, nc),
        in_specs=in_specs,
        out_specs=pl.BlockSpec((t, SSM_INNER), lambda b, c: (row(b, c), 0)),
        scratch_shapes=[pltpu.VMEM((t + 8, SSM_XBC), F32),
                        pltpu.VMEM((SSM_GROUPS, SSM_STATE, SSM_GW), F32)],
        compiler_params=_cparams(("parallel", "arbitrary")),
        name="ssd_mixer",
    )(main, main, main, main, small, conv_w, conv_b.reshape(1, SSM_XBC),
      _at_lanes(dt_bias, LANE_SSM_DT), _at_lanes(a_log, LANE_SSM_DT),
      jnp.repeat(d_skip, SSM_HEAD_DIM).reshape(1, SSM_INNER), norm_w.reshape(1, SSM_INNER), e_x, e_col)


def _fox_prep_kernel(f_ref, fb_ref, o_ref, carry_ref):
    t = f_ref.shape[0]

    @pl.when(pl.program_id(1) == 0)
    def _():
        carry_ref[...] = jnp.zeros_like(carry_ref)

    log_f = -_softplus(-(f_ref[...] + fb_ref[...]))
    tri = jnp.where(_lower_tri(t), 1.0, 0.0).astype(BF16)
    cs = _dot_sel_l(tri, log_f) + carry_ref[0:1, :]
    carry_ref[...] = jnp.broadcast_to(cs[t - 1:t, :], carry_ref.shape)
    o_ref[...] = cs.T[LANE_FOX_F:LANE_FOX_F + FOX_HEADS, :]


def fox_cumulative_log_f(small, f_bias, bsz, seq, t=512):
    nc = seq // t
    return pl.pallas_call(
        _fox_prep_kernel,
        out_shape=jax.ShapeDtypeStruct((bsz, FOX_HEADS, seq), F32),
        grid=(bsz, nc),
        in_specs=[pl.BlockSpec((t, LANES), lambda b, c: (b * nc + c, 1)),
                  pl.BlockSpec((1, LANES), lambda b, c: (0, 0))],
        out_specs=pl.BlockSpec((None, FOX_HEADS, t), lambda b, c: (b, 0, c)),
        scratch_shapes=[pltpu.VMEM((8, LANES), F32)],
        compiler_params=_cparams(("parallel", "arbitrary")),
        name="fox_cum_log_f",
    )(small, _at_lanes(f_bias, LANE_FOX_F))


def _fox_kernel(q_ref, k_ref, v_ref, cf_ref, o_ref):
    tq = q_ref.shape[0]
    tk = FOX_TK
    d = FOX_HEAD_DIM
    qi = pl.program_id(2)
    q = q_ref[...]
    scale2 = (d ** -0.5) * LOG2_E

    def step(j, carry, masked):
        m, l, acc = carry
        start = pl.multiple_of(j * tk, tk)
        k = k_ref[pl.ds(start, tk), :]
        v = v_ref[pl.ds(start, tk), :]
        s = _dot_nt(q, k) * scale2 - cf_ref[pl.ds(j, 1), :] * LOG2_E
        if masked:
            s = jnp.where(_lower_tri(tq), s, -jnp.inf)
        m_new = jnp.maximum(m, jnp.max(s, axis=-1, keepdims=True))
        alpha = jnp.exp2(m - m_new)
        p = jnp.exp2(s - m_new)
        l = alpha * l + jnp.sum(p, axis=-1, keepdims=True)
        acc = alpha * acc + _dot(p.astype(BF16), v)
        return m_new, l, acc

    m0 = jnp.full((tq, 1), -jnp.inf, F32)
    l0 = jnp.zeros((tq, 1), F32)
    acc0 = jnp.zeros((tq, d), F32)
    carry = lax.fori_loop(0, qi, lambda j, c: step(j, c, False), (m0, l0, acc0))
    _, l, acc = step(qi, carry, True)
    o_ref[...] = (acc / l).astype(o_ref.dtype)


def fox_mixer(main, cum_f, bsz, seq):
    tq = FOX_TQ
    assert FOX_TQ == FOX_TK
    nq = seq // tq
    d = FOX_HEAD_DIM
    cf = cum_f.reshape(bsz * FOX_HEADS, seq // FOX_TK, FOX_TK)
    return pl.pallas_call(
        _fox_kernel,
        out_shape=jax.ShapeDtypeStruct((bsz * seq, FOX_WIDTH), BF16),
        grid=(bsz, FOX_HEADS, nq),
        in_specs=[pl.BlockSpec((tq, d), lambda b, h, i: (b * nq + i, OFF_FOX_Q // d + h)),
                  pl.BlockSpec((seq, d), lambda b, h, i: (b, OFF_FOX_K // d + h)),
                  pl.BlockSpec((seq, d), lambda b, h, i: (b, OFF_FOX_V // d + h)),
                  pl.BlockSpec((None, seq // FOX_TK, FOX_TK), lambda b, h, i: (b * FOX_HEADS + h, 0, 0))],
        out_specs=pl.BlockSpec((tq, d), lambda b, h, i: (b * nq + i, h)),
        compiler_params=_cparams(("parallel", "parallel", "arbitrary")),
        name="fox_attention",
    )(main, main, main, cf)


INV_BASE = 32


def _unit_lower_inverse(a, blk_xor, eye):
    t = a[0].shape[0]
    n = range(len(a))
    pw = [jnp.where(blk_xor == 0, ai, 0.0) for ai in a]
    x = [eye - p for p in pw]
    for _ in range(INV_BASE.bit_length() - 2):
        pw = [_dot2(p, p) for p in pw]
        x = [x[i] + _dot2(x[i], pw[i]) for i in n]
    level = 1
    while INV_BASE * level < t:
        sel = (blk_xor >= level) & (blk_xor < 2 * level)
        xb = [xi.astype(BF16) for xi in x]
        xo = [_dot(xb[i], jnp.where(sel, a[i], 0.0).astype(BF16)).astype(BF16) for i in n]
        x = [x[i] - _dot(xo[i], xb[i]) for i in n]
        level *= 2
    return x


def _gdn_kernel(q_ref, k_ref, v_ref, z_ref, sm_ref, cwq_ref, cwk_ref, cwv_ref, dtb_ref, alog_ref,
                nw_ref, rep_ref, o_ref, ext_ref, state_ref):
    t = q_ref.shape[0]
    d = GDN_HEAD_DIM
    hb = GDN_HB
    w = hb * d
    chunk = pl.program_id(2)

    @pl.when(chunk == 0)
    def _():
        ext_ref[0:8, :] = jnp.zeros((8, 3 * w), F32)
        state_ref[...] = jnp.zeros_like(state_ref)

    @pl.when(chunk > 0)
    def _():
        ext_ref[0:8, :] = ext_ref[t:t + 8, :]

    ext_ref[8:8 + t, 0:w] = q_ref[...].astype(F32)
    ext_ref[8:8 + t, w:2 * w] = k_ref[...].astype(F32)
    ext_ref[8:8 + t, 2 * w:3 * w] = v_ref[...].astype(F32)

    def conv(lo, hi, w_ref):
        acc = None
        for j in range(GDN_CONV):
            off = 8 - (GDN_CONV - 1) + j
            term = w_ref[j:j + 1, :] * ext_ref[off:off + t, lo:hi]
            acc = term if acc is None else acc + term
        return _silu(acc)

    qc = conv(0, w, cwq_ref)
    kc = conv(w, 2 * w, cwk_ref)
    vc = conv(2 * w, 3 * w, cwv_ref)

    sm = sm_ref[...]
    lane = lax.broadcasted_iota(jnp.int32, sm.shape, 1)
    g_all = -jnp.exp(alog_ref[...]) * _softplus(sm + dtb_ref[...])
    mix = jnp.where(lane < LANE_GDN_B, g_all, _sigmoid(sm))
    rep = _dot_sel_r(mix, rep_ref[...])

    lower = _lower_tri(t)
    strict = _lower_tri(t, strict=True)
    tri = jnp.where(lower, 1.0, 0.0).astype(BF16)
    eye = jnp.where(lower, 1.0, 0.0) - jnp.where(strict, 1.0, 0.0)
    rows = lax.broadcasted_iota(jnp.int32, (t, t), 0)
    cols = lax.broadcasted_iota(jnp.int32, (t, t), 1)
    shift = INV_BASE.bit_length() - 1
    blk_xor = jnp.right_shift(rows, shift) ^ jnp.right_shift(cols, shift)
    z = z_ref[...].astype(F32)
    gc_all = _dot_sel_l(tri, rep[:, 0:w])

    hs = range(hb)
    sls = [slice(h * d, (h + 1) * d) for h in hs]

    def l2n(x):
        return x * lax.rsqrt(jnp.sum(x * x, axis=-1, keepdims=True) + 1e-6)

    q = [l2n(qc[:, sl]) * (d ** -0.5) for sl in sls]
    k = [l2n(kc[:, sl]) for sl in sls]
    beta = [rep[:, (hb + h) * d:(hb + h + 1) * d] for h in hs]
    gc = [gc_all[:, sl] for sl in sls]
    decay = [jnp.exp(jnp.where(lower, g - g.T, -jnp.inf)) for g in gc]
    eg = [jnp.exp(g) for g in gc]
    kb = [k[h] * beta[h] for h in hs]
    k_b = [x.astype(BF16) for x in k]
    a_mat = [jnp.where(strict, _dot_nt(kb[h].astype(BF16), k_b[h]) * decay[h], 0.0) for h in hs]
    attn = [(_dot_nt(q[h].astype(BF16), k_b[h]) * decay[h]).astype(BF16) for h in hs]
    rhs = [jnp.concatenate([vc[:, sls[h]] * beta[h], kb[h] * eg[h]], axis=1).astype(BF16) for h in hs]
    t_inv = _unit_lower_inverse(a_mat, blk_xor, eye)
    sol = [_dot(t_inv[h].astype(BF16), rhs[h]) for h in hs]
    s_old = [state_ref[h] for h in hs]
    s_b = [s.astype(BF16) for s in s_old]
    v_new = [(sol[h][:, :d] - _dot(sol[h][:, d:].astype(BF16), s_b[h])).astype(BF16) for h in hs]
    o = [_dot((q[h] * eg[h]).astype(BF16), s_b[h]) + _dot(attn[h], v_new[h]) for h in hs]
    g_last = [g[t - 1:t, :] for g in gc]
    k_dec = [(k[h] * jnp.exp(g_last[h] - gc[h])).T.astype(BF16) for h in hs]
    for h in hs:
        state_ref[h] = s_old[h] * jnp.exp(g_last[h]) + _dot(k_dec[h], v_new[h])
    nw = nw_ref[...]
    outs = [o[h] * lax.rsqrt(jnp.mean(o[h] * o[h], axis=-1, keepdims=True) + NORM_EPS) * nw
            * _silu(z[:, sls[h]]) for h in hs]
    o_ref[...] = jnp.concatenate(outs, axis=1).astype(o_ref.dtype)


def gdn_mixer(main, small, conv_w, dt_bias, a_log, norm_w, bsz, seq):
    t = GDN_CHUNK
    nc = seq // t
    hb = GDN_HB
    d = GDN_HEAD_DIM
    w = hb * d
    ng = GDN_HEADS // hb

    def row(b, hg, c):
        return b * nc + c

    src = jnp.arange(LANES)[None, :, None]
    col_head = (jnp.arange(2 * w) // d)[None, None, :]
    hg_idx = jnp.arange(ng)[:, None, None]
    want = jnp.where(col_head < hb, LANE_GDN_A + hg_idx * hb + col_head,
                     LANE_GDN_B + hg_idx * hb + (col_head - hb))
    rep = (src == want).astype(BF16)

    in_specs = [
        pl.BlockSpec((t, w), lambda b, hg, c: (row(b, hg, c), OFF_GDN_Q // w + hg)),
        pl.BlockSpec((t, w), lambda b, hg, c: (row(b, hg, c), OFF_GDN_K // w + hg)),
        pl.BlockSpec((t, w), lambda b, hg, c: (row(b, hg, c), OFF_GDN_V // w + hg)),
        pl.BlockSpec((t, w), lambda b, hg, c: (row(b, hg, c), OFF_GDN_Z // w + hg)),
        pl.BlockSpec((t, LANES), lambda b, hg, c: (row(b, hg, c), 2)),
        pl.BlockSpec((GDN_CONV, w), lambda b, hg, c: (0, hg)),
        pl.BlockSpec((GDN_CONV, w), lambda b, hg, c: (0, ng + hg)),
        pl.BlockSpec((GDN_CONV, w), lambda b, hg, c: (0, 2 * ng + hg)),
        pl.BlockSpec((1, LANES), lambda b, hg, c: (0, 0)),
        pl.BlockSpec((1, LANES), lambda b, hg, c: (0, 0)),
        pl.BlockSpec((1, d), lambda b, hg, c: (0, 0)),
        pl.BlockSpec((None, LANES, 2 * w), lambda b, hg, c: (hg, 0, 0)),
    ]
    return pl.pallas_call(
        _gdn_kernel,
        out_shape=jax.ShapeDtypeStruct((bsz * seq, GDN_WIDTH), BF16),
        grid=(bsz, ng, nc),
        in_specs=in_specs,
        out_specs=pl.BlockSpec((t, w), lambda b, hg, c: (row(b, hg, c), hg)),
        scratch_shapes=[pltpu.VMEM((t + 8, 3 * w), F32), pltpu.VMEM((hb, d, d), F32)],
        compiler_params=_cparams(("parallel", "parallel", "arbitrary")),
        name="gdn_mixer",
    )(main, main, main, main, small, conv_w, conv_w, conv_w,
      _at_lanes(dt_bias, LANE_GDN_A), _at_lanes(a_log, LANE_GDN_A), norm_w.reshape(1, d), rep)


def _merge_kernel(ys_ref, yf_ref, yg_ref, ws_ref, wf_ref, wg_ref, gs_ref, gf_ref, gg_ref, o_ref,
                  wsb_ref, wfb_ref, wgb_ref):
    @pl.when(pl.program_id(1) == 0)
    def _():
        wsb_ref[...] = ws_ref[...].astype(BF16)
        wfb_ref[...] = wf_ref[...].astype(BF16)
        wgb_ref[...] = wg_ref[...].astype(BF16)

    acc = _sigmoid(gs_ref[...].astype(F32)) * _dot(ys_ref[...], wsb_ref[...])
    acc = acc + _sigmoid(gf_ref[...].astype(F32)) * _dot(yf_ref[...], wfb_ref[...])
    acc = acc + _sigmoid(gg_ref[...].astype(F32)) * _dot(yg_ref[...], wgb_ref[...])
    o_ref[...] = acc.astype(o_ref.dtype)


def merge_branches(y_ssm, y_fox, y_gdn, w_ssm, w_fox, w_gdn, main, layer, tm=TM, tn=512):
    m = y_ssm.shape[0]
    n = D_MODEL
    gate_blk = OFF_GATE // tn
    per_gate = D_MODEL // tn

    def lhs(width):
        return pl.BlockSpec((tm, width), lambda j, i: (i, 0))

    def rhs(width):
        return pl.BlockSpec((None, width, tn), lambda j, i: (layer, 0, j))

    def gate(idx):
        return pl.BlockSpec((tm, tn), lambda j, i: (i, gate_blk + idx * per_gate + j))

    return pl.pallas_call(
        _merge_kernel,
        out_shape=jax.ShapeDtypeStruct((m, n), BF16),
        grid=(n // tn, m // tm),
        in_specs=[lhs(SSM_INNER), lhs(FOX_WIDTH), lhs(GDN_WIDTH),
                  rhs(SSM_INNER), rhs(FOX_WIDTH), rhs(GDN_WIDTH),
                  gate(0), gate(1), gate(2)],
        out_specs=pl.BlockSpec((tm, tn), lambda j, i: (i, j)),
        scratch_shapes=[pltpu.VMEM((SSM_INNER, tn), BF16), pltpu.VMEM((FOX_WIDTH, tn), BF16),
                        pltpu.VMEM((GDN_WIDTH, tn), BF16)],
        compiler_params=_cparams(("arbitrary", "arbitrary")),
        name="merge_branches",
    )(y_ssm, y_fox, y_gdn, w_ssm, w_fox, w_gdn, main, main, main)


def _out_proj_kernel(a_ref, w_ref, x_ref, o_ref, wb_ref):
    @pl.when(pl.program_id(1) == 0)
    def _():
        wb_ref[...] = w_ref[...].astype(BF16)

    o_ref[...] = x_ref[...] + _dot(a_ref[...], wb_ref[...])


def out_proj(a, w, x, layer, tm=TM, tn=512):
    m, k = a.shape
    n = w.shape[2]
    return pl.pallas_call(
        _out_proj_kernel,
        out_shape=jax.ShapeDtypeStruct((m, n), F32),
        grid=(n // tn, m // tm),
        in_specs=[pl.BlockSpec((tm, k), lambda j, i: (i, 0)),
                  pl.BlockSpec((None, k, tn), lambda j, i: (layer, 0, j)),
                  pl.BlockSpec((tm, tn), lambda j, i: (i, j))],
        out_specs=pl.BlockSpec((tm, tn), lambda j, i: (i, j)),
        scratch_shapes=[pltpu.VMEM((k, tn), BF16)],
        compiler_params=_cparams(("arbitrary", "arbitrary")),
        name="out_proj",
    )(a, w, x)


def _residual_matmul_kernel(a_ref, w_ref, x_ref, o_ref):
    kk = pl.program_id(2)

    @pl.when(kk == 0)
    def _():
        o_ref[...] = x_ref[...] + _dot(a_ref[...], w_ref[...])

    @pl.when(kk > 0)
    def _():
        o_ref[...] += _dot(a_ref[...], w_ref[...])


def ffn_down(a, w, x, layer, tm=TM, tn=512, tk=D_FF // 2):
    m, k = a.shape
    n = w.shape[2]
    return pl.pallas_call(
        _residual_matmul_kernel,
        out_shape=jax.ShapeDtypeStruct((m, n), F32),
        grid=(m // tm, n // tn, k // tk),
        in_specs=[pl.BlockSpec((tm, tk), lambda i, j, kk: (i, kk)),
                  pl.BlockSpec((None, tk, tn), lambda i, j, kk: (layer, kk, j)),
                  pl.BlockSpec((tm, tn), lambda i, j, kk: (i, j))],
        out_specs=pl.BlockSpec((tm, tn), lambda i, j, kk: (i, j)),
        compiler_params=_cparams(("parallel", "parallel", "arbitrary")),
        name="ffn_down",
    )(a, w, x)


def _ffn_up_kernel(h_ref, wg_ref, wv_ref, cwg_ref, cwv_ref, cbg_ref, cbv_ref, o_ref,
                   ug_ref, uv_ref, tg_ref, tv_ref, *, tiles_per_seq):
    tm = h_ref.shape[0]
    i = pl.program_id(0)
    j = pl.program_id(1)

    @pl.when(i == 0)
    def _():
        tg_ref[j] = jnp.zeros(tg_ref.shape[1:], F32)
        tv_ref[j] = jnp.zeros(tv_ref.shape[1:], F32)

    first = (i % tiles_per_seq) == 0
    h = h_ref[...]

    def conv_product(u_ref, t_ref, w_ref, cw_ref, cb_ref):
        u_ref[0:HALO, :] = jnp.where(first, 0.0, t_ref[j])
        u_ref[HALO:HALO + tm, :] = _dot(h, w_ref[...])
        t_ref[j] = u_ref[tm:tm + HALO, :]
        acc = cb_ref[...]
        for tap in range(FFN_CONV):
            off = HALO - (FFN_CONV - 1) + tap
            acc = acc + cw_ref[tap:tap + 1, :] * u_ref[off:off + tm, :]
        return acc

    gate = conv_product(ug_ref, tg_ref, wg_ref, cwg_ref, cbg_ref)
    val = conv_product(uv_ref, tv_ref, wv_ref, cwv_ref, cbv_ref)
    o_ref[...] = (_silu(gate) * val).astype(o_ref.dtype)


def ffn_up(h, w_gate, w_val, cw_gate, cw_val, cb_gate, cb_val, seq, tm=TM, tn=512):
    m, k = h.shape
    nb = D_FF_PAD // tn
    kern = functools.partial(_ffn_up_kernel, tiles_per_seq=seq // tm)

    def col_spec(rows):
        return pl.BlockSpec((rows, tn), lambda i, j: (0, j))

    return pl.pallas_call(
        kern,
        out_shape=jax.ShapeDtypeStruct((m, D_FF), BF16),
        grid=(m // tm, nb),
        in_specs=[pl.BlockSpec((tm, k), lambda i, j: (i, 0)),
                  col_spec(k), col_spec(k), col_spec(FFN_CONV), col_spec(FFN_CONV), col_spec(1), col_spec(1)],
        out_specs=pl.BlockSpec((tm, tn), lambda i, j: (i, j)),
        scratch_shapes=[pltpu.VMEM((HALO + tm, tn), F32)] * 2
                       + [pltpu.VMEM((nb, HALO, tn), F32)] * 2,
        compiler_params=_cparams(("arbitrary", "arbitrary")),
        name="ffn_up",
    )(h, w_gate, w_val, cw_gate, cw_val, cb_gate, cb_val)


def _pad_ffn_up(w_up, conv_w, conv_b):
    pad = D_FF_PAD - D_FF

    def halves(a, dtype):
        return tuple(jnp.pad(p.astype(dtype), ((0, 0), (0, pad))) for p in (a[:, :D_FF], a[:, D_FF:]))

    return halves(w_up, BF16), halves(conv_w, F32), halves(conv_b.reshape(1, -1), F32)


def kernel(x, norm_mix, w_in, ssm_conv_w, ssm_conv_b, ssm_dt_bias, ssm_a_log, ssm_d, ssm_norm,
           fox_f_bias, gdn_conv_w, gdn_dt_bias, gdn_a_log, gdn_norm, w_br_ssm, w_br_fox, w_br_gdn,
           w_out, norm_ffn, w_up, ffn_conv_w, ffn_conv_b, w_down, norm_final):
    bsz, seq, d = x.shape
    depth = w_in.shape[0]
    m = bsz * seq
    xf = x.reshape(m, d)
    w_down_b = w_down.astype(BF16)
    w_in_b = w_in.astype(BF16)
    for layer in range(depth):
        h = rmsnorm(xf, norm_mix[layer], BF16)
        main = in_proj_main(h, w_in_b, layer)
        small = in_proj_small(h, w_in_b, layer)
        y_ssm = ssd_mixer(main, small, ssm_conv_w[layer], ssm_conv_b[layer], ssm_dt_bias[layer],
                          ssm_a_log[layer], ssm_d[layer], ssm_norm[layer], bsz, seq)
        cum_f = fox_cumulative_log_f(small, fox_f_bias[layer], bsz, seq)
        y_fox = fox_mixer(main, cum_f, bsz, seq)
        y_gdn = gdn_mixer(main, small, gdn_conv_w[layer], gdn_dt_bias[layer], gdn_a_log[layer],
                          gdn_norm[layer], bsz, seq)
        merged = merge_branches(y_ssm, y_fox, y_gdn, w_br_ssm, w_br_fox, w_br_gdn, main, layer)
        xf = out_proj(merged, w_out, xf, layer)
        h2 = rmsnorm(xf, norm_ffn[layer], BF16)
        w_up_p, cw_p, cb_p = _pad_ffn_up(w_up[layer], ffn_conv_w[layer], ffn_conv_b[layer])
        act = ffn_up(h2, *w_up_p, *cw_p, *cb_p, seq)
        xf = ffn_down(act, w_down_b, xf, layer)
    return rmsnorm(xf, norm_final, F32).reshape(bsz, seq, d)
```

```python
import functools

import jax
import jax.numpy as jnp
from jax import lax
from jax.experimental import pallas as pl
from jax.experimental.pallas import tpu as pltpu

F32 = jnp.float32
BF16 = jnp.bfloat16

D_MODEL = 4096
SSM_INNER = 2048
SSM_HEAD_DIM = 64
SSM_HEADS = 32
SSM_GROUPS = 4
SSM_STATE = 128
SSM_CONV = 4
SSM_XBC = SSM_INNER + 2 * SSM_GROUPS * SSM_STATE
SSM_HPG = SSM_HEADS // SSM_GROUPS
SSM_GW = SSM_HPG * SSM_HEAD_DIM
FOX_HEADS = 8
FOX_HEAD_DIM = 128
FOX_WIDTH = FOX_HEADS * FOX_HEAD_DIM
GDN_HEADS = 8
GDN_HEAD_DIM = 128
GDN_WIDTH = GDN_HEADS * GDN_HEAD_DIM
GDN_CONV = 4
D_FF = 11008
D_FF_PAD = 11264
FFN_CONV = 3
NORM_EPS = 1e-6
LOG2_E = 1.4426950408889634
N_MAIN = 24576
LANES = 128
HALO = 16

OFF_SSM_Z = 0
OFF_SSM_X = 2048
OFF_SSM_B = 4096
OFF_SSM_C = 4608
OFF_FOX_Q = 5120
OFF_FOX_K = 6144
OFF_FOX_V = 7168
OFF_GDN_Q = 8192
OFF_GDN_K = 9216
OFF_GDN_V = 10240
OFF_GDN_Z = 11264
OFF_GATE = 12288
MAIN_SEGMENTS = ((0, 0), (OFF_FOX_Q, SSM_HEADS), (OFF_GDN_Q, SSM_HEADS + FOX_HEADS),
                 (OFF_GDN_Z, SSM_HEADS + FOX_HEADS + 2 * GDN_HEADS))
SMALL_BLOCKS = (OFF_FOX_Q // LANES, (OFF_GDN_Q + SSM_HEADS) // LANES,
                (OFF_GDN_Z + SSM_HEADS + FOX_HEADS) // LANES)
LANE_SSM_DT = 0
LANE_FOX_F = SSM_HEADS
LANE_GDN_A = SSM_HEADS + FOX_HEADS
LANE_GDN_B = SSM_HEADS + FOX_HEADS + GDN_HEADS
N_SMALL = len(SMALL_BLOCKS) * LANES

SSD_CHUNK = 128
GDN_CHUNK = 128
GDN_HB = 8
FOX_TQ = 512
FOX_TK = 512
TM = 1024
VMEM_LIMIT = 56 * 1024 * 1024


def _cparams(sem):
    return pltpu.CompilerParams(dimension_semantics=sem, vmem_limit_bytes=VMEM_LIMIT)


def _sigmoid(x):
    return 1.0 / (1.0 + jnp.exp(-x))


def _silu(x):
    return x * _sigmoid(x)


def _softplus(x):
    return jnp.maximum(x, 0.0) + jnp.log(1.0 + jnp.exp(-jnp.abs(x)))


def _split3(x):
    hi = x.astype(BF16)
    r1 = x - hi.astype(F32)
    mid = r1.astype(BF16)
    lo = (r1 - mid.astype(F32)).astype(BF16)
    return hi, mid, lo


def _dot(a, b):
    return jnp.dot(a, b, preferred_element_type=F32)


def _dot_nt(a, b):
    return lax.dot_general(a, b, (((1,), (1,)), ((), ())), preferred_element_type=F32)


def _dot_sel_r(x, sel):
    hi, mid, lo = _split3(x)
    return (_dot(lo, sel) + _dot(mid, sel)) + _dot(hi, sel)


def _dot_sel_l(sel, x):
    hi, mid, lo = _split3(x)
    return (_dot(sel, lo) + _dot(sel, mid)) + _dot(sel, hi)


def _dot2(a, b):
    ah = a.astype(BF16)
    al = (a - ah.astype(F32)).astype(BF16)
    bh = b.astype(BF16)
    bl = (b - bh.astype(F32)).astype(BF16)
    return (_dot(al, bh) + _dot(ah, bl)) + _dot(ah, bh)


def _lower_tri(n, strict=False):
    r = lax.broadcasted_iota(jnp.int32, (n, n), 0)
    c = lax.broadcasted_iota(jnp.int32, (n, n), 1)
    return (r > c) if strict else (r >= c)


def _at_lanes(v, start):
    return jnp.pad(v, (start, LANES - start - v.shape[0])).reshape(1, LANES)


def _rmsnorm_kernel(x_ref, w_ref, o_ref):
    x = x_ref[...]
    ms = jnp.mean(x * x, axis=-1, keepdims=True)
    o_ref[...] = (x * lax.rsqrt(ms + NORM_EPS) * w_ref[...]).astype(o_ref.dtype)


def rmsnorm(x, w, out_dtype, tm=512):
    m, d = x.shape
    return pl.pallas_call(
        _rmsnorm_kernel,
        out_shape=jax.ShapeDtypeStruct((m, d), out_dtype),
        grid=(m // tm,),
        in_specs=[pl.BlockSpec((tm, d), lambda i: (i, 0)),
                  pl.BlockSpec((1, d), lambda i: (0, 0))],
        out_specs=pl.BlockSpec((tm, d), lambda i: (i, 0)),
        compiler_params=_cparams(("parallel",)),
        name="rmsnorm",
    )(x, w.reshape(1, d))


CAST_ROWS = 256


def _in_proj_kernel(a_ref, w_ref, wx_ref, o_ref, wb_ref, *, seg_blocks):
    j = pl.program_id(0)
    k, tn = w_ref.shape

    for lo, hi, shift in seg_blocks:
        @pl.when((pl.program_id(1) == 0) & (j >= lo) & (j < hi))
        def _(shift=shift):
            def body(r, carry):
                r0 = pl.multiple_of(r * CAST_ROWS, CAST_ROWS)
                w = w_ref[pl.ds(r0, CAST_ROWS), :]
                if shift:
                    both = jnp.concatenate([w, wx_ref[pl.ds(r0, CAST_ROWS), :]], axis=1)
                    w = both[:, shift:shift + tn]
                wb_ref[pl.ds(r0, CAST_ROWS), :] = w
                return carry
            lax.fori_loop(0, k // CAST_ROWS, body, 0)

    o_ref[...] = _dot(a_ref[...], wb_ref[...]).astype(o_ref.dtype)


def in_proj_main(h, w_in, layer, tm=TM, tn=1024):
    m, k = h.shape
    starts = [s for s, _ in MAIN_SEGMENTS] + [N_MAIN]
    seg_blocks = tuple((starts[i] // tn, starts[i + 1] // tn, MAIN_SEGMENTS[i][1])
                       for i in range(len(MAIN_SEGMENTS)))
    assert all(s % tn == 0 for s in starts)
    return pl.pallas_call(
        functools.partial(_in_proj_kernel, seg_blocks=seg_blocks),
        out_shape=jax.ShapeDtypeStruct((m, N_MAIN), BF16),
        grid=(N_MAIN // tn, m // tm),
        in_specs=[pl.BlockSpec((tm, k), lambda j, i: (i, 0)),
                  pl.BlockSpec((None, k, tn), lambda j, i: (layer, 0, j)),
                  pl.BlockSpec((None, k, LANES), lambda j, i: (layer, 0, (j + 1) * (tn // LANES)))],
        out_specs=pl.BlockSpec((tm, tn), lambda j, i: (i, j)),
        scratch_shapes=[pltpu.VMEM((k, tn), BF16)],
        compiler_params=_cparams(("arbitrary", "arbitrary")),
        name="in_proj_main",
    )(h, w_in, w_in)


def _in_proj_small_kernel(a_ref, w0_ref, w1_ref, w2_ref, o_ref):
    w = jnp.concatenate([w0_ref[...], w1_ref[...], w2_ref[...]], axis=1)
    o_ref[...] = _dot(a_ref[...], w)


def in_proj_small(h, w_in, layer):
    m, k = h.shape

    def w_spec(blk):
        return pl.BlockSpec((None, k, LANES), lambda i: (layer, 0, blk))

    return pl.pallas_call(
        _in_proj_small_kernel,
        out_shape=jax.ShapeDtypeStruct((m, N_SMALL), F32),
        grid=(m // TM,),
        in_specs=[pl.BlockSpec((TM, k), lambda i: (i, 0))] + [w_spec(b) for b in SMALL_BLOCKS],
        out_specs=pl.BlockSpec((TM, N_SMALL), lambda i: (i, 0)),
        compiler_params=_cparams(("parallel",)),
        name="in_proj_small",
    )(h, w_in, w_in, w_in)


def _ssd_kernel(z_ref, xs_ref, b_ref, c_ref, dt_ref, cw_ref, cb_ref, dtb_ref, alog_ref, dskip_ref,
                nw_ref, ex_ref, ecol_ref, o_ref, ext_ref, state_ref):
    t = z_ref.shape[0]
    n = SSM_STATE
    gw = SSM_GW
    g = SSM_GROUPS
    chunk = pl.program_id(1)

    @pl.when(chunk == 0)
    def _():
        ext_ref[0:8, :] = jnp.zeros((8, SSM_XBC), F32)
        state_ref[...] = jnp.zeros_like(state_ref)

    @pl.when(chunk > 0)
    def _():
        ext_ref[0:8, :] = ext_ref[t:t + 8, :]

    ext_ref[8:8 + t, 0:SSM_INNER] = xs_ref[...].astype(F32)
    ext_ref[8:8 + t, SSM_INNER:SSM_INNER + g * n] = b_ref[...].astype(F32)
    ext_ref[8:8 + t, SSM_INNER + g * n:SSM_XBC] = c_ref[...].astype(F32)

    def conv(lo, hi):
        acc = cb_ref[:, lo:hi]
        for j in range(SSM_CONV):
            off = 8 - (SSM_CONV - 1) + j
            acc = acc + cw_ref[j:j + 1, lo:hi] * ext_ref[off:off + t, lo:hi]
        return _silu(acc)

    dt = _softplus(dt_ref[...] + dtb_ref[...])
    a = -jnp.exp(alog_ref[...])
    causal = _lower_tri(t)
    tri = jnp.where(causal, 1.0, 0.0).astype(BF16)
    a_cs = _dot_sel_l(tri, dt * a)
    a_last = a_cs[t - 1:t, :]
    ex = ex_ref[...]
    dt_x = _dot_sel_r(dt, ex)
    ea_x = _dot_sel_r(jnp.exp(a_cs), ex)
    eb_x = _dot_sel_r(jnp.exp(a_last - a_cs), ex)
    a_col = _dot_sel_r(a_cs, ecol_ref[...])
    a_row = a_cs.T
    lane = lax.broadcasted_iota(jnp.int32, (t, 2 * SSM_HEAD_DIM), 1)
    z = z_ref[...].astype(F32)

    outs = []
    for gi in range(g):
        cols = slice(gi * gw, (gi + 1) * gw)
        xs = conv(gi * gw, (gi + 1) * gw)
        bm = conv(SSM_INNER + gi * n, SSM_INNER + (gi + 1) * n)
        cm = conv(SSM_INNER + (g + gi) * n, SSM_INNER + (g + gi + 1) * n)
        xdt = xs * dt_x[:, cols]
        xdt_b = xdt.astype(BF16)
        cm_b = cm.astype(BF16)
        cb = _dot_nt(cm_b, bm.astype(BF16))
        pieces = []
        for p in range(SSM_HPG // 2):
            x_pair = xdt_b[:, p * 128:(p + 1) * 128]
            res = []
            for i in (2 * p, 2 * p + 1):
                hd = gi * SSM_HPG + i
                seg = a_col[:, hd * 128:(hd + 1) * 128] - a_row[hd:hd + 1, :]
                decay = jnp.exp(jnp.where(causal, seg, -jnp.inf))
                res.append(_dot((cb * decay).astype(BF16), x_pair))
            pieces.append(jnp.where(lane < SSM_HEAD_DIM, res[0], res[1]))
        y_diag = jnp.concatenate(pieces, axis=1)

        state = state_ref[gi]
        y_off = ea_x[:, cols] * _dot(cm_b, state.astype(BF16))
        state_ref[gi] = (state * ea_x[t - 1:t, cols]
                         + _dot(bm.T.astype(BF16), (eb_x[:, cols] * xdt).astype(BF16)))

        y = y_diag + y_off + xs * dskip_ref[:, cols]
        y = y * _silu(z[:, cols])
        ms = jnp.mean(y * y, axis=-1, keepdims=True)
        outs.append(y * lax.rsqrt(ms + NORM_EPS) * nw_ref[:, cols])
    o_ref[...] = jnp.concatenate(outs, axis=1).astype(o_ref.dtype)


def ssd_mixer(main, small, conv_w, conv_b, dt_bias, a_log, d_skip, norm_w, bsz, seq):
    t = SSD_CHUNK
    nc = seq // t
    gn = SSM_GROUPS * SSM_STATE

    def row(b, c):
        return b * nc + c

    heads = jnp.arange(LANES)[:, None]
    e_x = (heads == (jnp.arange(SSM_INNER)[None, :] // SSM_HEAD_DIM)).astype(BF16)
    e_col = (heads == (jnp.arange(SSM_HEADS * LANES)[None, :] // LANES)).astype(BF16)

    def full(shape):
        return pl.BlockSpec(shape, lambda b, c: (0,) * len(shape))

    in_specs = [
        pl.BlockSpec((t, SSM_INNER), lambda b, c: (row(b, c), OFF_SSM_Z // SSM_INNER)),
        pl.BlockSpec((t, SSM_INNER), lambda b, c: (row(b, c), OFF_SSM_X // SSM_INNER)),
        pl.BlockSpec((t, gn), lambda b, c: (row(b, c), OFF_SSM_B // gn)),
        pl.BlockSpec((t, gn), lambda b, c: (row(b, c), OFF_SSM_C // gn)),
        pl.BlockSpec((t, LANES), lambda b, c: (row(b, c), 0)),
        full((SSM_CONV, SSM_XBC)), full((1, SSM_XBC)), full((1, LANES)), full((1, LANES)),
        full((1, SSM_INNER)), full((1, SSM_INNER)), full((LANES, SSM_INNER)), full((LANES, SSM_HEADS * LANES)),
    ]
    return pl.pallas_call(
        _ssd_kernel,
        out_shape=jax.ShapeDtypeStruct((bsz * seq, SSM_INNER), BF16),
        grid=(bsz, nc),
        in_specs=in_specs,
        out_specs=pl.BlockSpec((t, SSM_INNER), lambda b, c: (row(b, c), 0)),
        scratch_shapes=[pltpu.VMEM((t + 8, SSM_XBC), F32),
                        pltpu.VMEM((SSM_GROUPS, SSM_STATE, SSM_GW), F32)],
        compiler_params=_cparams(("parallel", "arbitrary")),
        name="ssd_mixer",
    )(main, main, main, main, small, conv_w, conv_b.reshape(1, SSM_XBC),
      _at_lanes(dt_bias, LANE_SSM_DT), _at_lanes(a_log, LANE_SSM_DT),
      jnp.repeat(d_skip, SSM_HEAD_DIM).reshape(1, SSM_INNER), norm_w.reshape(1, SSM_INNER), e_x, e_col)


def _fox_prep_kernel(f_ref, fb_ref, o_ref, carry_ref):
    t = f_ref.shape[0]

    @pl.when(pl.program_id(1) == 0)
    def _():
        carry_ref[...] = jnp.zeros_like(carry_ref)

    log_f = -_softplus(-(f_ref[...] + fb_ref[...]))
    tri = jnp.where(_lower_tri(t), 1.0, 0.0).astype(BF16)
    cs = _dot_sel_l(tri, log_f) + carry_ref[0:1, :]
    carry_ref[...] = jnp.broadcast_to(cs[t - 1:t, :], carry_ref.shape)
    o_ref[...] = cs.T[LANE_FOX_F:LANE_FOX_F + FOX_HEADS, :]


def fox_cumulative_log_f(small, f_bias, bsz, seq, t=512):
    nc = seq // t
    return pl.pallas_call(
        _fox_prep_kernel,
        out_shape=jax.ShapeDtypeStruct((bsz, FOX_HEADS, seq), F32),
        grid=(bsz, nc),
        in_specs=[pl.BlockSpec((t, LANES), lambda b, c: (b * nc + c, 1)),
                  pl.BlockSpec((1, LANES), lambda b, c: (0, 0))],
        out_specs=pl.BlockSpec((None, FOX_HEADS, t), lambda b, c: (b, 0, c)),
        scratch_shapes=[pltpu.VMEM((8, LANES), F32)],
        compiler_params=_cparams(("parallel", "arbitrary")),
        name="fox_cum_log_f",
    )(small, _at_lanes(f_bias, LANE_FOX_F))


def _fox_kernel(q_ref, k_ref, v_ref, cf_ref, o_ref):
    tq = q_ref.shape[0]
    tk = FOX_TK
    d = FOX_HEAD_DIM
    qi = pl.program_id(2)
    q = q_ref[...]
    scale2 = (d ** -0.5) * LOG2_E

    def step(j, carry, masked):
        m, l, acc = carry
        start = pl.multiple_of(j * tk, tk)
        k = k_ref[pl.ds(start, tk), :]
        v = v_ref[pl.ds(start, tk), :]
        s = _dot_nt(q, k) * scale2 - cf_ref[pl.ds(j, 1), :] * LOG2_E
        if masked:
            s = jnp.where(_lower_tri(tq), s, -jnp.inf)
        m_new = jnp.maximum(m, jnp.max(s, axis=-1, keepdims=True))
        alpha = jnp.exp2(m - m_new)
        p = jnp.exp2(s - m_new)
        l = alpha * l + jnp.sum(p, axis=-1, keepdims=True)
        acc = alpha * acc + _dot(p.astype(BF16), v)
        return m_new, l, acc

    m0 = jnp.full((tq, 1), -jnp.inf, F32)
    l0 = jnp.zeros((tq, 1), F32)
    acc0 = jnp.zeros((tq, d), F32)
    carry = lax.fori_loop(0, qi, lambda j, c: step(j, c, False), (m0, l0, acc0))
    _, l, acc = step(qi, carry, True)
    o_ref[...] = (acc / l).astype(o_ref.dtype)


def fox_mixer(main, cum_f, bsz, seq):
    tq = FOX_TQ
    assert FOX_TQ == FOX_TK
    nq = seq // tq
    d = FOX_HEAD_DIM
    cf = cum_f.reshape(bsz * FOX_HEADS, seq // FOX_TK, FOX_TK)
    return pl.pallas_call(
        _fox_kernel,
        out_shape=jax.ShapeDtypeStruct((bsz * seq, FOX_WIDTH), BF16),
        grid=(bsz, FOX_HEADS, nq),
        in_specs=[pl.BlockSpec((tq, d), lambda b, h, i: (b * nq + i, OFF_FOX_Q // d + h)),
                  pl.BlockSpec((seq, d), lambda b, h, i: (b, OFF_FOX_K // d + h)),
                  pl.BlockSpec((seq, d), lambda b, h, i: (b, OFF_FOX_V // d + h)),
                  pl.BlockSpec((None, seq // FOX_TK, FOX_TK), lambda b, h, i: (b * FOX_HEADS + h, 0, 0))],
        out_specs=pl.BlockSpec((tq, d), lambda b, h, i: (b * nq + i, h)),
        compiler_params=_cparams(("parallel", "parallel", "arbitrary")),
        name="fox_attention",
    )(main, main, main, cf)


INV_BASE = 32


def _unit_lower_inverse(a, blk_xor, eye):
    t = a[0].shape[0]
    n = range(len(a))
    pw = [jnp.where(blk_xor == 0, ai, 0.0) for ai in a]
    x = [eye - p for p in pw]
    for _ in range(INV_BASE.bit_length() - 2):
        pw = [_dot2(p, p) for p in pw]
        x = [x[i] + _dot2(x[i], pw[i]) for i in n]
    level = 1
    while INV_BASE * level < t:
        sel = (blk_xor >= level) & (blk_xor < 2 * level)
        xb = [xi.astype(BF16) for xi in x]
        xo = [_dot(xb[i], jnp.where(sel, a[i], 0.0).astype(BF16)).astype(BF16) for i in n]
        x = [x[i] - _dot(xo[i], xb[i]) for i in n]
        level *= 2
    return x


def _gdn_kernel(q_ref, k_ref, v_ref, z_ref, sm_ref, cwq_ref, cwk_ref, cwv_ref, dtb_ref, alog_ref,
                nw_ref, rep_ref, o_ref, ext_ref, state_ref):
    t = q_ref.shape[0]
    d = GDN_HEAD_DIM
    hb = GDN_HB
    w = hb * d
    chunk = pl.program_id(2)

    @pl.when(chunk == 0)
    def _():
        ext_ref[0:8, :] = jnp.zeros((8, 3 * w), F32)
        state_ref[...] = jnp.zeros_like(state_ref)

    @pl.when(chunk > 0)
    def _():
        ext_ref[0:8, :] = ext_ref[t:t + 8, :]

    ext_ref[8:8 + t, 0:w] = q_ref[...].astype(F32)
    ext_ref[8:8 + t, w:2 * w] = k_ref[...].astype(F32)
    ext_ref[8:8 + t, 2 * w:3 * w] = v_ref[...].astype(F32)

    def conv(lo, hi, w_ref):
        acc = None
        for j in range(GDN_CONV):
            off = 8 - (GDN_CONV - 1) + j
            term = w_ref[j:j + 1, :] * ext_ref[off:off + t, lo:hi]
            acc = term if acc is None else acc + term
        return _silu(acc)

    qc = conv(0, w, cwq_ref)
    kc = conv(w, 2 * w, cwk_ref)
    vc = conv(2 * w, 3 * w, cwv_ref)

    sm = sm_ref[...]
    lane = lax.broadcasted_iota(jnp.int32, sm.shape, 1)
    g_all = -jnp.exp(alog_ref[...]) * _softplus(sm + dtb_ref[...])
    mix = jnp.where(lane < LANE_GDN_B, g_all, _sigmoid(sm))
    rep = _dot_sel_r(mix, rep_ref[...])

    lower = _lower_tri(t)
    strict = _lower_tri(t, strict=True)
    tri = jnp.where(lower, 1.0, 0.0).astype(BF16)
    eye = jnp.where(lower, 1.0, 0.0) - jnp.where(strict, 1.0, 0.0)
    rows = lax.broadcasted_iota(jnp.int32, (t, t), 0)
    cols = lax.broadcasted_iota(jnp.int32, (t, t), 1)
    shift = INV_BASE.bit_length() - 1
    blk_xor = jnp.right_shift(rows, shift) ^ jnp.right_shift(cols, shift)
    z = z_ref[...].astype(F32)
    gc_all = _dot_sel_l(tri, rep[:, 0:w])

    hs = range(hb)
    sls = [slice(h * d, (h + 1) * d) for h in hs]

    def l2n(x):
        return x * lax.rsqrt(jnp.sum(x * x, axis=-1, keepdims=True) + 1e-6)

    q = [l2n(qc[:, sl]) * (d ** -0.5) for sl in sls]
    k = [l2n(kc[:, sl]) for sl in sls]
    beta = [rep[:, (hb + h) * d:(hb + h + 1) * d] for h in hs]
    gc = [gc_all[:, sl] for sl in sls]
    decay = [jnp.exp(jnp.where(lower, g - g.T, -jnp.inf)) for g in gc]
    eg = [jnp.exp(g) for g in gc]
    kb = [k[h] * beta[h] for h in hs]
    k_b = [x.astype(BF16) for x in k]
    a_mat = [jnp.where(strict, _dot_nt(kb[h].astype(BF16), k_b[h]) * decay[h], 0.0) for h in hs]
    attn = [(_dot_nt(q[h].astype(BF16), k_b[h]) * decay[h]).astype(BF16) for h in hs]
    rhs = [jnp.concatenate([vc[:, sls[h]] * beta[h], kb[h] * eg[h]], axis=1).astype(BF16) for h in hs]
    t_inv = _unit_lower_inverse(a_mat, blk_xor, eye)
    sol = [_dot(t_inv[h].astype(BF16), rhs[h]) for h in hs]
    s_old = [state_ref[h] for h in hs]
    s_b = [s.astype(BF16) for s in s_old]
    v_new = [(sol[h][:, :d] - _dot(sol[h][:, d:].astype(BF16), s_b[h])).astype(BF16) for h in hs]
    o = [_dot((q[h] * eg[h]).astype(BF16), s_b[h]) + _dot(attn[h], v_new[h]) for h in hs]
    g_last = [g[t - 1:t, :] for g in gc]
    k_dec = [(k[h] * jnp.exp(g_last[h] - gc[h])).T.astype(BF16) for h in hs]
    for h in hs:
        state_ref[h] = s_old[h] * jnp.exp(g_last[h]) + _dot(k_dec[h], v_new[h])
    nw = nw_ref[...]
    outs = [o[h] * lax.rsqrt(jnp.mean(o[h] * o[h], axis=-1, keepdims=True) + NORM_EPS) * nw
            * _silu(z[:, sls[h]]) for h in hs]
    o_ref[...] = jnp.concatenate(outs, axis=1).astype(o_ref.dtype)


def gdn_mixer(main, small, conv_w, dt_bias, a_log, norm_w, bsz, seq):
    t = GDN_CHUNK
    nc = seq // t
    hb = GDN_HB
    d = GDN_HEAD_DIM
    w = hb * d
    ng = GDN_HEADS // hb

    def row(b, hg, c):
        return b * nc + c

    src = jnp.arange(LANES)[None, :, None]
    col_head = (jnp.arange(2 * w) // d)[None, None, :]
    hg_idx = jnp.arange(ng)[:, None, None]
    want = jnp.where(col_head < hb, LANE_GDN_A + hg_idx * hb + col_head,
                     LANE_GDN_B + hg_idx * hb + (col_head - hb))
    rep = (src == want).astype(BF16)

    in_specs = [
        pl.BlockSpec((t, w), lambda b, hg, c: (row(b, hg, c), OFF_GDN_Q // w + hg)),
        pl.BlockSpec((t, w), lambda b, hg, c: (row(b, hg, c), OFF_GDN_K // w + hg)),
        pl.BlockSpec((t, w), lambda b, hg, c: (row(b, hg, c), OFF_GDN_V // w + hg)),
        pl.BlockSpec((t, w), lambda b, hg, c: (row(b, hg, c), OFF_GDN_Z // w + hg)),
        pl.BlockSpec((t, LANES), lambda b, hg, c: (row(b, hg, c), 2)),
        pl.BlockSpec((GDN_CONV, w), lambda b, hg, c: (0, hg)),
        pl.BlockSpec((GDN_CONV, w), lambda b, hg, c: (0, ng + hg)),
        pl.BlockSpec((GDN_CONV, w), lambda b, hg, c: (0, 2 * ng + hg)),
        pl.BlockSpec((1, LANES), lambda b, hg, c: (0, 0)),
        pl.BlockSpec((1, LANES), lambda b, hg, c: (0, 0)),
        pl.BlockSpec((1, d), lambda b, hg, c: (0, 0)),
        pl.BlockSpec((None, LANES, 2 * w), lambda b, hg, c: (hg, 0, 0)),
    ]
    return pl.pallas_call(
        _gdn_kernel,
        out_shape=jax.ShapeDtypeStruct((bsz * seq, GDN_WIDTH), BF16),
        grid=(bsz, ng, nc),
        in_specs=in_specs,
        out_specs=pl.BlockSpec((t, w), lambda b, hg, c: (row(b, hg, c), hg)),
        scratch_shapes=[pltpu.VMEM((t + 8, 3 * w), F32), pltpu.VMEM((hb, d, d), F32)],
        compiler_params=_cparams(("parallel", "parallel", "arbitrary")),
        name="gdn_mixer",
    )(main, main, main, main, small, conv_w, conv_w, conv_w,
      _at_lanes(dt_bias, LANE_GDN_A), _at_lanes(a_log, LANE_GDN_A), norm_w.reshape(1, d), rep)


def _merge_kernel(ys_ref, yf_ref, yg_ref, ws_ref, wf_ref, wg_ref, gs_ref, gf_ref, gg_ref, o_ref,
                  wsb_ref, wfb_ref, wgb_ref):
    @pl.when(pl.program_id(1) == 0)
    def _():
        wsb_ref[...] = ws_ref[...].astype(BF16)
        wfb_ref[...] = wf_ref[...].astype(BF16)
        wgb_ref[...] = wg_ref[...].astype(BF16)

    acc = _sigmoid(gs_ref[...].astype(F32)) * _dot(ys_ref[...], wsb_ref[...])
    acc = acc + _sigmoid(gf_ref[...].astype(F32)) * _dot(yf_ref[...], wfb_ref[...])
    acc = acc + _sigmoid(gg_ref[...].astype(F32)) * _dot(yg_ref[...], wgb_ref[...])
    o_ref[...] = acc.astype(o_ref.dtype)


def merge_branches(y_ssm, y_fox, y_gdn, w_ssm, w_fox, w_gdn, main, layer, tm=TM, tn=512):
    m = y_ssm.shape[0]
    n = D_MODEL
    gate_blk = OFF_GATE // tn
    per_gate = D_MODEL // tn

    def lhs(width):
        return pl.BlockSpec((tm, width), lambda j, i: (i, 0))

    def rhs(width):
        return pl.BlockSpec((None, width, tn), lambda j, i: (layer, 0, j))

    def gate(idx):
        return pl.BlockSpec((tm, tn), lambda j, i: (i, gate_blk + idx * per_gate + j))

    return pl.pallas_call(
        _merge_kernel,
        out_shape=jax.ShapeDtypeStruct((m, n), BF16),
        grid=(n // tn, m // tm),
        in_specs=[lhs(SSM_INNER), lhs(FOX_WIDTH), lhs(GDN_WIDTH),
                  rhs(SSM_INNER), rhs(FOX_WIDTH), rhs(GDN_WIDTH),
                  gate(0), gate(1), gate(2)],
        out_specs=pl.BlockSpec((tm, tn), lambda j, i: (i, j)),
        scratch_shapes=[pltpu.VMEM((SSM_INNER, tn), BF16), pltpu.VMEM((FOX_WIDTH, tn), BF16),
                        pltpu.VMEM((GDN_WIDTH, tn), BF16)],
        compiler_params=_cparams(("arbitrary", "arbitrary")),
        name="merge_branches",
    )(y_ssm, y_fox, y_gdn, w_ssm, w_fox, w_gdn, main, main, main)


def _out_proj_kernel(a_ref, w_ref, x_ref, o_ref, wb_ref):
    @pl.when(pl.program_id(1) == 0)
    def _():
        wb_ref[...] = w_ref[...].astype(BF16)

    o_ref[...] = x_ref[...] + _dot(a_ref[...], wb_ref[...])


def out_proj(a, w, x, layer, tm=TM, tn=512):
    m, k = a.shape
    n = w.shape[2]
    return pl.pallas_call(
        _out_proj_kernel,
        out_shape=jax.ShapeDtypeStruct((m, n), F32),
        grid=(n // tn, m // tm),
        in_specs=[pl.BlockSpec((tm, k), lambda j, i: (i, 0)),
                  pl.BlockSpec((None, k, tn), lambda j, i: (layer, 0, j)),
                  pl.BlockSpec((tm, tn), lambda j, i: (i, j))],
        out_specs=pl.BlockSpec((tm, tn), lambda j, i: (i, j)),
        scratch_shapes=[pltpu.VMEM((k, tn), BF16)],
        compiler_params=_cparams(("arbitrary", "arbitrary")),
        name="out_proj",
    )(a, w, x)


def _residual_matmul_kernel(a_ref, w_ref, x_ref, o_ref):
    kk = pl.program_id(2)

    @pl.when(kk == 0)
    def _():
        o_ref[...] = x_ref[...] + _dot(a_ref[...], w_ref[...])

    @pl.when(kk > 0)
    def _():
        o_ref[...] += _dot(a_ref[...], w_ref[...])


def ffn_down(a, w, x, layer, tm=TM, tn=512, tk=D_FF // 2):
    m, k = a.shape
    n = w.shape[2]
    return pl.pallas_call(
        _residual_matmul_kernel,
        out_shape=jax.ShapeDtypeStruct((m, n), F32),
        grid=(m // tm, n // tn, k // tk),
        in_specs=[pl.BlockSpec((tm, tk), lambda i, j, kk: (i, kk)),
                  pl.BlockSpec((None, tk, tn), lambda i, j, kk: (layer, kk, j)),
                  pl.BlockSpec((tm, tn), lambda i, j, kk: (i, j))],
        out_specs=pl.BlockSpec((tm, tn), lambda i, j, kk: (i, j)),
        compiler_params=_cparams(("parallel", "parallel", "arbitrary")),
        name="ffn_down",
    )(a, w, x)


def _ffn_up_kernel(h_ref, wg_ref, wv_ref, cwg_ref, cwv_ref, cbg_ref, cbv_ref, o_ref,
                   ug_ref, uv_ref, tg_ref, tv_ref, *, tiles_per_seq):
    tm = h_ref.shape[0]
    i = pl.program_id(0)
    j = pl.program_id(1)

    @pl.when(i == 0)
    def _():
        tg_ref[j] = jnp.zeros(tg_ref.shape[1:], F32)
        tv_ref[j] = jnp.zeros(tv_ref.shape[1:], F32)

    first = (i % tiles_per_seq) == 0
    h = h_ref[...]

    def conv_product(u_ref, t_ref, w_ref, cw_ref, cb_ref):
        u_ref[0:HALO, :] = jnp.where(first, 0.0, t_ref[j])
        u_ref[HALO:HALO + tm, :] = _dot(h, w_ref[...])
        t_ref[j] = u_ref[tm:tm + HALO, :]
        acc = cb_ref[...]
        for tap in range(FFN_CONV):
            off = HALO - (FFN_CONV - 1) + tap
            acc = acc + cw_ref[tap:tap + 1, :] * u_ref[off:off + tm, :]
        return acc

    gate = conv_product(ug_ref, tg_ref, wg_ref, cwg_ref, cbg_ref)
    val = conv_product(uv_ref, tv_ref, wv_ref, cwv_ref, cbv_ref)
    o_ref[...] = (_silu(gate) * val).astype(o_ref.dtype)


def ffn_up(h, w_gate, w_val, cw_gate, cw_val, cb_gate, cb_val, seq, tm=TM, tn=512):
    m, k = h.shape
    nb = D_FF_PAD // tn
    kern = functools.partial(_ffn_up_kernel, tiles_per_seq=seq // tm)

    def col_spec(rows):
        return pl.BlockSpec((rows, tn), lambda i, j: (0, j))

    return pl.pallas_call(
        kern,
        out_shape=jax.ShapeDtypeStruct((m, D_FF), BF16),
        grid=(m // tm, nb),
        in_specs=[pl.BlockSpec((tm, k), lambda i, j: (i, 0)),
                  col_spec(k), col_spec(k), col_spec(FFN_CONV), col_spec(FFN_CONV), col_spec(1), col_spec(1)],
        out_specs=pl.BlockSpec((tm, tn), lambda i, j: (i, j)),
        scratch_shapes=[pltpu.VMEM((HALO + tm, tn), F32)] * 2
                       + [pltpu.VMEM((nb, HALO, tn), F32)] * 2,
        compiler_params=_cparams(("arbitrary", "arbitrary")),
        name="ffn_up",
    )(h, w_gate, w_val, cw_gate, cw_val, cb_gate, cb_val)


def _pad_ffn_up(w_up, conv_w, conv_b):
    pad = D_FF_PAD - D_FF

    def halves(a, dtype):
        return tuple(jnp.pad(p, ((0, 0), (0, pad))).astype(dtype) for p in (a[:, :D_FF], a[:, D_FF:]))

    return halves(w_up, BF16), halves(conv_w, F32), halves(conv_b.reshape(1, -1), F32)


def kernel(x, norm_mix, w_in, ssm_conv_w, ssm_conv_b, ssm_dt_bias, ssm_a_log, ssm_d, ssm_norm,
           fox_f_bias, gdn_conv_w, gdn_dt_bias, gdn_a_log, gdn_norm, w_br_ssm, w_br_fox, w_br_gdn,
           w_out, norm_ffn, w_up, ffn_conv_w, ffn_conv_b, w_down, norm_final):
    bsz, seq, d = x.shape
    depth = w_in.shape[0]
    m = bsz * seq
    xf = x.reshape(m, d)
    w_down_b = w_down.astype(BF16)
    w_in_b = w_in.astype(BF16)
    for layer in range(depth):
        h = rmsnorm(xf, norm_mix[layer], BF16)
        main = in_proj_main(h, w_in_b, layer)
        small = in_proj_small(h, w_in_b, layer)
        y_ssm = ssd_mixer(main, small, ssm_conv_w[layer], ssm_conv_b[layer], ssm_dt_bias[layer],
                          ssm_a_log[layer], ssm_d[layer], ssm_norm[layer], bsz, seq)
        cum_f = fox_cumulative_log_f(small, fox_f_bias[layer], bsz, seq)
        y_fox = fox_mixer(main, cum_f, bsz, seq)
        y_gdn = gdn_mixer(main, small, gdn_conv_w[layer], gdn_dt_bias[layer], gdn_a_log[layer],
                          gdn_norm[layer], bsz, seq)
        merged = merge_branches(y_ssm, y_fox, y_gdn, w_br_ssm, w_br_fox, w_br_gdn, main, layer)
        xf = out_proj(merged, w_out, xf, layer)
        h2 = rmsnorm(xf, norm_ffn[layer], BF16)
        w_up_p, cw_p, cb_p = _pad_ffn_up(w_up[layer], ffn_conv_w[layer], ffn_conv_b[layer])
        act = ffn_up(h2, *w_up_p, *cw_p, *cb_p, seq)
        xf = ffn_down(act, w_down_b, xf, layer)
    return rmsnorm(xf, norm_final, F32).reshape(bsz, seq, d)
```

```python
import functools

import jax
import jax.numpy as jnp
from jax import lax
from jax.experimental import pallas as pl
from jax.experimental.pallas import tpu as pltpu

F32 = jnp.float32
BF16 = jnp.bfloat16

D_MODEL = 4096
SSM_INNER = 2048
SSM_HEAD_DIM = 64
SSM_HEADS = 32
SSM_GROUPS = 4
SSM_STATE = 128
SSM_CONV = 4
SSM_XBC = SSM_INNER + 2 * SSM_GROUPS * SSM_STATE
SSM_HPG = SSM_HEADS // SSM_GROUPS
SSM_GW = SSM_HPG * SSM_HEAD_DIM
FOX_HEADS = 8
FOX_HEAD_DIM = 128
FOX_WIDTH = FOX_HEADS * FOX_HEAD_DIM
GDN_HEADS = 8
GDN_HEAD_DIM = 128
GDN_WIDTH = GDN_HEADS * GDN_HEAD_DIM
GDN_CONV = 4
D_FF = 11008
D_FF_PAD = 11264
FFN_CONV = 3
NORM_EPS = 1e-6
LOG2_E = 1.4426950408889634
N_MAIN = 24576
LANES = 128
HALO = 16

OFF_SSM_Z = 0
OFF_SSM_X = 2048
OFF_SSM_B = 4096
OFF_SSM_C = 4608
OFF_FOX_Q = 5120
OFF_FOX_K = 6144
OFF_FOX_V = 7168
OFF_GDN_Q = 8192
OFF_GDN_K = 9216
OFF_GDN_V = 10240
OFF_GDN_Z = 11264
OFF_GATE = 12288
MAIN_SEGMENTS = ((0, 0), (OFF_FOX_Q, SSM_HEADS), (OFF_GDN_Q, SSM_HEADS + FOX_HEADS),
                 (OFF_GDN_Z, SSM_HEADS + FOX_HEADS + 2 * GDN_HEADS))
SMALL_BLOCKS = (OFF_FOX_Q // LANES, (OFF_GDN_Q + SSM_HEADS) // LANES,
                (OFF_GDN_Z + SSM_HEADS + FOX_HEADS) // LANES)
LANE_SSM_DT = 0
LANE_FOX_F = SSM_HEADS
LANE_GDN_A = SSM_HEADS + FOX_HEADS
LANE_GDN_B = SSM_HEADS + FOX_HEADS + GDN_HEADS
N_SMALL = len(SMALL_BLOCKS) * LANES

SSD_CHUNK = 128
GDN_CHUNK = 128
GDN_HB = 8
FOX_TQ = 512
FOX_TK = 512
TM = 1024
VMEM_LIMIT = 56 * 1024 * 1024


def _cparams(sem):
    return pltpu.CompilerParams(dimension_semantics=sem, vmem_limit_bytes=VMEM_LIMIT)


def _sigmoid(x):
    return 1.0 / (1.0 + jnp.exp(-x))


def _silu(x):
    return x * _sigmoid(x)


def _softplus(x):
    return jnp.maximum(x, 0.0) + jnp.log(1.0 + jnp.exp(-jnp.abs(x)))


def _split3(x):
    hi = x.astype(BF16)
    r1 = x - hi.astype(F32)
    mid = r1.astype(BF16)
    lo = (r1 - mid.astype(F32)).astype(BF16)
    return hi, mid, lo


def _dot(a, b):
    return jnp.dot(a, b, preferred_element_type=F32)


def _dot_nt(a, b):
    return lax.dot_general(a, b, (((1,), (1,)), ((), ())), preferred_element_type=F32)


def _dot_sel_r(x, sel):
    hi, mid, lo = _split3(x)
    return (_dot(lo, sel) + _dot(mid, sel)) + _dot(hi, sel)


def _dot_sel_l(sel, x):
    hi, mid, lo = _split3(x)
    return (_dot(sel, lo) + _dot(sel, mid)) + _dot(sel, hi)


def _dot2(a, b):
    ah = a.astype(BF16)
    al = (a - ah.astype(F32)).astype(BF16)
    bh = b.astype(BF16)
    bl = (b - bh.astype(F32)).astype(BF16)
    return (_dot(al, bh) + _dot(ah, bl)) + _dot(ah, bh)


def _lower_tri(n, strict=False):
    r = lax.broadcasted_iota(jnp.int32, (n, n), 0)
    c = lax.broadcasted_iota(jnp.int32, (n, n), 1)
    return (r > c) if strict else (r >= c)


def _at_lanes(v, start):
    return jnp.pad(v, (start, LANES - start - v.shape[0])).reshape(1, LANES)


def _rmsnorm_kernel(x_ref, w_ref, o_ref):
    x = x_ref[...]
    ms = jnp.mean(x * x, axis=-1, keepdims=True)
    o_ref[...] = (x * lax.rsqrt(ms + NORM_EPS) * w_ref[...]).astype(o_ref.dtype)


def rmsnorm(x, w, out_dtype, tm=512):
    m, d = x.shape
    return pl.pallas_call(
        _rmsnorm_kernel,
        out_shape=jax.ShapeDtypeStruct((m, d), out_dtype),
        grid=(m // tm,),
        in_specs=[pl.BlockSpec((tm, d), lambda i: (i, 0)),
                  pl.BlockSpec((1, d), lambda i: (0, 0))],
        out_specs=pl.BlockSpec((tm, d), lambda i: (i, 0)),
        compiler_params=_cparams(("parallel",)),
        name="rmsnorm",
    )(x, w.reshape(1, d))


def _in_proj_kernel(a_ref, w_ref, o_ref, wb_ref):
    @pl.when(pl.program_id(1) == 0)
    def _():
        wb_ref[...] = w_ref[...].astype(BF16)

    o_ref[...] = _dot_nt(a_ref[...], wb_ref[...]).astype(o_ref.dtype)


def in_proj_main(h, w_in_t, layer, tm=TM, tn=1024):
    m, k = h.shape
    assert all(s % tn == 0 for s, _ in MAIN_SEGMENTS)

    sub = 8
    assert all(d % sub == 0 for _, d in MAIN_SEGMENTS)

    def w_row(j):
        dropped = 0
        for start, d in MAIN_SEGMENTS[1:]:
            dropped = jnp.where(j >= start // tn, d // sub, dropped)
        return (j * (tn // sub) + dropped) * sub

    return pl.pallas_call(
        _in_proj_kernel,
        out_shape=jax.ShapeDtypeStruct((m, N_MAIN), BF16),
        grid=(N_MAIN // tn, m // tm),
        in_specs=[pl.BlockSpec((tm, k), lambda j, i: (i, 0)),
                  pl.BlockSpec((None, pl.Element(tn), pl.Element(k)), lambda j, i: (layer, w_row(j), 0),
                               pipeline_mode=pl.Buffered(1))],
        out_specs=pl.BlockSpec((tm, tn), lambda j, i: (i, j)),
        scratch_shapes=[pltpu.VMEM((tn, k), BF16)],
        compiler_params=_cparams(("arbitrary", "arbitrary")),
        name="in_proj_main",
    )(h, w_in_t)


def _in_proj_small_kernel(a_ref, w0_ref, w1_ref, w2_ref, o_ref):
    w = jnp.concatenate([w0_ref[...], w1_ref[...], w2_ref[...]], axis=0).astype(BF16)
    o_ref[...] = _dot_nt(a_ref[...], w)


def in_proj_small(h, w_in_t, layer):
    m, k = h.shape

    def w_spec(blk):
        return pl.BlockSpec((None, LANES, k), lambda i: (layer, blk, 0))

    return pl.pallas_call(
        _in_proj_small_kernel,
        out_shape=jax.ShapeDtypeStruct((m, N_SMALL), F32),
        grid=(m // TM,),
        in_specs=[pl.BlockSpec((TM, k), lambda i: (i, 0))] + [w_spec(b) for b in SMALL_BLOCKS],
        out_specs=pl.BlockSpec((TM, N_SMALL), lambda i: (i, 0)),
        compiler_params=_cparams(("parallel",)),
        name="in_proj_small",
    )(h, w_in_t, w_in_t, w_in_t)


def _ssd_kernel(z_ref, xs_ref, b_ref, c_ref, dt_ref, cw_ref, cb_ref, dtb_ref, alog_ref, dskip_ref,
                nw_ref, ex_ref, ecol_ref, o_ref, ext_ref, state_ref):
    t = z_ref.shape[0]
    n = SSM_STATE
    gw = SSM_GW
    g = SSM_GROUPS
    chunk = pl.program_id(1)

    @pl.when(chunk == 0)
    def _():
        ext_ref[0:8, :] = jnp.zeros((8, SSM_XBC), F32)
        state_ref[...] = jnp.zeros_like(state_ref)

    @pl.when(chunk > 0)
    def _():
        ext_ref[0:8, :] = ext_ref[t:t + 8, :]

    ext_ref[8:8 + t, 0:SSM_INNER] = xs_ref[...].astype(F32)
    ext_ref[8:8 + t, SSM_INNER:SSM_INNER + g * n] = b_ref[...].astype(F32)
    ext_ref[8:8 + t, SSM_INNER + g * n:SSM_XBC] = c_ref[...].astype(F32)

    def conv(lo, hi):
        acc = cb_ref[:, lo:hi]
        for j in range(SSM_CONV):
            off = 8 - (SSM_CONV - 1) + j
            acc = acc + cw_ref[j:j + 1, lo:hi] * ext_ref[off:off + t, lo:hi]
        return _silu(acc)

    dt = _softplus(dt_ref[...] + dtb_ref[...])
    a = -jnp.exp(alog_ref[...])
    causal = _lower_tri(t)
    tri = jnp.where(causal, 1.0, 0.0).astype(BF16)
    a_cs = _dot_sel_l(tri, dt * a)
    a_last = a_cs[t - 1:t, :]
    ex = ex_ref[...]
    dt_x = _dot_sel_r(dt, ex)
    ea_x = _dot_sel_r(jnp.exp(a_cs), ex)
    eb_x = _dot_sel_r(jnp.exp(a_last - a_cs), ex)
    a_col = _dot_sel_r(a_cs, ecol_ref[...])
    a_row = a_cs.T
    lane = lax.broadcasted_iota(jnp.int32, (t, 2 * SSM_HEAD_DIM), 1)
    z = z_ref[...].astype(F32)

    outs = []
    for gi in range(g):
        cols = slice(gi * gw, (gi + 1) * gw)
        xs = conv(gi * gw, (gi + 1) * gw)
        bm = conv(SSM_INNER + gi * n, SSM_INNER + (gi + 1) * n)
        cm = conv(SSM_INNER + (g + gi) * n, SSM_INNER + (g + gi + 1) * n)
        xdt = xs * dt_x[:, cols]
        xdt_b = xdt.astype(BF16)
        cm_b = cm.astype(BF16)
        cb = _dot_nt(cm_b, bm.astype(BF16))
        pieces = []
        for p in range(SSM_HPG // 2):
            x_pair = xdt_b[:, p * 128:(p + 1) * 128]
            res = []
            for i in (2 * p, 2 * p + 1):
                hd = gi * SSM_HPG + i
                seg = a_col[:, hd * 128:(hd + 1) * 128] - a_row[hd:hd + 1, :]
                decay = jnp.exp(jnp.where(causal, seg, -jnp.inf))
                res.append(_dot((cb * decay).astype(BF16), x_pair))
            pieces.append(jnp.where(lane < SSM_HEAD_DIM, res[0], res[1]))
        y_diag = jnp.concatenate(pieces, axis=1)

        state = state_ref[gi]
        y_off = ea_x[:, cols] * _dot(cm_b, state.astype(BF16))
        state_ref[gi] = (state * ea_x[t - 1:t, cols]
                         + _dot(bm.T.astype(BF16), (eb_x[:, cols] * xdt).astype(BF16)))

        y = y_diag + y_off + xs * dskip_ref[:, cols]
        y = y * _silu(z[:, cols])
        ms = jnp.mean(y * y, axis=-1, keepdims=True)
        outs.append(y * lax.rsqrt(ms + NORM_EPS) * nw_ref[:, cols])
    o_ref[...] = jnp.concatenate(outs, axis=1).astype(o_ref.dtype)


def ssd_mixer(main, small, conv_w, conv_b, dt_bias, a_log, d_skip, norm_w, bsz, seq):
    t = SSD_CHUNK
    nc = seq // t
    gn = SSM_GROUPS * SSM_STATE

    def row(b, c):
        return b * nc + c

    heads = jnp.arange(LANES)[:, None]
    e_x = (heads == (jnp.arange(SSM_INNER)[None, :] // SSM_HEAD_DIM)).astype(BF16)
    e_col = (heads == (jnp.arange(SSM_HEADS * LANES)[None, :] // LANES)).astype(BF16)

    def full(shape):
        return pl.BlockSpec(shape, lambda b, c: (0,) * len(shape))

    in_specs = [
        pl.BlockSpec((t, SSM_INNER), lambda b, c: (row(b, c), OFF_SSM_Z // SSM_INNER)),
        pl.BlockSpec((t, SSM_INNER), lambda b, c: (row(b, c), OFF_SSM_X // SSM_INNER)),
        pl.BlockSpec((t, gn), lambda b, c: (row(b, c), OFF_SSM_B // gn)),
        pl.BlockSpec((t, gn), lambda b, c: (row(b, c), OFF_SSM_C // gn)),
        pl.BlockSpec((t, LANES), lambda b, c: (row(b, c), 0)),
        full((SSM_CONV, SSM_XBC)), full((1, SSM_XBC)), full((1, LANES)), full((1, LANES)),
        full((1, SSM_INNER)), full((1, SSM_INNER)), full((LANES, SSM_INNER)), full((LANES, SSM_HEADS * LANES)),
    ]
    return pl.pallas_call(
        _ssd_kernel,
        out_shape=jax.ShapeDtypeStruct((bsz * seq, SSM_INNER), BF16),
        grid=(bsz, nc),
        in_specs=in_specs,
        out_specs=pl.BlockSpec((t, SSM_INNER), lambda b, c: (row(b, c), 0)),
        scratch_shapes=[pltpu.VMEM((t + 8, SSM_XBC), F32),
                        pltpu.VMEM((SSM_GROUPS, SSM_STATE, SSM_GW), F32)],
        compiler_params=_cparams(("parallel", "arbitrary")),
        name="ssd_mixer",
    )(main, main, main, main, small, conv_w, conv_b.reshape(1, SSM_XBC),
      _at_lanes(dt_bias, LANE_SSM_DT), _at_lanes(a_log, LANE_SSM_DT),
      jnp.repeat(d_skip, SSM_HEAD_DIM).reshape(1, SSM_INNER), norm_w.reshape(1, SSM_INNER), e_x, e_col)


def _fox_prep_kernel(f_ref, fb_ref, o_ref, carry_ref):
    t = f_ref.shape[0]

    @pl.when(pl.program_id(1) == 0)
    def _():
        carry_ref[...] = jnp.zeros_like(carry_ref)

    log_f = -_softplus(-(f_ref[...] + fb_ref[...]))
    tri = jnp.where(_lower_tri(t), 1.0, 0.0).astype(BF16)
    cs = _dot_sel_l(tri, log_f) + carry_ref[0:1, :]
    carry_ref[...] = jnp.broadcast_to(cs[t - 1:t, :], carry_ref.shape)
    o_ref[...] = cs.T[LANE_FOX_F:LANE_FOX_F + FOX_HEADS, :]


def fox_cumulative_log_f(small, f_bias, bsz, seq, t=512):
    nc = seq // t
    return pl.pallas_call(
        _fox_prep_kernel,
        out_shape=jax.ShapeDtypeStruct((bsz, FOX_HEADS, seq), F32),
        grid=(bsz, nc),
        in_specs=[pl.BlockSpec((t, LANES), lambda b, c: (b * nc + c, 1)),
                  pl.BlockSpec((1, LANES), lambda b, c: (0, 0))],
        out_specs=pl.BlockSpec((None, FOX_HEADS, t), lambda b, c: (b, 0, c)),
        scratch_shapes=[pltpu.VMEM((8, LANES), F32)],
        compiler_params=_cparams(("parallel", "arbitrary")),
        name="fox_cum_log_f",
    )(small, _at_lanes(f_bias, LANE_FOX_F))


def _fox_kernel(q_ref, k_ref, v_ref, cf_ref, o_ref):
    tq = q_ref.shape[0]
    tk = FOX_TK
    d = FOX_HEAD_DIM
    qi = pl.program_id(2)
    q = q_ref[...]
    scale2 = (d ** -0.5) * LOG2_E

    def step(j, carry, masked):
        m, l, acc = carry
        start = pl.multiple_of(j * tk, tk)
        k = k_ref[pl.ds(start, tk), :]
        v = v_ref[pl.ds(start, tk), :]
        s = _dot_nt(q, k) * scale2 - cf_ref[pl.ds(j, 1), :] * LOG2_E
        if masked:
            s = jnp.where(_lower_tri(tq), s, -jnp.inf)
        m_new = jnp.maximum(m, jnp.max(s, axis=-1, keepdims=True))
        alpha = jnp.exp2(m - m_new)
        p = jnp.exp2(s - m_new)
        l = alpha * l + jnp.sum(p, axis=-1, keepdims=True)
        acc = alpha * acc + _dot(p.astype(BF16), v)
        return m_new, l, acc

    m0 = jnp.full((tq, 1), -jnp.inf, F32)
    l0 = jnp.zeros((tq, 1), F32)
    acc0 = jnp.zeros((tq, d), F32)
    carry = lax.fori_loop(0, qi, lambda j, c: step(j, c, False), (m0, l0, acc0))
    _, l, acc = step(qi, carry, True)
    o_ref[...] = (acc / l).astype(o_ref.dtype)


def fox_mixer(main, cum_f, bsz, seq):
    tq = FOX_TQ
    assert FOX_TQ == FOX_TK
    nq = seq // tq
    d = FOX_HEAD_DIM
    cf = cum_f.reshape(bsz * FOX_HEADS, seq // FOX_TK, FOX_TK)
    return pl.pallas_call(
        _fox_kernel,
        out_shape=jax.ShapeDtypeStruct((bsz * seq, FOX_WIDTH), BF16),
        grid=(bsz, FOX_HEADS, nq),
        in_specs=[pl.BlockSpec((tq, d), lambda b, h, i: (b * nq + i, OFF_FOX_Q // d + h)),
                  pl.BlockSpec((seq, d), lambda b, h, i: (b, OFF_FOX_K // d + h)),
                  pl.BlockSpec((seq, d), lambda b, h, i: (b, OFF_FOX_V // d + h)),
                  pl.BlockSpec((None, seq // FOX_TK, FOX_TK), lambda b, h, i: (b * FOX_HEADS + h, 0, 0))],
        out_specs=pl.BlockSpec((tq, d), lambda b, h, i: (b * nq + i, h)),
        compiler_params=_cparams(("parallel", "parallel", "arbitrary")),
        name="fox_attention",
    )(main, main, main, cf)


INV_BASE = 32


def _unit_lower_inverse(a, blk_xor, eye):
    t = a[0].shape[0]
    n = range(len(a))
    pw = [jnp.where(blk_xor == 0, ai, 0.0) for ai in a]
    x = [eye - p for p in pw]
    for _ in range(INV_BASE.bit_length() - 2):
        pw = [_dot2(p, p) for p in pw]
        x = [x[i] + _dot2(x[i], pw[i]) for i in n]
    level = 1
    while INV_BASE * level < t:
        sel = (blk_xor >= level) & (blk_xor < 2 * level)
        xb = [xi.astype(BF16) for xi in x]
        xo = [_dot(xb[i], jnp.where(sel, a[i], 0.0).astype(BF16)).astype(BF16) for i in n]
        x = [x[i] - _dot(xo[i], xb[i]) for i in n]
        level *= 2
    return x


def _gdn_kernel(q_ref, k_ref, v_ref, z_ref, sm_ref, cwq_ref, cwk_ref, cwv_ref, dtb_ref, alog_ref,
                nw_ref, rep_ref, o_ref, ext_ref, state_ref):
    t = q_ref.shape[0]
    d = GDN_HEAD_DIM
    hb = GDN_HB
    w = hb * d
    chunk = pl.program_id(2)

    @pl.when(chunk == 0)
    def _():
        ext_ref[0:8, :] = jnp.zeros((8, 3 * w), F32)
        state_ref[...] = jnp.zeros_like(state_ref)

    @pl.when(chunk > 0)
    def _():
        ext_ref[0:8, :] = ext_ref[t:t + 8, :]

    ext_ref[8:8 + t, 0:w] = q_ref[...].astype(F32)
    ext_ref[8:8 + t, w:2 * w] = k_ref[...].astype(F32)
    ext_ref[8:8 + t, 2 * w:3 * w] = v_ref[...].astype(F32)

    def conv(lo, hi, w_ref):
        acc = None
        for j in range(GDN_CONV):
            off = 8 - (GDN_CONV - 1) + j
            term = w_ref[j:j + 1, :] * ext_ref[off:off + t, lo:hi]
            acc = term if acc is None else acc + term
        return _silu(acc)

    qc = conv(0, w, cwq_ref)
    kc = conv(w, 2 * w, cwk_ref)
    vc = conv(2 * w, 3 * w, cwv_ref)

    sm = sm_ref[...]
    lane = lax.broadcasted_iota(jnp.int32, sm.shape, 1)
    g_all = -jnp.exp(alog_ref[...]) * _softplus(sm + dtb_ref[...])
    mix = jnp.where(lane < LANE_GDN_B, g_all, _sigmoid(sm))
    rep = _dot_sel_r(mix, rep_ref[...])

    lower = _lower_tri(t)
    strict = _lower_tri(t, strict=True)
    tri = jnp.where(lower, 1.0, 0.0).astype(BF16)
    eye = jnp.where(lower, 1.0, 0.0) - jnp.where(strict, 1.0, 0.0)
    rows = lax.broadcasted_iota(jnp.int32, (t, t), 0)
    cols = lax.broadcasted_iota(jnp.int32, (t, t), 1)
    shift = INV_BASE.bit_length() - 1
    blk_xor = jnp.right_shift(rows, shift) ^ jnp.right_shift(cols, shift)
    z = z_ref[...].astype(F32)
    gc_all = _dot_sel_l(tri, rep[:, 0:w])

    hs = range(hb)
    sls = [slice(h * d, (h + 1) * d) for h in hs]

    def l2n(x):
        return x * lax.rsqrt(jnp.sum(x * x, axis=-1, keepdims=True) + 1e-6)

    q = [l2n(qc[:, sl]) * (d ** -0.5) for sl in sls]
    k = [l2n(kc[:, sl]) for sl in sls]
    beta = [rep[:, (hb + h) * d:(hb + h + 1) * d] for h in hs]
    gc = [gc_all[:, sl] for sl in sls]
    decay = [jnp.exp(jnp.where(lower, g - g.T, -jnp.inf)) for g in gc]
    eg = [jnp.exp(g) for g in gc]
    kb = [k[h] * beta[h] for h in hs]
    k_b = [x.astype(BF16) for x in k]
    a_mat = [jnp.where(strict, _dot_nt(kb[h].astype(BF16), k_b[h]) * decay[h], 0.0) for h in hs]
    attn = [(_dot_nt(q[h].astype(BF16), k_b[h]) * decay[h]).astype(BF16) for h in hs]
    rhs = [jnp.concatenate([vc[:, sls[h]] * beta[h], kb[h] * eg[h]], axis=1).astype(BF16) for h in hs]
    t_inv = _unit_lower_inverse(a_mat, blk_xor, eye)
    sol = [_dot(t_inv[h].astype(BF16), rhs[h]) for h in hs]
    s_old = [state_ref[h] for h in hs]
    s_b = [s.astype(BF16) for s in s_old]
    v_new = [(sol[h][:, :d] - _dot(sol[h][:, d:].astype(BF16), s_b[h])).astype(BF16) for h in hs]
    o = [_dot((q[h] * eg[h]).astype(BF16), s_b[h]) + _dot(attn[h], v_new[h]) for h in hs]
    g_last = [g[t - 1:t, :] for g in gc]
    k_dec = [(k[h] * jnp.exp(g_last[h] - gc[h])).T.astype(BF16) for h in hs]
    for h in hs:
        state_ref[h] = s_old[h] * jnp.exp(g_last[h]) + _dot(k_dec[h], v_new[h])
    nw = nw_ref[...]
    outs = [o[h] * lax.rsqrt(jnp.mean(o[h] * o[h], axis=-1, keepdims=True) + NORM_EPS) * nw
            * _silu(z[:, sls[h]]) for h in hs]
    o_ref[...] = jnp.concatenate(outs, axis=1).astype(o_ref.dtype)


def gdn_mixer(main, small, conv_w, dt_bias, a_log, norm_w, bsz, seq):
    t = GDN_CHUNK
    nc = seq // t
    hb = GDN_HB
    d = GDN_HEAD_DIM
    w = hb * d
    ng = GDN_HEADS // hb

    def row(b, hg, c):
        return b * nc + c

    src = jnp.arange(LANES)[None, :, None]
    col_head = (jnp.arange(2 * w) // d)[None, None, :]
    hg_idx = jnp.arange(ng)[:, None, None]
    want = jnp.where(col_head < hb, LANE_GDN_A + hg_idx * hb + col_head,
                     LANE_GDN_B + hg_idx * hb + (col_head - hb))
    rep = (src == want).astype(BF16)

    in_specs = [
        pl.BlockSpec((t, w), lambda b, hg, c: (row(b, hg, c), OFF_GDN_Q // w + hg)),
        pl.BlockSpec((t, w), lambda b, hg, c: (row(b, hg, c), OFF_GDN_K // w + hg)),
        pl.BlockSpec((t, w), lambda b, hg, c: (row(b, hg, c), OFF_GDN_V // w + hg)),
        pl.BlockSpec((t, w), lambda b, hg, c: (row(b, hg, c), OFF_GDN_Z // w + hg)),
        pl.BlockSpec((t, LANES), lambda b, hg, c: (row(b, hg, c), 2)),
        pl.BlockSpec((GDN_CONV, w), lambda b, hg, c: (0, hg)),
        pl.BlockSpec((GDN_CONV, w), lambda b, hg, c: (0, ng + hg)),
        pl.BlockSpec((GDN_CONV, w), lambda b, hg, c: (0, 2 * ng + hg)),
        pl.BlockSpec((1, LANES), lambda b, hg, c: (0, 0)),
        pl.BlockSpec((1, LANES), lambda b, hg, c: (0, 0)),
        pl.BlockSpec((1, d), lambda b, hg, c: (0, 0)),
        pl.BlockSpec((None, LANES, 2 * w), lambda b, hg, c: (hg, 0, 0)),
    ]
    return pl.pallas_call(
        _gdn_kernel,
        out_shape=jax.ShapeDtypeStruct((bsz * seq, GDN_WIDTH), BF16),
        grid=(bsz, ng, nc),
        in_specs=in_specs,
        out_specs=pl.BlockSpec((t, w), lambda b, hg, c: (row(b, hg, c), hg)),
        scratch_shapes=[pltpu.VMEM((t + 8, 3 * w), F32), pltpu.VMEM((hb, d, d), F32)],
        compiler_params=_cparams(("parallel", "parallel", "arbitrary")),
        name="gdn_mixer",
    )(main, main, main, main, small, conv_w, conv_w, conv_w,
      _at_lanes(dt_bias, LANE_GDN_A), _at_lanes(a_log, LANE_GDN_A), norm_w.reshape(1, d), rep)


def _merge_kernel(ys_ref, yf_ref, yg_ref, ws_ref, wf_ref, wg_ref, gs_ref, gf_ref, gg_ref, o_ref,
                  wsb_ref, wfb_ref, wgb_ref):
    @pl.when(pl.program_id(1) == 0)
    def _():
        wsb_ref[...] = ws_ref[...].astype(BF16)
        wfb_ref[...] = wf_ref[...].astype(BF16)
        wgb_ref[...] = wg_ref[...].astype(BF16)

    acc = _sigmoid(gs_ref[...].astype(F32)) * _dot(ys_ref[...], wsb_ref[...])
    acc = acc + _sigmoid(gf_ref[...].astype(F32)) * _dot(yf_ref[...], wfb_ref[...])
    acc = acc + _sigmoid(gg_ref[...].astype(F32)) * _dot(yg_ref[...], wgb_ref[...])
    o_ref[...] = acc.astype(o_ref.dtype)


def merge_branches(y_ssm, y_fox, y_gdn, w_ssm, w_fox, w_gdn, main, layer, tm=TM, tn=512):
    m = y_ssm.shape[0]
    n = D_MODEL
    gate_blk = OFF_GATE // tn
    per_gate = D_MODEL // tn

    def lhs(width):
        return pl.BlockSpec((tm, width), lambda j, i: (i, 0))

    def rhs(width):
        return pl.BlockSpec((None, width, tn), lambda j, i: (layer, 0, j))

    def gate(idx):
        return pl.BlockSpec((tm, tn), lambda j, i: (i, gate_blk + idx * per_gate + j))

    return pl.pallas_call(
        _merge_kernel,
        out_shape=jax.ShapeDtypeStruct((m, n), BF16),
        grid=(n // tn, m // tm),
        in_specs=[lhs(SSM_INNER), lhs(FOX_WIDTH), lhs(GDN_WIDTH),
                  rhs(SSM_INNER), rhs(FOX_WIDTH), rhs(GDN_WIDTH),
                  gate(0), gate(1), gate(2)],
        out_specs=pl.BlockSpec((tm, tn), lambda j, i: (i, j)),
        scratch_shapes=[pltpu.VMEM((SSM_INNER, tn), BF16), pltpu.VMEM((FOX_WIDTH, tn), BF16),
                        pltpu.VMEM((GDN_WIDTH, tn), BF16)],
        compiler_params=_cparams(("arbitrary", "arbitrary")),
        name="merge_branches",
    )(y_ssm, y_fox, y_gdn, w_ssm, w_fox, w_gdn, main, main, main)


def _out_proj_kernel(a_ref, w_ref, x_ref, o_ref, wb_ref):
    @pl.when(pl.program_id(1) == 0)
    def _():
        wb_ref[...] = w_ref[...].astype(BF16)

    o_ref[...] = x_ref[...] + _dot(a_ref[...], wb_ref[...])


def out_proj(a, w, x, layer, tm=TM, tn=512):
    m, k = a.shape
    n = w.shape[2]
    return pl.pallas_call(
        _out_proj_kernel,
        out_shape=jax.ShapeDtypeStruct((m, n), F32),
        grid=(n // tn, m // tm),
        in_specs=[pl.BlockSpec((tm, k), lambda j, i: (i, 0)),
                  pl.BlockSpec((None, k, tn), lambda j, i: (layer, 0, j)),
                  pl.BlockSpec((tm, tn), lambda j, i: (i, j))],
        out_specs=pl.BlockSpec((tm, tn), lambda j, i: (i, j)),
        scratch_shapes=[pltpu.VMEM((k, tn), BF16)],
        compiler_params=_cparams(("arbitrary", "arbitrary")),
        name="out_proj",
    )(a, w, x)


def _residual_matmul_kernel(a_ref, w_ref, x_ref, o_ref):
    kk = pl.program_id(2)

    @pl.when(kk == 0)
    def _():
        o_ref[...] = x_ref[...] + _dot(a_ref[...], w_ref[...])

    @pl.when(kk > 0)
    def _():
        o_ref[...] += _dot(a_ref[...], w_ref[...])


def ffn_down(a, w, x, layer, tm=TM, tn=512, tk=D_FF // 2):
    m, k = a.shape
    n = w.shape[2]
    return pl.pallas_call(
        _residual_matmul_kernel,
        out_shape=jax.ShapeDtypeStruct((m, n), F32),
        grid=(m // tm, n // tn, k // tk),
        in_specs=[pl.BlockSpec((tm, tk), lambda i, j, kk: (i, kk)),
                  pl.BlockSpec((None, tk, tn), lambda i, j, kk: (layer, kk, j)),
                  pl.BlockSpec((tm, tn), lambda i, j, kk: (i, j))],
        out_specs=pl.BlockSpec((tm, tn), lambda i, j, kk: (i, j)),
        compiler_params=_cparams(("parallel", "parallel", "arbitrary")),
        name="ffn_down",
    )(a, w, x)


def _ffn_up_kernel(h_ref, wg_ref, wv_ref, cwg_ref, cwv_ref, cbg_ref, cbv_ref, o_ref,
                   ug_ref, uv_ref, tg_ref, tv_ref, *, tiles_per_seq):
    tm = h_ref.shape[0]
    i = pl.program_id(0)
    j = pl.program_id(1)

    @pl.when(i == 0)
    def _():
        tg_ref[j] = jnp.zeros(tg_ref.shape[1:], F32)
        tv_ref[j] = jnp.zeros(tv_ref.shape[1:], F32)

    first = (i % tiles_per_seq) == 0
    h = h_ref[...]

    def conv_product(u_ref, t_ref, w_ref, cw_ref, cb_ref):
        u_ref[0:HALO, :] = jnp.where(first, 0.0, t_ref[j])
        u_ref[HALO:HALO + tm, :] = _dot(h, w_ref[...])
        t_ref[j] = u_ref[tm:tm + HALO, :]
        acc = cb_ref[...]
        for tap in range(FFN_CONV):
            off = HALO - (FFN_CONV - 1) + tap
            acc = acc + cw_ref[tap:tap + 1, :] * u_ref[off:off + tm, :]
        return acc

    gate = conv_product(ug_ref, tg_ref, wg_ref, cwg_ref, cbg_ref)
    val = conv_product(uv_ref, tv_ref, wv_ref, cwv_ref, cbv_ref)
    o_ref[...] = (_silu(gate) * val).astype(o_ref.dtype)


def ffn_up(h, w_gv, cw_gate, cw_val, cb_gate, cb_val, layer, seq, tm=TM, tn=512):
    m, k = h.shape
    nb = D_FF_PAD // tn
    kern = functools.partial(_ffn_up_kernel, tiles_per_seq=seq // tm)

    def w_spec(half):
        return pl.BlockSpec((None, None, k, tn), lambda i, j: (layer, half, 0, j))

    def col_spec(rows):
        return pl.BlockSpec((rows, tn), lambda i, j: (0, j))

    return pl.pallas_call(
        kern,
        out_shape=jax.ShapeDtypeStruct((m, D_FF), BF16),
        grid=(m // tm, nb),
        in_specs=[pl.BlockSpec((tm, k), lambda i, j: (i, 0)),
                  w_spec(0), w_spec(1), col_spec(FFN_CONV), col_spec(FFN_CONV), col_spec(1), col_spec(1)],
        out_specs=pl.BlockSpec((tm, tn), lambda i, j: (i, j)),
        scratch_shapes=[pltpu.VMEM((HALO + tm, tn), F32)] * 2
                       + [pltpu.VMEM((nb, HALO, tn), F32)] * 2,
        compiler_params=_cparams(("arbitrary", "arbitrary")),
        name="ffn_up",
    )(h, w_gv, w_gv, cw_gate, cw_val, cb_gate, cb_val)


def _w_up_prep_kernel(w_ref, o_ref, *, nb):
    o_ref[...] = jnp.where(pl.program_id(2) < nb, w_ref[...], 0.0).astype(BF16)


def w_up_prep(w_up, tn=256):
    depth, k, _ = w_up.shape
    nb = D_FF // tn
    nbp = D_FF_PAD // tn
    return pl.pallas_call(
        functools.partial(_w_up_prep_kernel, nb=nb),
        out_shape=jax.ShapeDtypeStruct((depth, 2, k, D_FF_PAD), BF16),
        grid=(depth, 2, nbp),
        in_specs=[pl.BlockSpec((None, k, tn), lambda l, half, j: (l, 0, jnp.minimum(half * nb + j, 2 * nb - 1)))],
        out_specs=pl.BlockSpec((None, None, k, tn), lambda l, half, j: (l, half, 0, j)),
        compiler_params=_cparams(("parallel", "parallel", "parallel")),
        name="w_up_prep",
    )(w_up)


def _pad_ffn_conv(conv_w, conv_b):
    pad = D_FF_PAD - D_FF

    def halves(a):
        return tuple(jnp.pad(p, ((0, 0), (0, pad))) for p in (a[:, :D_FF], a[:, D_FF:]))

    return halves(conv_w), halves(conv_b.reshape(1, -1))


def kernel(x, norm_mix, w_in, ssm_conv_w, ssm_conv_b, ssm_dt_bias, ssm_a_log, ssm_d, ssm_norm,
           fox_f_bias, gdn_conv_w, gdn_dt_bias, gdn_a_log, gdn_norm, w_br_ssm, w_br_fox, w_br_gdn,
           w_out, norm_ffn, w_up, ffn_conv_w, ffn_conv_b, w_down, norm_final):
    bsz, seq, d = x.shape
    depth = w_in.shape[0]
    m = bsz * seq
    xf = x.reshape(m, d)
    w_down_b = w_down.astype(BF16)
    w_in_t = jnp.swapaxes(w_in, 1, 2)
    w_gv = w_up_prep(w_up)
    for layer in range(depth):
        h = rmsnorm(xf, norm_mix[layer], BF16)
        main = in_proj_main(h, w_in_t, layer)
        small = in_proj_small(h, w_in_t, layer)
        y_ssm = ssd_mixer(main, small, ssm_conv_w[layer], ssm_conv_b[layer], ssm_dt_bias[layer],
                          ssm_a_log[layer], ssm_d[layer], ssm_norm[layer], bsz, seq)
        cum_f = fox_cumulative_log_f(small, fox_f_bias[layer], bsz, seq)
        y_fox = fox_mixer(main, cum_f, bsz, seq)
        y_gdn = gdn_mixer(main, small, gdn_conv_w[layer], gdn_dt_bias[layer], gdn_a_log[layer],
                          gdn_norm[layer], bsz, seq)
        merged = merge_branches(y_ssm, y_fox, y_gdn, w_br_ssm, w_br_fox, w_br_gdn, main, layer)
        xf = out_proj(merged, w_out, xf, layer)
        h2 = rmsnorm(xf, norm_ffn[layer], BF16)
        cw_p, cb_p = _pad_ffn_conv(ffn_conv_w[layer], ffn_conv_b[layer])
        act = ffn_up(h2, w_gv, *cw_p, *cb_p, layer, seq)
        xf = ffn_down(act, w_down_b, xf, layer)
    return rmsnorm(xf, norm_final, F32).reshape(bsz, seq, d)
```

```python
import functools

import jax
import jax.numpy as jnp
from jax import lax
from jax.experimental import pallas as pl
from jax.experimental.pallas import tpu as pltpu

F32 = jnp.float32
BF16 = jnp.bfloat16

D_MODEL = 4096
SSM_INNER = 2048
SSM_HEAD_DIM = 64
SSM_HEADS = 32
SSM_GROUPS = 4
SSM_STATE = 128
SSM_CONV = 4
SSM_XBC = SSM_INNER + 2 * SSM_GROUPS * SSM_STATE
SSM_HPG = SSM_HEADS // SSM_GROUPS
SSM_GW = SSM_HPG * SSM_HEAD_DIM
FOX_HEADS = 8
FOX_HEAD_DIM = 128
FOX_WIDTH = FOX_HEADS * FOX_HEAD_DIM
GDN_HEADS = 8
GDN_HEAD_DIM = 128
GDN_WIDTH = GDN_HEADS * GDN_HEAD_DIM
GDN_CONV = 4
D_FF = 11008
D_FF_PAD = 11264
FFN_CONV = 3
NORM_EPS = 1e-6
LOG2_E = 1.4426950408889634
N_MAIN = 24576
LANES = 128
HALO = 16

OFF_SSM_Z = 0
OFF_SSM_X = 2048
OFF_SSM_B = 4096
OFF_SSM_C = 4608
OFF_FOX_Q = 5120
OFF_FOX_K = 6144
OFF_FOX_V = 7168
OFF_GDN_Q = 8192
OFF_GDN_K = 9216
OFF_GDN_V = 10240
OFF_GDN_Z = 11264
OFF_GATE = 12288
MAIN_SEGMENTS = ((0, 0), (OFF_FOX_Q, SSM_HEADS), (OFF_GDN_Q, SSM_HEADS + FOX_HEADS),
                 (OFF_GDN_Z, SSM_HEADS + FOX_HEADS + 2 * GDN_HEADS))
SMALL_BLOCKS = (OFF_FOX_Q // LANES, (OFF_GDN_Q + SSM_HEADS) // LANES,
                (OFF_GDN_Z + SSM_HEADS + FOX_HEADS) // LANES)
LANE_SSM_DT = 0
LANE_FOX_F = SSM_HEADS
LANE_GDN_A = SSM_HEADS + FOX_HEADS
LANE_GDN_B = SSM_HEADS + FOX_HEADS + GDN_HEADS
N_SMALL = len(SMALL_BLOCKS) * LANES

SSD_CHUNK = 128
GDN_CHUNK = 128
GDN_HB = 8
FOX_TQ = 512
FOX_TK = 512
TM = 1024
VMEM_LIMIT = 56 * 1024 * 1024


def _cparams(sem):
    return pltpu.CompilerParams(dimension_semantics=sem, vmem_limit_bytes=VMEM_LIMIT)


def _sigmoid(x):
    return 1.0 / (1.0 + jnp.exp(-x))


def _silu(x):
    return x * _sigmoid(x)


def _softplus(x):
    return jnp.maximum(x, 0.0) + jnp.log(1.0 + jnp.exp(-jnp.abs(x)))


def _split3(x):
    hi = x.astype(BF16)
    r1 = x - hi.astype(F32)
    mid = r1.astype(BF16)
    lo = (r1 - mid.astype(F32)).astype(BF16)
    return hi, mid, lo


def _dot(a, b):
    return jnp.dot(a, b, preferred_element_type=F32)


def _dot_nt(a, b):
    return lax.dot_general(a, b, (((1,), (1,)), ((), ())), preferred_element_type=F32)


def _dot_sel_r(x, sel):
    hi, mid, lo = _split3(x)
    return (_dot(lo, sel) + _dot(mid, sel)) + _dot(hi, sel)


def _dot_sel_l(sel, x):
    hi, mid, lo = _split3(x)
    return (_dot(sel, lo) + _dot(sel, mid)) + _dot(sel, hi)


def _dot2(a, b):
    ah = a.astype(BF16)
    al = (a - ah.astype(F32)).astype(BF16)
    bh = b.astype(BF16)
    bl = (b - bh.astype(F32)).astype(BF16)
    return (_dot(al, bh) + _dot(ah, bl)) + _dot(ah, bh)


def _lower_tri(n, strict=False):
    r = lax.broadcasted_iota(jnp.int32, (n, n), 0)
    c = lax.broadcasted_iota(jnp.int32, (n, n), 1)
    return (r > c) if strict else (r >= c)


def _at_lanes(v, start):
    return jnp.pad(v, (start, LANES - start - v.shape[0])).reshape(1, LANES)


def _rmsnorm_kernel(x_ref, w_ref, o_ref):
    x = x_ref[...]
    ms = jnp.mean(x * x, axis=-1, keepdims=True)
    o_ref[...] = (x * lax.rsqrt(ms + NORM_EPS) * w_ref[...]).astype(o_ref.dtype)


def rmsnorm(x, w, out_dtype, tm=512):
    m, d = x.shape
    return pl.pallas_call(
        _rmsnorm_kernel,
        out_shape=jax.ShapeDtypeStruct((m, d), out_dtype),
        grid=(m // tm,),
        in_specs=[pl.BlockSpec((tm, d), lambda i: (i, 0)),
                  pl.BlockSpec((1, d), lambda i: (0, 0))],
        out_specs=pl.BlockSpec((tm, d), lambda i: (i, 0)),
        compiler_params=_cparams(("parallel",)),
        name="rmsnorm",
    )(x, w.reshape(1, d))


def _in_proj_kernel(a_ref, w_ref, o_ref, wb_ref):
    @pl.when(pl.program_id(1) == 0)
    def _():
        wb_ref[...] = w_ref[...].astype(BF16)

    o_ref[...] = _dot_nt(a_ref[...], wb_ref[...]).astype(o_ref.dtype)


def in_proj_main(h, w_in_t, layer, tm=TM, tn=1024):
    m, k = h.shape
    assert all(s % tn == 0 for s, _ in MAIN_SEGMENTS)

    sub = 8
    assert all(d % sub == 0 for _, d in MAIN_SEGMENTS)

    def w_row(j):
        dropped = 0
        for start, d in MAIN_SEGMENTS[1:]:
            dropped = jnp.where(j >= start // tn, d // sub, dropped)
        return (j * (tn // sub) + dropped) * sub

    return pl.pallas_call(
        _in_proj_kernel,
        out_shape=jax.ShapeDtypeStruct((m, N_MAIN), BF16),
        grid=(N_MAIN // tn, m // tm),
        in_specs=[pl.BlockSpec((tm, k), lambda j, i: (i, 0)),
                  pl.BlockSpec((None, pl.Element(tn), pl.Element(k)), lambda j, i: (layer, w_row(j), 0),
                               pipeline_mode=pl.Buffered(1))],
        out_specs=pl.BlockSpec((tm, tn), lambda j, i: (i, j)),
        scratch_shapes=[pltpu.VMEM((tn, k), BF16)],
        compiler_params=_cparams(("arbitrary", "arbitrary")),
        name="in_proj_main",
    )(h, w_in_t)


def _rmsnorm_small_kernel(x_ref, w_ref, w0_ref, w1_ref, w2_ref, o_ref, sm_ref):
    x = x_ref[...]
    ms = jnp.mean(x * x, axis=-1, keepdims=True)
    h = (x * lax.rsqrt(ms + NORM_EPS) * w_ref[...]).astype(o_ref.dtype)
    o_ref[...] = h
    ws = jnp.concatenate([w0_ref[...], w1_ref[...], w2_ref[...]], axis=0).astype(BF16)
    sm_ref[...] = _dot_nt(h, ws)


def rmsnorm_small(x, w, w_in_t, layer, tm=512):
    m, d = x.shape

    def w_spec(blk):
        return pl.BlockSpec((None, LANES, d), lambda i: (layer, blk, 0))

    return pl.pallas_call(
        _rmsnorm_small_kernel,
        out_shape=(jax.ShapeDtypeStruct((m, d), BF16), jax.ShapeDtypeStruct((m, N_SMALL), F32)),
        grid=(m // tm,),
        in_specs=[pl.BlockSpec((tm, d), lambda i: (i, 0)),
                  pl.BlockSpec((1, d), lambda i: (0, 0))] + [w_spec(b) for b in SMALL_BLOCKS],
        out_specs=(pl.BlockSpec((tm, d), lambda i: (i, 0)), pl.BlockSpec((tm, N_SMALL), lambda i: (i, 0))),
        compiler_params=_cparams(("parallel",)),
        name="rmsnorm_small",
    )(x, w.reshape(1, d), w_in_t, w_in_t, w_in_t)


def _ssd_kernel(z_ref, xs_ref, b_ref, c_ref, dt_ref, cw_ref, cb_ref, dtb_ref, alog_ref, dskip_ref,
                nw_ref, ex_ref, ecol_ref, o_ref, ext_ref, state_ref):
    t = z_ref.shape[0]
    n = SSM_STATE
    gw = SSM_GW
    g = SSM_GROUPS
    chunk = pl.program_id(1)

    @pl.when(chunk == 0)
    def _():
        ext_ref[0:8, :] = jnp.zeros((8, SSM_XBC), F32)
        state_ref[...] = jnp.zeros_like(state_ref)

    @pl.when(chunk > 0)
    def _():
        ext_ref[0:8, :] = ext_ref[t:t + 8, :]

    ext_ref[8:8 + t, 0:SSM_INNER] = xs_ref[...].astype(F32)
    ext_ref[8:8 + t, SSM_INNER:SSM_INNER + g * n] = b_ref[...].astype(F32)
    ext_ref[8:8 + t, SSM_INNER + g * n:SSM_XBC] = c_ref[...].astype(F32)

    def conv(lo, hi):
        acc = cb_ref[:, lo:hi]
        for j in range(SSM_CONV):
            off = 8 - (SSM_CONV - 1) + j
            acc = acc + cw_ref[j:j + 1, lo:hi] * ext_ref[off:off + t, lo:hi]
        return _silu(acc)

    dt = _softplus(dt_ref[...] + dtb_ref[...])
    a = -jnp.exp(alog_ref[...])
    causal = _lower_tri(t)
    tri = jnp.where(causal, 1.0, 0.0).astype(BF16)
    a_cs = _dot_sel_l(tri, dt * a)
    a_last = a_cs[t - 1:t, :]
    ex = ex_ref[...]
    dt_x = _dot_sel_r(dt, ex)
    ea_x = _dot_sel_r(jnp.exp(a_cs), ex)
    eb_x = _dot_sel_r(jnp.exp(a_last - a_cs), ex)
    a_col = _dot_sel_r(a_cs, ecol_ref[...])
    a_row = a_cs.T
    lane = lax.broadcasted_iota(jnp.int32, (t, 2 * SSM_HEAD_DIM), 1)
    z = z_ref[...].astype(F32)

    outs = []
    for gi in range(g):
        cols = slice(gi * gw, (gi + 1) * gw)
        xs = conv(gi * gw, (gi + 1) * gw)
        bm = conv(SSM_INNER + gi * n, SSM_INNER + (gi + 1) * n)
        cm = conv(SSM_INNER + (g + gi) * n, SSM_INNER + (g + gi + 1) * n)
        xdt = xs * dt_x[:, cols]
        xdt_b = xdt.astype(BF16)
        cm_b = cm.astype(BF16)
        cb = _dot_nt(cm_b, bm.astype(BF16))
        pieces = []
        for p in range(SSM_HPG // 2):
            x_pair = xdt_b[:, p * 128:(p + 1) * 128]
            res = []
            for i in (2 * p, 2 * p + 1):
                hd = gi * SSM_HPG + i
                seg = a_col[:, hd * 128:(hd + 1) * 128] - a_row[hd:hd + 1, :]
                decay = jnp.exp(jnp.where(causal, seg, -jnp.inf))
                res.append(_dot((cb * decay).astype(BF16), x_pair))
            pieces.append(jnp.where(lane < SSM_HEAD_DIM, res[0], res[1]))
        y_diag = jnp.concatenate(pieces, axis=1)

        state = state_ref[gi]
        y_off = ea_x[:, cols] * _dot(cm_b, state.astype(BF16))
        state_ref[gi] = (state * ea_x[t - 1:t, cols]
                         + _dot(bm.T.astype(BF16), (eb_x[:, cols] * xdt).astype(BF16)))

        y = y_diag + y_off + xs * dskip_ref[:, cols]
        y = y * _silu(z[:, cols])
        ms = jnp.mean(y * y, axis=-1, keepdims=True)
        outs.append(y * lax.rsqrt(ms + NORM_EPS) * nw_ref[:, cols])
    o_ref[...] = jnp.concatenate(outs, axis=1).astype(o_ref.dtype)


def ssd_mixer(main, small, conv_w, conv_b, dt_bias, a_log, d_skip, norm_w, bsz, seq):
    t = SSD_CHUNK
    nc = seq // t
    gn = SSM_GROUPS * SSM_STATE

    def row(b, c):
        return b * nc + c

    heads = jnp.arange(LANES)[:, None]
    e_x = (heads == (jnp.arange(SSM_INNER)[None, :] // SSM_HEAD_DIM)).astype(BF16)
    e_col = (heads == (jnp.arange(SSM_HEADS * LANES)[None, :] // LANES)).astype(BF16)

    def full(shape):
        return pl.BlockSpec(shape, lambda b, c: (0,) * len(shape))

    in_specs = [
        pl.BlockSpec((t, SSM_INNER), lambda b, c: (row(b, c), OFF_SSM_Z // SSM_INNER)),
        pl.BlockSpec((t, SSM_INNER), lambda b, c: (row(b, c), OFF_SSM_X // SSM_INNER)),
        pl.BlockSpec((t, gn), lambda b, c: (row(b, c), OFF_SSM_B // gn)),
        pl.BlockSpec((t, gn), lambda b, c: (row(b, c), OFF_SSM_C // gn)),
        pl.BlockSpec((t, LANES), lambda b, c: (row(b, c), 0)),
        full((SSM_CONV, SSM_XBC)), full((1, SSM_XBC)), full((1, LANES)), full((1, LANES)),
        full((1, SSM_INNER)), full((1, SSM_INNER)), full((LANES, SSM_INNER)), full((LANES, SSM_HEADS * LANES)),
    ]
    return pl.pallas_call(
        _ssd_kernel,
        out_shape=jax.ShapeDtypeStruct((bsz * seq, SSM_INNER), BF16),
        grid=(bsz, nc),
        in_specs=in_specs,
        out_specs=pl.BlockSpec((t, SSM_INNER), lambda b, c: (row(b, c), 0)),
        scratch_shapes=[pltpu.VMEM((t + 8, SSM_XBC), F32),
                        pltpu.VMEM((SSM_GROUPS, SSM_STATE, SSM_GW), F32)],
        compiler_params=_cparams(("parallel", "arbitrary")),
        name="ssd_mixer",
    )(main, main, main, main, small, conv_w, conv_b.reshape(1, SSM_XBC),
      _at_lanes(dt_bias, LANE_SSM_DT), _at_lanes(a_log, LANE_SSM_DT),
      jnp.repeat(d_skip, SSM_HEAD_DIM).reshape(1, SSM_INNER), norm_w.reshape(1, SSM_INNER), e_x, e_col)


def _fox_prep_kernel(f_ref, fb_ref, o_ref, carry_ref):
    t = f_ref.shape[0]

    @pl.when(pl.program_id(1) == 0)
    def _():
        carry_ref[...] = jnp.zeros_like(carry_ref)

    log_f = -_softplus(-(f_ref[...] + fb_ref[...]))
    tri = jnp.where(_lower_tri(t), 1.0, 0.0).astype(BF16)
    cs = _dot_sel_l(tri, log_f) + carry_ref[0:1, :]
    carry_ref[...] = jnp.broadcast_to(cs[t - 1:t, :], carry_ref.shape)
    o_ref[...] = cs.T[LANE_FOX_F:LANE_FOX_F + FOX_HEADS, :]


def fox_cumulative_log_f(small, f_bias, bsz, seq, t=512):
    nc = seq // t
    return pl.pallas_call(
        _fox_prep_kernel,
        out_shape=jax.ShapeDtypeStruct((bsz, FOX_HEADS, seq), F32),
        grid=(bsz, nc),
        in_specs=[pl.BlockSpec((t, LANES), lambda b, c: (b * nc + c, 1)),
                  pl.BlockSpec((1, LANES), lambda b, c: (0, 0))],
        out_specs=pl.BlockSpec((None, FOX_HEADS, t), lambda b, c: (b, 0, c)),
        scratch_shapes=[pltpu.VMEM((8, LANES), F32)],
        compiler_params=_cparams(("parallel", "arbitrary")),
        name="fox_cum_log_f",
    )(small, _at_lanes(f_bias, LANE_FOX_F))


def _fox_kernel(q_ref, k_ref, v_ref, cf_ref, o_ref):
    tq = q_ref.shape[0]
    tk = FOX_TK
    d = FOX_HEAD_DIM
    qi = pl.program_id(2)
    q = q_ref[...]
    scale2 = (d ** -0.5) * LOG2_E

    def step(j, carry, masked):
        m, l, acc = carry
        start = pl.multiple_of(j * tk, tk)
        k = k_ref[pl.ds(start, tk), :]
        v = v_ref[pl.ds(start, tk), :]
        s = _dot_nt(q, k) * scale2 - cf_ref[pl.ds(j, 1), :] * LOG2_E
        if masked:
            s = jnp.where(_lower_tri(tq), s, -jnp.inf)
        m_new = jnp.maximum(m, jnp.max(s, axis=-1, keepdims=True))
        alpha = jnp.exp2(m - m_new)
        p = jnp.exp2(s - m_new)
        l = alpha * l + jnp.sum(p, axis=-1, keepdims=True)
        acc = alpha * acc + _dot(p.astype(BF16), v)
        return m_new, l, acc

    m0 = jnp.full((tq, 1), -jnp.inf, F32)
    l0 = jnp.zeros((tq, 1), F32)
    acc0 = jnp.zeros((tq, d), F32)
    carry = lax.fori_loop(0, qi, lambda j, c: step(j, c, False), (m0, l0, acc0))
    _, l, acc = step(qi, carry, True)
    o_ref[...] = (acc / l).astype(o_ref.dtype)


def fox_mixer(main, cum_f, bsz, seq):
    tq = FOX_TQ
    assert FOX_TQ == FOX_TK
    nq = seq // tq
    d = FOX_HEAD_DIM
    cf = cum_f.reshape(bsz * FOX_HEADS, seq // FOX_TK, FOX_TK)
    return pl.pallas_call(
        _fox_kernel,
        out_shape=jax.ShapeDtypeStruct((bsz * seq, FOX_WIDTH), BF16),
        grid=(bsz, FOX_HEADS, nq),
        in_specs=[pl.BlockSpec((tq, d), lambda b, h, i: (b * nq + i, OFF_FOX_Q // d + h)),
                  pl.BlockSpec((seq, d), lambda b, h, i: (b, OFF_FOX_K // d + h)),
                  pl.BlockSpec((seq, d), lambda b, h, i: (b, OFF_FOX_V // d + h)),
                  pl.BlockSpec((None, seq // FOX_TK, FOX_TK), lambda b, h, i: (b * FOX_HEADS + h, 0, 0))],
        out_specs=pl.BlockSpec((tq, d), lambda b, h, i: (b * nq + i, h)),
        compiler_params=_cparams(("parallel", "parallel", "arbitrary")),
        name="fox_attention",
    )(main, main, main, cf)


INV_BASE = 32


def _unit_lower_inverse(a, blk_xor, eye):
    t = a[0].shape[0]
    n = range(len(a))
    pw = [jnp.where(blk_xor == 0, ai, 0.0) for ai in a]
    x = [eye - p for p in pw]
    for _ in range(INV_BASE.bit_length() - 2):
        pw = [_dot2(p, p) for p in pw]
        x = [x[i] + _dot2(x[i], pw[i]) for i in n]
    level = 1
    while INV_BASE * level < t:
        sel = (blk_xor >= level) & (blk_xor < 2 * level)
        xb = [xi.astype(BF16) for xi in x]
        xo = [_dot(xb[i], jnp.where(sel, a[i], 0.0).astype(BF16)).astype(BF16) for i in n]
        x = [x[i] - _dot(xo[i], xb[i]) for i in n]
        level *= 2
    return x


def _gdn_kernel(q_ref, k_ref, v_ref, z_ref, sm_ref, cwq_ref, cwk_ref, cwv_ref, dtb_ref, alog_ref,
                nw_ref, rep_ref, o_ref, ext_ref, state_ref):
    t = q_ref.shape[0]
    d = GDN_HEAD_DIM
    hb = GDN_HB
    w = hb * d
    chunk = pl.program_id(2)

    @pl.when(chunk == 0)
    def _():
        ext_ref[0:8, :] = jnp.zeros((8, 3 * w), F32)
        state_ref[...] = jnp.zeros_like(state_ref)

    @pl.when(chunk > 0)
    def _():
        ext_ref[0:8, :] = ext_ref[t:t + 8, :]

    ext_ref[8:8 + t, 0:w] = q_ref[...].astype(F32)
    ext_ref[8:8 + t, w:2 * w] = k_ref[...].astype(F32)
    ext_ref[8:8 + t, 2 * w:3 * w] = v_ref[...].astype(F32)

    def conv(lo, hi, w_ref):
        acc = None
        for j in range(GDN_CONV):
            off = 8 - (GDN_CONV - 1) + j
            term = w_ref[j:j + 1, :] * ext_ref[off:off + t, lo:hi]
            acc = term if acc is None else acc + term
        return _silu(acc)

    qc = conv(0, w, cwq_ref)
    kc = conv(w, 2 * w, cwk_ref)
    vc = conv(2 * w, 3 * w, cwv_ref)

    sm = sm_ref[...]
    lane = lax.broadcasted_iota(jnp.int32, sm.shape, 1)
    g_all = -jnp.exp(alog_ref[...]) * _softplus(sm + dtb_ref[...])
    mix = jnp.where(lane < LANE_GDN_B, g_all, _sigmoid(sm))
    rep = _dot_sel_r(mix, rep_ref[...])

    lower = _lower_tri(t)
    strict = _lower_tri(t, strict=True)
    tri = jnp.where(lower, 1.0, 0.0).astype(BF16)
    eye = jnp.where(lower, 1.0, 0.0) - jnp.where(strict, 1.0, 0.0)
    rows = lax.broadcasted_iota(jnp.int32, (t, t), 0)
    cols = lax.broadcasted_iota(jnp.int32, (t, t), 1)
    shift = INV_BASE.bit_length() - 1
    blk_xor = jnp.right_shift(rows, shift) ^ jnp.right_shift(cols, shift)
    z = z_ref[...].astype(F32)
    gc_all = _dot_sel_l(tri, rep[:, 0:w])

    hs = range(hb)
    sls = [slice(h * d, (h + 1) * d) for h in hs]

    def l2n(x):
        return x * lax.rsqrt(jnp.sum(x * x, axis=-1, keepdims=True) + 1e-6)

    q = [l2n(qc[:, sl]) * (d ** -0.5) for sl in sls]
    k = [l2n(kc[:, sl]) for sl in sls]
    beta = [rep[:, (hb + h) * d:(hb + h + 1) * d] for h in hs]
    gc = [gc_all[:, sl] for sl in sls]
    decay = [jnp.exp(jnp.where(lower, g - g.T, -jnp.inf)) for g in gc]
    eg = [jnp.exp(g) for g in gc]
    kb = [k[h] * beta[h] for h in hs]
    k_b = [x.astype(BF16) for x in k]
    a_mat = [jnp.where(strict, _dot_nt(kb[h].astype(BF16), k_b[h]) * decay[h], 0.0) for h in hs]
    attn = [(_dot_nt(q[h].astype(BF16), k_b[h]) * decay[h]).astype(BF16) for h in hs]
    rhs = [jnp.concatenate([vc[:, sls[h]] * beta[h], kb[h] * eg[h]], axis=1).astype(BF16) for h in hs]
    t_inv = _unit_lower_inverse(a_mat, blk_xor, eye)
    sol = [_dot(t_inv[h].astype(BF16), rhs[h]) for h in hs]
    s_old = [state_ref[h] for h in hs]
    s_b = [s.astype(BF16) for s in s_old]
    v_new = [(sol[h][:, :d] - _dot(sol[h][:, d:].astype(BF16), s_b[h])).astype(BF16) for h in hs]
    o = [_dot((q[h] * eg[h]).astype(BF16), s_b[h]) + _dot(attn[h], v_new[h]) for h in hs]
    g_last = [g[t - 1:t, :] for g in gc]
    k_dec = [(k[h] * jnp.exp(g_last[h] - gc[h])).T.astype(BF16) for h in hs]
    for h in hs:
        state_ref[h] = s_old[h] * jnp.exp(g_last[h]) + _dot(k_dec[h], v_new[h])
    nw = nw_ref[...]
    outs = [o[h] * lax.rsqrt(jnp.mean(o[h] * o[h], axis=-1, keepdims=True) + NORM_EPS) * nw
            * _silu(z[:, sls[h]]) for h in hs]
    o_ref[...] = jnp.concatenate(outs, axis=1).astype(o_ref.dtype)


def gdn_mixer(main, small, conv_w, dt_bias, a_log, norm_w, bsz, seq):
    t = GDN_CHUNK
    nc = seq // t
    hb = GDN_HB
    d = GDN_HEAD_DIM
    w = hb * d
    ng = GDN_HEADS // hb

    def row(b, hg, c):
        return b * nc + c

    src = jnp.arange(LANES)[None, :, None]
    col_head = (jnp.arange(2 * w) // d)[None, None, :]
    hg_idx = jnp.arange(ng)[:, None, None]
    want = jnp.where(col_head < hb, LANE_GDN_A + hg_idx * hb + col_head,
                     LANE_GDN_B + hg_idx * hb + (col_head - hb))
    rep = (src == want).astype(BF16)

    in_specs = [
        pl.BlockSpec((t, w), lambda b, hg, c: (row(b, hg, c), OFF_GDN_Q // w + hg)),
        pl.BlockSpec((t, w), lambda b, hg, c: (row(b, hg, c), OFF_GDN_K // w + hg)),
        pl.BlockSpec((t, w), lambda b, hg, c: (row(b, hg, c), OFF_GDN_V // w + hg)),
        pl.BlockSpec((t, w), lambda b, hg, c: (row(b, hg, c), OFF_GDN_Z // w + hg)),
        pl.BlockSpec((t, LANES), lambda b, hg, c: (row(b, hg, c), 2)),
        pl.BlockSpec((GDN_CONV, w), lambda b, hg, c: (0, hg)),
        pl.BlockSpec((GDN_CONV, w), lambda b, hg, c: (0, ng + hg)),
        pl.BlockSpec((GDN_CONV, w), lambda b, hg, c: (0, 2 * ng + hg)),
        pl.BlockSpec((1, LANES), lambda b, hg, c: (0, 0)),
        pl.BlockSpec((1, LANES), lambda b, hg, c: (0, 0)),
        pl.BlockSpec((1, d), lambda b, hg, c: (0, 0)),
        pl.BlockSpec((None, LANES, 2 * w), lambda b, hg, c: (hg, 0, 0)),
    ]
    return pl.pallas_call(
        _gdn_kernel,
        out_shape=jax.ShapeDtypeStruct((bsz * seq, GDN_WIDTH), BF16),
        grid=(bsz, ng, nc),
        in_specs=in_specs,
        out_specs=pl.BlockSpec((t, w), lambda b, hg, c: (row(b, hg, c), hg)),
        scratch_shapes=[pltpu.VMEM((t + 8, 3 * w), F32), pltpu.VMEM((hb, d, d), F32)],
        compiler_params=_cparams(("parallel", "parallel", "arbitrary")),
        name="gdn_mixer",
    )(main, main, main, main, small, conv_w, conv_w, conv_w,
      _at_lanes(dt_bias, LANE_GDN_A), _at_lanes(a_log, LANE_GDN_A), norm_w.reshape(1, d), rep)


def _merge_kernel(ys_ref, yf_ref, yg_ref, ws_ref, wf_ref, wg_ref, gs_ref, gf_ref, gg_ref, o_ref,
                  wsb_ref, wfb_ref, wgb_ref):
    @pl.when(pl.program_id(1) == 0)
    def _():
        wsb_ref[...] = ws_ref[...].astype(BF16)
        wfb_ref[...] = wf_ref[...].astype(BF16)
        wgb_ref[...] = wg_ref[...].astype(BF16)

    acc = _sigmoid(gs_ref[...].astype(F32)) * _dot(ys_ref[...], wsb_ref[...])
    acc = acc + _sigmoid(gf_ref[...].astype(F32)) * _dot(yf_ref[...], wfb_ref[...])
    acc = acc + _sigmoid(gg_ref[...].astype(F32)) * _dot(yg_ref[...], wgb_ref[...])
    o_ref[...] = acc.astype(o_ref.dtype)


def merge_branches(y_ssm, y_fox, y_gdn, w_ssm, w_fox, w_gdn, main, layer, tm=TM, tn=512):
    m = y_ssm.shape[0]
    n = D_MODEL
    gate_blk = OFF_GATE // tn
    per_gate = D_MODEL // tn

    def lhs(width):
        return pl.BlockSpec((tm, width), lambda j, i: (i, 0))

    def rhs(width):
        return pl.BlockSpec((None, width, tn), lambda j, i: (layer, 0, j))

    def gate(idx):
        return pl.BlockSpec((tm, tn), lambda j, i: (i, gate_blk + idx * per_gate + j))

    return pl.pallas_call(
        _merge_kernel,
        out_shape=jax.ShapeDtypeStruct((m, n), BF16),
        grid=(n // tn, m // tm),
        in_specs=[lhs(SSM_INNER), lhs(FOX_WIDTH), lhs(GDN_WIDTH),
                  rhs(SSM_INNER), rhs(FOX_WIDTH), rhs(GDN_WIDTH),
                  gate(0), gate(1), gate(2)],
        out_specs=pl.BlockSpec((tm, tn), lambda j, i: (i, j)),
        scratch_shapes=[pltpu.VMEM((SSM_INNER, tn), BF16), pltpu.VMEM((FOX_WIDTH, tn), BF16),
                        pltpu.VMEM((GDN_WIDTH, tn), BF16)],
        compiler_params=_cparams(("arbitrary", "arbitrary")),
        name="merge_branches",
    )(y_ssm, y_fox, y_gdn, w_ssm, w_fox, w_gdn, main, main, main)


def _out_proj_kernel(a_ref, w_ref, x_ref, o_ref, wb_ref):
    @pl.when(pl.program_id(1) == 0)
    def _():
        wb_ref[...] = w_ref[...].astype(BF16)

    o_ref[...] = x_ref[...] + _dot(a_ref[...], wb_ref[...])


def out_proj(a, w, x, layer, tm=TM, tn=512):
    m, k = a.shape
    n = w.shape[2]
    return pl.pallas_call(
        _out_proj_kernel,
        out_shape=jax.ShapeDtypeStruct((m, n), F32),
        grid=(n // tn, m // tm),
        in_specs=[pl.BlockSpec((tm, k), lambda j, i: (i, 0)),
                  pl.BlockSpec((None, k, tn), lambda j, i: (layer, 0, j)),
                  pl.BlockSpec((tm, tn), lambda j, i: (i, j))],
        out_specs=pl.BlockSpec((tm, tn), lambda j, i: (i, j)),
        scratch_shapes=[pltpu.VMEM((k, tn), BF16)],
        compiler_params=_cparams(("arbitrary", "arbitrary")),
        name="out_proj",
    )(a, w, x)


def _residual_matmul_kernel(a_ref, w_ref, x_ref, o_ref):
    kk = pl.program_id(2)

    @pl.when(kk == 0)
    def _():
        o_ref[...] = x_ref[...] + _dot(a_ref[...], w_ref[...])

    @pl.when(kk > 0)
    def _():
        o_ref[...] += _dot(a_ref[...], w_ref[...])


def ffn_down(a, w, x, layer, tm=TM, tn=512, tk=D_FF // 2):
    m, k = a.shape
    n = w.shape[2]
    return pl.pallas_call(
        _residual_matmul_kernel,
        out_shape=jax.ShapeDtypeStruct((m, n), F32),
        grid=(m // tm, n // tn, k // tk),
        in_specs=[pl.BlockSpec((tm, tk), lambda i, j, kk: (i, kk)),
                  pl.BlockSpec((None, tk, tn), lambda i, j, kk: (layer, kk, j)),
                  pl.BlockSpec((tm, tn), lambda i, j, kk: (i, j))],
        out_specs=pl.BlockSpec((tm, tn), lambda i, j, kk: (i, j)),
        compiler_params=_cparams(("parallel", "parallel", "arbitrary")),
        name="ffn_down",
    )(a, w, x)


def _ffn_up_kernel(h_ref, wg_ref, wv_ref, cwg_ref, cwv_ref, cbg_ref, cbv_ref, o_ref,
                   ug_ref, uv_ref, tg_ref, tv_ref, *, tiles_per_seq):
    tm = h_ref.shape[0]
    i = pl.program_id(0)
    j = pl.program_id(1)

    @pl.when(i == 0)
    def _():
        tg_ref[j] = jnp.zeros(tg_ref.shape[1:], F32)
        tv_ref[j] = jnp.zeros(tv_ref.shape[1:], F32)

    first = (i % tiles_per_seq) == 0
    h = h_ref[...]

    def conv_product(u_ref, t_ref, w_ref, cw_ref, cb_ref):
        u_ref[0:HALO, :] = jnp.where(first, 0.0, t_ref[j])
        u_ref[HALO:HALO + tm, :] = _dot(h, w_ref[...])
        t_ref[j] = u_ref[tm:tm + HALO, :]
        win = u_ref[HALO - 8:HALO + tm, :]
        acc = cb_ref[...] + cw_ref[FFN_CONV - 1:FFN_CONV, :] * win[8:, :]
        for back in range(1, FFN_CONV):
            tap = FFN_CONV - 1 - back
            acc = acc + cw_ref[tap:tap + 1, :] * pltpu.roll(win, back, axis=0)[8:, :]
        return acc

    gate = conv_product(ug_ref, tg_ref, wg_ref, cwg_ref, cbg_ref)
    val = conv_product(uv_ref, tv_ref, wv_ref, cwv_ref, cbv_ref)
    o_ref[...] = (_silu(gate) * val).astype(o_ref.dtype)


def ffn_up(h, w_gv, cw_gate, cw_val, cb_gate, cb_val, layer, seq, tm=TM, tn=512):
    m, k = h.shape
    nb = D_FF_PAD // tn
    kern = functools.partial(_ffn_up_kernel, tiles_per_seq=seq // tm)

    def w_spec(half):
        return pl.BlockSpec((None, None, k, tn), lambda i, j: (layer, half, 0, j))

    def col_spec(rows):
        return pl.BlockSpec((rows, tn), lambda i, j: (0, j))

    return pl.pallas_call(
        kern,
        out_shape=jax.ShapeDtypeStruct((m, D_FF), BF16),
        grid=(m // tm, nb),
        in_specs=[pl.BlockSpec((tm, k), lambda i, j: (i, 0)),
                  w_spec(0), w_spec(1), col_spec(FFN_CONV), col_spec(FFN_CONV), col_spec(1), col_spec(1)],
        out_specs=pl.BlockSpec((tm, tn), lambda i, j: (i, j)),
        scratch_shapes=[pltpu.VMEM((HALO + tm, tn), F32)] * 2
                       + [pltpu.VMEM((nb, HALO, tn), F32)] * 2,
        compiler_params=_cparams(("arbitrary", "arbitrary")),
        name="ffn_up",
    )(h, w_gv, w_gv, cw_gate, cw_val, cb_gate, cb_val)


def _w_up_prep_kernel(w_ref, o_ref, *, nb):
    o_ref[...] = jnp.where(pl.program_id(2) < nb, w_ref[...], 0.0).astype(BF16)


def w_up_prep(w_up, tn=256):
    depth, k, _ = w_up.shape
    nb = D_FF // tn
    nbp = D_FF_PAD // tn
    return pl.pallas_call(
        functools.partial(_w_up_prep_kernel, nb=nb),
        out_shape=jax.ShapeDtypeStruct((depth, 2, k, D_FF_PAD), BF16),
        grid=(depth, 2, nbp),
        in_specs=[pl.BlockSpec((None, k, tn), lambda l, half, j: (l, 0, jnp.minimum(half * nb + j, 2 * nb - 1)))],
        out_specs=pl.BlockSpec((None, None, k, tn), lambda l, half, j: (l, half, 0, j)),
        compiler_params=_cparams(("parallel", "parallel", "parallel")),
        name="w_up_prep",
    )(w_up)


def _pad_ffn_conv(conv_w, conv_b):
    pad = D_FF_PAD - D_FF

    def halves(a):
        return tuple(jnp.pad(p, ((0, 0), (0, pad))) for p in (a[:, :D_FF], a[:, D_FF:]))

    return halves(conv_w), halves(conv_b.reshape(1, -1))


def kernel(x, norm_mix, w_in, ssm_conv_w, ssm_conv_b, ssm_dt_bias, ssm_a_log, ssm_d, ssm_norm,
           fox_f_bias, gdn_conv_w, gdn_dt_bias, gdn_a_log, gdn_norm, w_br_ssm, w_br_fox, w_br_gdn,
           w_out, norm_ffn, w_up, ffn_conv_w, ffn_conv_b, w_down, norm_final):
    bsz, seq, d = x.shape
    depth = w_in.shape[0]
    m = bsz * seq
    xf = x.reshape(m, d)
    w_down_b = w_down.astype(BF16)
    w_in_t = jnp.swapaxes(w_in, 1, 2)
    w_gv = w_up_prep(w_up)
    for layer in range(depth):
        h, small = rmsnorm_small(xf, norm_mix[layer], w_in_t, layer)
        main = in_proj_main(h, w_in_t, layer)
        y_ssm = ssd_mixer(main, small, ssm_conv_w[layer], ssm_conv_b[layer], ssm_dt_bias[layer],
                          ssm_a_log[layer], ssm_d[layer], ssm_norm[layer], bsz, seq)
        cum_f = fox_cumulative_log_f(small, fox_f_bias[layer], bsz, seq)
        y_fox = fox_mixer(main, cum_f, bsz, seq)
        y_gdn = gdn_mixer(main, small, gdn_conv_w[layer], gdn_dt_bias[layer], gdn_a_log[layer],
                          gdn_norm[layer], bsz, seq)
        merged = merge_branches(y_ssm, y_fox, y_gdn, w_br_ssm, w_br_fox, w_br_gdn, main, layer)
        xf = out_proj(merged, w_out, xf, layer)
        h2 = rmsnorm(xf, norm_ffn[layer], BF16)
        cw_p, cb_p = _pad_ffn_conv(ffn_conv_w[layer], ffn_conv_b[layer])
        act = ffn_up(h2, w_gv, *cw_p, *cb_p, layer, seq)
        xf = ffn_down(act, w_down_b, xf, layer)
    return rmsnorm(xf, norm_final, F32).reshape(bsz, seq, d)
```

```python
import functools

import jax
import jax.numpy as jnp
from jax import lax
from jax.experimental import pallas as pl
from jax.experimental.pallas import tpu as pltpu

F32 = jnp.float32
BF16 = jnp.bfloat16

D_MODEL = 4096
SSM_INNER = 2048
SSM_HEAD_DIM = 64
SSM_HEADS = 32
SSM_GROUPS = 4
SSM_STATE = 128
SSM_CONV = 4
SSM_XBC = SSM_INNER + 2 * SSM_GROUPS * SSM_STATE
SSM_HPG = SSM_HEADS // SSM_GROUPS
SSM_GW = SSM_HPG * SSM_HEAD_DIM
FOX_HEADS = 8
FOX_HEAD_DIM = 128
FOX_WIDTH = FOX_HEADS * FOX_HEAD_DIM
GDN_HEADS = 8
GDN_HEAD_DIM = 128
GDN_WIDTH = GDN_HEADS * GDN_HEAD_DIM
GDN_CONV = 4
D_FF = 11008
D_FF_PAD = 11264
FFN_CONV = 3
NORM_EPS = 1e-6
LOG2_E = 1.4426950408889634
N_MAIN = 24576
LANES = 128
HALO = 16

OFF_SSM_Z = 0
OFF_SSM_X = 2048
OFF_SSM_B = 4096
OFF_SSM_C = 4608
OFF_FOX_Q = 5120
OFF_FOX_K = 6144
OFF_FOX_V = 7168
OFF_GDN_Q = 8192
OFF_GDN_K = 9216
OFF_GDN_V = 10240
OFF_GDN_Z = 11264
OFF_GATE = 12288
MAIN_SEGMENTS = ((0, 0), (OFF_FOX_Q, SSM_HEADS), (OFF_GDN_Q, SSM_HEADS + FOX_HEADS),
                 (OFF_GDN_Z, SSM_HEADS + FOX_HEADS + 2 * GDN_HEADS))
SMALL_BLOCKS = (OFF_FOX_Q // LANES, (OFF_GDN_Q + SSM_HEADS) // LANES,
                (OFF_GDN_Z + SSM_HEADS + FOX_HEADS) // LANES)
LANE_SSM_DT = 0
LANE_FOX_F = SSM_HEADS
LANE_GDN_A = SSM_HEADS + FOX_HEADS
LANE_GDN_B = SSM_HEADS + FOX_HEADS + GDN_HEADS
N_SMALL = len(SMALL_BLOCKS) * LANES

SSD_CHUNK = 128
GDN_CHUNK = 128
GDN_HB = 8
FOX_TQ = 512
FOX_TK = 512
TM = 1024
VMEM_LIMIT = 56 * 1024 * 1024


def _cparams(sem):
    return pltpu.CompilerParams(dimension_semantics=sem, vmem_limit_bytes=VMEM_LIMIT)


def _sigmoid(x):
    return 1.0 / (1.0 + jnp.exp(-x))


def _silu(x):
    return x * _sigmoid(x)


def _softplus(x):
    return jnp.maximum(x, 0.0) + jnp.log(1.0 + jnp.exp(-jnp.abs(x)))


def _split3(x):
    hi = x.astype(BF16)
    r1 = x - hi.astype(F32)
    mid = r1.astype(BF16)
    lo = (r1 - mid.astype(F32)).astype(BF16)
    return hi, mid, lo


def _dot(a, b):
    return jnp.dot(a, b, preferred_element_type=F32)


def _dot_nt(a, b):
    return lax.dot_general(a, b, (((1,), (1,)), ((), ())), preferred_element_type=F32)


def _dot_sel_r(x, sel):
    hi, mid, lo = _split3(x)
    return (_dot(lo, sel) + _dot(mid, sel)) + _dot(hi, sel)


def _dot_sel_l(sel, x):
    hi, mid, lo = _split3(x)
    return (_dot(sel, lo) + _dot(sel, mid)) + _dot(sel, hi)


def _dot2(a, b):
    ah = a.astype(BF16)
    al = (a - ah.astype(F32)).astype(BF16)
    bh = b.astype(BF16)
    bl = (b - bh.astype(F32)).astype(BF16)
    return (_dot(al, bh) + _dot(ah, bl)) + _dot(ah, bh)


def _lower_tri(n, strict=False):
    r = lax.broadcasted_iota(jnp.int32, (n, n), 0)
    c = lax.broadcasted_iota(jnp.int32, (n, n), 1)
    return (r > c) if strict else (r >= c)


def _at_lanes(v, start):
    return jnp.pad(v, (start, LANES - start - v.shape[0])).reshape(1, LANES)


def _rmsnorm_kernel(x_ref, w_ref, o_ref):
    x = x_ref[...]
    ms = jnp.mean(x * x, axis=-1, keepdims=True)
    o_ref[...] = (x * lax.rsqrt(ms + NORM_EPS) * w_ref[...]).astype(o_ref.dtype)


def rmsnorm(x, w, out_dtype, tm=512):
    m, d = x.shape
    return pl.pallas_call(
        _rmsnorm_kernel,
        out_shape=jax.ShapeDtypeStruct((m, d), out_dtype),
        grid=(m // tm,),
        in_specs=[pl.BlockSpec((tm, d), lambda i: (i, 0)),
                  pl.BlockSpec((1, d), lambda i: (0, 0))],
        out_specs=pl.BlockSpec((tm, d), lambda i: (i, 0)),
        compiler_params=_cparams(("parallel",)),
        name="rmsnorm",
    )(x, w.reshape(1, d))


F32_SUBLANES = 8


def _in_proj_kernel(a_ref, w_hbm, o_ref, wf_ref, wb_ref, sem, *, layer, tn):
    j = pl.program_id(0)
    i = pl.program_id(1)

    def window_copy(blk):
        dropped = 0
        for start, d in MAIN_SEGMENTS[1:]:
            dropped = jnp.where(blk >= start // tn, d // F32_SUBLANES, dropped)
        row = pl.multiple_of((blk * (tn // F32_SUBLANES) + dropped) * F32_SUBLANES, F32_SUBLANES)
        return pltpu.make_async_copy(w_hbm.at[layer, pl.ds(row, tn), :], wf_ref, sem)

    @pl.when((j == 0) & (i == 0))
    def _():
        window_copy(j).start()

    @pl.when(i == 0)
    def _():
        window_copy(j).wait()
        wb_ref[...] = wf_ref[...].astype(BF16)

    @pl.when((i == 1) & (j + 1 < pl.num_programs(0)))
    def _():
        window_copy(j + 1).start()

    o_ref[...] = _dot_nt(a_ref[...], wb_ref[...]).astype(o_ref.dtype)


def in_proj_main(h, w_in_t, layer, tm=TM, tn=1024):
    m, k = h.shape
    assert all(s % tn == 0 and d % F32_SUBLANES == 0 for s, d in MAIN_SEGMENTS)
    assert m // tm >= 2
    return pl.pallas_call(
        functools.partial(_in_proj_kernel, layer=layer, tn=tn),
        out_shape=jax.ShapeDtypeStruct((m, N_MAIN), BF16),
        grid=(N_MAIN // tn, m // tm),
        in_specs=[pl.BlockSpec((tm, k), lambda j, i: (i, 0)),
                  pl.BlockSpec(memory_space=pl.ANY)],
        out_specs=pl.BlockSpec((tm, tn), lambda j, i: (i, j)),
        scratch_shapes=[pltpu.VMEM((tn, k), F32), pltpu.VMEM((tn, k), BF16), pltpu.SemaphoreType.DMA(())],
        compiler_params=_cparams(("arbitrary", "arbitrary")),
        name="in_proj_main",
    )(h, w_in_t)


def _rmsnorm_small_kernel(x_ref, w_ref, w0_ref, w1_ref, w2_ref, o_ref, sm_ref):
    x = x_ref[...]
    ms = jnp.mean(x * x, axis=-1, keepdims=True)
    h = (x * lax.rsqrt(ms + NORM_EPS) * w_ref[...]).astype(o_ref.dtype)
    o_ref[...] = h
    ws = jnp.concatenate([w0_ref[...], w1_ref[...], w2_ref[...]], axis=0).astype(BF16)
    sm_ref[...] = _dot_nt(h, ws)


def rmsnorm_small(x, w, w_in_t, layer, tm=512):
    m, d = x.shape

    def w_spec(blk):
        return pl.BlockSpec((None, LANES, d), lambda i: (layer, blk, 0))

    return pl.pallas_call(
        _rmsnorm_small_kernel,
        out_shape=(jax.ShapeDtypeStruct((m, d), BF16), jax.ShapeDtypeStruct((m, N_SMALL), F32)),
        grid=(m // tm,),
        in_specs=[pl.BlockSpec((tm, d), lambda i: (i, 0)),
                  pl.BlockSpec((1, d), lambda i: (0, 0))] + [w_spec(b) for b in SMALL_BLOCKS],
        out_specs=(pl.BlockSpec((tm, d), lambda i: (i, 0)), pl.BlockSpec((tm, N_SMALL), lambda i: (i, 0))),
        compiler_params=_cparams(("parallel",)),
        name="rmsnorm_small",
    )(x, w.reshape(1, d), w_in_t, w_in_t, w_in_t)


def _ssd_kernel(z_ref, xs_ref, b_ref, c_ref, dt_ref, cw_ref, cb_ref, dtb_ref, alog_ref, dskip_ref,
                nw_ref, ex_ref, ecol_ref, o_ref, ext_ref, state_ref):
    t = z_ref.shape[0]
    n = SSM_STATE
    gw = SSM_GW
    g = SSM_GROUPS
    chunk = pl.program_id(1)

    @pl.when(chunk == 0)
    def _():
        ext_ref[0:8, :] = jnp.zeros((8, SSM_XBC), F32)
        state_ref[...] = jnp.zeros_like(state_ref)

    @pl.when(chunk > 0)
    def _():
        ext_ref[0:8, :] = ext_ref[t:t + 8, :]

    ext_ref[8:8 + t, 0:SSM_INNER] = xs_ref[...].astype(F32)
    ext_ref[8:8 + t, SSM_INNER:SSM_INNER + g * n] = b_ref[...].astype(F32)
    ext_ref[8:8 + t, SSM_INNER + g * n:SSM_XBC] = c_ref[...].astype(F32)

    def conv(lo, hi):
        acc = cb_ref[:, lo:hi]
        for j in range(SSM_CONV):
            off = 8 - (SSM_CONV - 1) + j
            acc = acc + cw_ref[j:j + 1, lo:hi] * ext_ref[off:off + t, lo:hi]
        return _silu(acc)

    dt = _softplus(dt_ref[...] + dtb_ref[...])
    a = -jnp.exp(alog_ref[...])
    causal = _lower_tri(t)
    tri = jnp.where(causal, 1.0, 0.0).astype(BF16)
    a_cs = _dot_sel_l(tri, dt * a)
    a_last = a_cs[t - 1:t, :]
    ex = ex_ref[...]
    dt_x = _dot_sel_r(dt, ex)
    ea_x = _dot_sel_r(jnp.exp(a_cs), ex)
    eb_x = _dot_sel_r(jnp.exp(a_last - a_cs), ex)
    a_col = _dot_sel_r(a_cs, ecol_ref[...])
    a_row = a_cs.T
    lane = lax.broadcasted_iota(jnp.int32, (t, 2 * SSM_HEAD_DIM), 1)
    z = z_ref[...].astype(F32)

    outs = []
    for gi in range(g):
        cols = slice(gi * gw, (gi + 1) * gw)
        xs = conv(gi * gw, (gi + 1) * gw)
        bm = conv(SSM_INNER + gi * n, SSM_INNER + (gi + 1) * n)
        cm = conv(SSM_INNER + (g + gi) * n, SSM_INNER + (g + gi + 1) * n)
        xdt = xs * dt_x[:, cols]
        xdt_b = xdt.astype(BF16)
        cm_b = cm.astype(BF16)
        cb = _dot_nt(cm_b, bm.astype(BF16))
        pieces = []
        for p in range(SSM_HPG // 2):
            x_pair = xdt_b[:, p * 128:(p + 1) * 128]
            res = []
            for i in (2 * p, 2 * p + 1):
                hd = gi * SSM_HPG + i
                seg = a_col[:, hd * 128:(hd + 1) * 128] - a_row[hd:hd + 1, :]
                decay = jnp.exp(jnp.where(causal, seg, -jnp.inf))
                res.append(_dot((cb * decay).astype(BF16), x_pair))
            pieces.append(jnp.where(lane < SSM_HEAD_DIM, res[0], res[1]))
        y_diag = jnp.concatenate(pieces, axis=1)

        state = state_ref[gi]
        y_off = ea_x[:, cols] * _dot(cm_b, state.astype(BF16))
        state_ref[gi] = (state * ea_x[t - 1:t, cols]
                         + _dot(bm.T.astype(BF16), (eb_x[:, cols] * xdt).astype(BF16)))

        y = y_diag + y_off + xs * dskip_ref[:, cols]
        y = y * _silu(z[:, cols])
        ms = jnp.mean(y * y, axis=-1, keepdims=True)
        outs.append(y * lax.rsqrt(ms + NORM_EPS) * nw_ref[:, cols])
    o_ref[...] = jnp.concatenate(outs, axis=1).astype(o_ref.dtype)


def ssd_mixer(main, small, conv_w, conv_b, dt_bias, a_log, d_skip, norm_w, bsz, seq):
    t = SSD_CHUNK
    nc = seq // t
    gn = SSM_GROUPS * SSM_STATE

    def row(b, c):
        return b * nc + c

    heads = jnp.arange(LANES)[:, None]
    e_x = (heads == (jnp.arange(SSM_INNER)[None, :] // SSM_HEAD_DIM)).astype(BF16)
    e_col = (heads == (jnp.arange(SSM_HEADS * LANES)[None, :] // LANES)).astype(BF16)

    def full(shape):
        return pl.BlockSpec(shape, lambda b, c: (0,) * len(shape))

    in_specs = [
        pl.BlockSpec((t, SSM_INNER), lambda b, c: (row(b, c), OFF_SSM_Z // SSM_INNER)),
        pl.BlockSpec((t, SSM_INNER), lambda b, c: (row(b, c), OFF_SSM_X // SSM_INNER)),
        pl.BlockSpec((t, gn), lambda b, c: (row(b, c), OFF_SSM_B // gn)),
        pl.BlockSpec((t, gn), lambda b, c: (row(b, c), OFF_SSM_C // gn)),
        pl.BlockSpec((t, LANES), lambda b, c: (row(b, c), 0)),
        full((SSM_CONV, SSM_XBC)), full((1, SSM_XBC)), full((1, LANES)), full((1, LANES)),
        full((1, SSM_INNER)), full((1, SSM_INNER)), full((LANES, SSM_INNER)), full((LANES, SSM_HEADS * LANES)),
    ]
    return pl.pallas_call(
        _ssd_kernel,
        out_shape=jax.ShapeDtypeStruct((bsz * seq, SSM_INNER), BF16),
        grid=(bsz, nc),
        in_specs=in_specs,
        out_specs=pl.BlockSpec((t, SSM_INNER), lambda b, c: (row(b, c), 0)),
        scratch_shapes=[pltpu.VMEM((t + 8, SSM_XBC), F32),
                        pltpu.VMEM((SSM_GROUPS, SSM_STATE, SSM_GW), F32)],
        compiler_params=_cparams(("parallel", "arbitrary")),
        name="ssd_mixer",
    )(main, main, main, main, small, conv_w, conv_b.reshape(1, SSM_XBC),
      _at_lanes(dt_bias, LANE_SSM_DT), _at_lanes(a_log, LANE_SSM_DT),
      jnp.repeat(d_skip, SSM_HEAD_DIM).reshape(1, SSM_INNER), norm_w.reshape(1, SSM_INNER), e_x, e_col)


def _fox_prep_kernel(f_ref, fb_ref, o_ref, carry_ref):
    t = f_ref.shape[0]

    @pl.when(pl.program_id(1) == 0)
    def _():
        carry_ref[...] = jnp.zeros_like(carry_ref)

    log_f = -_softplus(-(f_ref[...] + fb_ref[...]))
    tri = jnp.where(_lower_tri(t), 1.0, 0.0).astype(BF16)
    cs = _dot_sel_l(tri, log_f) + carry_ref[0:1, :]
    carry_ref[...] = jnp.broadcast_to(cs[t - 1:t, :], carry_ref.shape)
    o_ref[...] = cs.T[LANE_FOX_F:LANE_FOX_F + FOX_HEADS, :]


def fox_cumulative_log_f(small, f_bias, bsz, seq, t=512):
    nc = seq // t
    return pl.pallas_call(
        _fox_prep_kernel,
        out_shape=jax.ShapeDtypeStruct((bsz, FOX_HEADS, seq), F32),
        grid=(bsz, nc),
        in_specs=[pl.BlockSpec((t, LANES), lambda b, c: (b * nc + c, 1)),
                  pl.BlockSpec((1, LANES), lambda b, c: (0, 0))],
        out_specs=pl.BlockSpec((None, FOX_HEADS, t), lambda b, c: (b, 0, c)),
        scratch_shapes=[pltpu.VMEM((8, LANES), F32)],
        compiler_params=_cparams(("parallel", "arbitrary")),
        name="fox_cum_log_f",
    )(small, _at_lanes(f_bias, LANE_FOX_F))


def _fox_kernel(q_ref, k_ref, v_ref, cf_ref, o_ref):
    tq = q_ref.shape[0]
    tk = FOX_TK
    d = FOX_HEAD_DIM
    qi = pl.program_id(2)
    q = q_ref[...]
    scale2 = (d ** -0.5) * LOG2_E

    def step(j, carry, masked):
        m, l, acc = carry
        start = pl.multiple_of(j * tk, tk)
        k = k_ref[pl.ds(start, tk), :]
        v = v_ref[pl.ds(start, tk), :]
        s = _dot_nt(q, k) * scale2 - cf_ref[pl.ds(j, 1), :] * LOG2_E
        if masked:
            s = jnp.where(_lower_tri(tq), s, -jnp.inf)
        m_new = jnp.maximum(m, jnp.max(s, axis=-1, keepdims=True))
        alpha = jnp.exp2(m - m_new)
        p = jnp.exp2(s - m_new)
        l = alpha * l + jnp.sum(p, axis=-1, keepdims=True)
        acc = alpha * acc + _dot(p.astype(BF16), v)
        return m_new, l, acc

    m0 = jnp.full((tq, 1), -jnp.inf, F32)
    l0 = jnp.zeros((tq, 1), F32)
    acc0 = jnp.zeros((tq, d), F32)
    carry = lax.fori_loop(0, qi, lambda j, c: step(j, c, False), (m0, l0, acc0))
    _, l, acc = step(qi, carry, True)
    o_ref[...] = (acc / l).astype(o_ref.dtype)


def fox_mixer(main, cum_f, bsz, seq):
    tq = FOX_TQ
    assert FOX_TQ == FOX_TK
    nq = seq // tq
    d = FOX_HEAD_DIM
    cf = cum_f.reshape(bsz * FOX_HEADS, seq // FOX_TK, FOX_TK)
    return pl.pallas_call(
        _fox_kernel,
        out_shape=jax.ShapeDtypeStruct((bsz * seq, FOX_WIDTH), BF16),
        grid=(bsz, FOX_HEADS, nq),
        in_specs=[pl.BlockSpec((tq, d), lambda b, h, i: (b * nq + i, OFF_FOX_Q // d + h)),
                  pl.BlockSpec((seq, d), lambda b, h, i: (b, OFF_FOX_K // d + h)),
                  pl.BlockSpec((seq, d), lambda b, h, i: (b, OFF_FOX_V // d + h)),
                  pl.BlockSpec((None, seq // FOX_TK, FOX_TK), lambda b, h, i: (b * FOX_HEADS + h, 0, 0))],
        out_specs=pl.BlockSpec((tq, d), lambda b, h, i: (b * nq + i, h)),
        compiler_params=_cparams(("parallel", "parallel", "arbitrary")),
        name="fox_attention",
    )(main, main, main, cf)


INV_BASE = 32


def _unit_lower_inverse(a, blk_xor, eye):
    t = a[0].shape[0]
    n = range(len(a))
    pw = [jnp.where(blk_xor == 0, ai, 0.0) for ai in a]
    x = [eye - p for p in pw]
    for _ in range(INV_BASE.bit_length() - 2):
        pw = [_dot2(p, p) for p in pw]
        x = [x[i] + _dot2(x[i], pw[i]) for i in n]
    level = 1
    while INV_BASE * level < t:
        sel = (blk_xor >= level) & (blk_xor < 2 * level)
        xb = [xi.astype(BF16) for xi in x]
        xo = [_dot(xb[i], jnp.where(sel, a[i], 0.0).astype(BF16)).astype(BF16) for i in n]
        x = [x[i] - _dot(xo[i], xb[i]) for i in n]
        level *= 2
    return x


def _gdn_kernel(q_ref, k_ref, v_ref, z_ref, sm_ref, cwq_ref, cwk_ref, cwv_ref, dtb_ref, alog_ref,
                nw_ref, rep_ref, o_ref, ext_ref, state_ref):
    t = q_ref.shape[0]
    d = GDN_HEAD_DIM
    hb = GDN_HB
    w = hb * d
    chunk = pl.program_id(2)

    @pl.when(chunk == 0)
    def _():
        ext_ref[0:8, :] = jnp.zeros((8, 3 * w), F32)
        state_ref[...] = jnp.zeros_like(state_ref)

    @pl.when(chunk > 0)
    def _():
        ext_ref[0:8, :] = ext_ref[t:t + 8, :]

    ext_ref[8:8 + t, 0:w] = q_ref[...].astype(F32)
    ext_ref[8:8 + t, w:2 * w] = k_ref[...].astype(F32)
    ext_ref[8:8 + t, 2 * w:3 * w] = v_ref[...].astype(F32)

    def conv(lo, hi, w_ref):
        acc = None
        for j in range(GDN_CONV):
            off = 8 - (GDN_CONV - 1) + j
            term = w_ref[j:j + 1, :] * ext_ref[off:off + t, lo:hi]
            acc = term if acc is None else acc + term
        return _silu(acc)

    qc = conv(0, w, cwq_ref)
    kc = conv(w, 2 * w, cwk_ref)
    vc = conv(2 * w, 3 * w, cwv_ref)

    sm = sm_ref[...]
    lane = lax.broadcasted_iota(jnp.int32, sm.shape, 1)
    g_all = -jnp.exp(alog_ref[...]) * _softplus(sm + dtb_ref[...])
    mix = jnp.where(lane < LANE_GDN_B, g_all, _sigmoid(sm))
    rep = _dot_sel_r(mix, rep_ref[...])

    lower = _lower_tri(t)
    strict = _lower_tri(t, strict=True)
    tri = jnp.where(lower, 1.0, 0.0).astype(BF16)
    eye = jnp.where(lower, 1.0, 0.0) - jnp.where(strict, 1.0, 0.0)
    rows = lax.broadcasted_iota(jnp.int32, (t, t), 0)
    cols = lax.broadcasted_iota(jnp.int32, (t, t), 1)
    shift = INV_BASE.bit_length() - 1
    blk_xor = jnp.right_shift(rows, shift) ^ jnp.right_shift(cols, shift)
    z = z_ref[...].astype(F32)
    gc_all = _dot_sel_l(tri, rep[:, 0:w])

    hs = range(hb)
    sls = [slice(h * d, (h + 1) * d) for h in hs]

    def l2n(x):
        return x * lax.rsqrt(jnp.sum(x * x, axis=-1, keepdims=True) + 1e-6)

    q = [l2n(qc[:, sl]) * (d ** -0.5) for sl in sls]
    k = [l2n(kc[:, sl]) for sl in sls]
    beta = [rep[:, (hb + h) * d:(hb + h + 1) * d] for h in hs]
    gc = [gc_all[:, sl] for sl in sls]
    decay = [jnp.exp(jnp.where(lower, g - g.T, -jnp.inf)) for g in gc]
    eg = [jnp.exp(g) for g in gc]
    kb = [k[h] * beta[h] for h in hs]
    k_b = [x.astype(BF16) for x in k]
    a_mat = [jnp.where(strict, _dot_nt(kb[h].astype(BF16), k_b[h]) * decay[h], 0.0) for h in hs]
    attn = [(_dot_nt(q[h].astype(BF16), k_b[h]) * decay[h]).astype(BF16) for h in hs]
    rhs = [jnp.concatenate([vc[:, sls[h]] * beta[h], kb[h] * eg[h]], axis=1).astype(BF16) for h in hs]
    t_inv = _unit_lower_inverse(a_mat, blk_xor, eye)
    sol = [_dot(t_inv[h].astype(BF16), rhs[h]) for h in hs]
    s_old = [state_ref[h] for h in hs]
    s_b = [s.astype(BF16) for s in s_old]
    v_new = [(sol[h][:, :d] - _dot(sol[h][:, d:].astype(BF16), s_b[h])).astype(BF16) for h in hs]
    o = [_dot((q[h] * eg[h]).astype(BF16), s_b[h]) + _dot(attn[h], v_new[h]) for h in hs]
    g_last = [g[t - 1:t, :] for g in gc]
    k_dec = [(k[h] * jnp.exp(g_last[h] - gc[h])).T.astype(BF16) for h in hs]
    for h in hs:
        state_ref[h] = s_old[h] * jnp.exp(g_last[h]) + _dot(k_dec[h], v_new[h])
    nw = nw_ref[...]
    outs = [o[h] * lax.rsqrt(jnp.mean(o[h] * o[h], axis=-1, keepdims=True) + NORM_EPS) * nw
            * _silu(z[:, sls[h]]) for h in hs]
    o_ref[...] = jnp.concatenate(outs, axis=1).astype(o_ref.dtype)


def gdn_mixer(main, small, conv_w, dt_bias, a_log, norm_w, bsz, seq):
    t = GDN_CHUNK
    nc = seq // t
    hb = GDN_HB
    d = GDN_HEAD_DIM
    w = hb * d
    ng = GDN_HEADS // hb

    def row(b, hg, c):
        return b * nc + c

    src = jnp.arange(LANES)[None, :, None]
    col_head = (jnp.arange(2 * w) // d)[None, None, :]
    hg_idx = jnp.arange(ng)[:, None, None]
    want = jnp.where(col_head < hb, LANE_GDN_A + hg_idx * hb + col_head,
                     LANE_GDN_B + hg_idx * hb + (col_head - hb))
    rep = (src == want).astype(BF16)

    in_specs = [
        pl.BlockSpec((t, w), lambda b, hg, c: (row(b, hg, c), OFF_GDN_Q // w + hg)),
        pl.BlockSpec((t, w), lambda b, hg, c: (row(b, hg, c), OFF_GDN_K // w + hg)),
        pl.BlockSpec((t, w), lambda b, hg, c: (row(b, hg, c), OFF_GDN_V // w + hg)),
        pl.BlockSpec((t, w), lambda b, hg, c: (row(b, hg, c), OFF_GDN_Z // w + hg)),
        pl.BlockSpec((t, LANES), lambda b, hg, c: (row(b, hg, c), 2)),
        pl.BlockSpec((GDN_CONV, w), lambda b, hg, c: (0, hg)),
        pl.BlockSpec((GDN_CONV, w), lambda b, hg, c: (0, ng + hg)),
        pl.BlockSpec((GDN_CONV, w), lambda b, hg, c: (0, 2 * ng + hg)),
        pl.BlockSpec((1, LANES), lambda b, hg, c: (0, 0)),
        pl.BlockSpec((1, LANES), lambda b, hg, c: (0, 0)),
        pl.BlockSpec((1, d), lambda b, hg, c: (0, 0)),
        pl.BlockSpec((None, LANES, 2 * w), lambda b, hg, c: (hg, 0, 0)),
    ]
    return pl.pallas_call(
        _gdn_kernel,
        out_shape=jax.ShapeDtypeStruct((bsz * seq, GDN_WIDTH), BF16),
        grid=(bsz, ng, nc),
        in_specs=in_specs,
        out_specs=pl.BlockSpec((t, w), lambda b, hg, c: (row(b, hg, c), hg)),
        scratch_shapes=[pltpu.VMEM((t + 8, 3 * w), F32), pltpu.VMEM((hb, d, d), F32)],
        compiler_params=_cparams(("parallel", "parallel", "arbitrary")),
        name="gdn_mixer",
    )(main, main, main, main, small, conv_w, conv_w, conv_w,
      _at_lanes(dt_bias, LANE_GDN_A), _at_lanes(a_log, LANE_GDN_A), norm_w.reshape(1, d), rep)


def _merge_kernel(ys_ref, yf_ref, yg_ref, ws_ref, wf_ref, wg_ref, gs_ref, gf_ref, gg_ref, o_ref,
                  wsb_ref, wfb_ref, wgb_ref):
    @pl.when(pl.program_id(1) == 0)
    def _():
        wsb_ref[...] = ws_ref[...].astype(BF16)
        wfb_ref[...] = wf_ref[...].astype(BF16)
        wgb_ref[...] = wg_ref[...].astype(BF16)

    acc = _sigmoid(gs_ref[...].astype(F32)) * _dot(ys_ref[...], wsb_ref[...])
    acc = acc + _sigmoid(gf_ref[...].astype(F32)) * _dot(yf_ref[...], wfb_ref[...])
    acc = acc + _sigmoid(gg_ref[...].astype(F32)) * _dot(yg_ref[...], wgb_ref[...])
    o_ref[...] = acc.astype(o_ref.dtype)


def merge_branches(y_ssm, y_fox, y_gdn, w_ssm, w_fox, w_gdn, main, layer, tm=TM, tn=512):
    m = y_ssm.shape[0]
    n = D_MODEL
    gate_blk = OFF_GATE // tn
    per_gate = D_MODEL // tn

    def lhs(width):
        return pl.BlockSpec((tm, width), lambda j, i: (i, 0))

    def rhs(width):
        return pl.BlockSpec((None, width, tn), lambda j, i: (layer, 0, j))

    def gate(idx):
        return pl.BlockSpec((tm, tn), lambda j, i: (i, gate_blk + idx * per_gate + j))

    return pl.pallas_call(
        _merge_kernel,
        out_shape=jax.ShapeDtypeStruct((m, n), BF16),
        grid=(n // tn, m // tm),
        in_specs=[lhs(SSM_INNER), lhs(FOX_WIDTH), lhs(GDN_WIDTH),
                  rhs(SSM_INNER), rhs(FOX_WIDTH), rhs(GDN_WIDTH),
                  gate(0), gate(1), gate(2)],
        out_specs=pl.BlockSpec((tm, tn), lambda j, i: (i, j)),
        scratch_shapes=[pltpu.VMEM((SSM_INNER, tn), BF16), pltpu.VMEM((FOX_WIDTH, tn), BF16),
                        pltpu.VMEM((GDN_WIDTH, tn), BF16)],
        compiler_params=_cparams(("arbitrary", "arbitrary")),
        name="merge_branches",
    )(y_ssm, y_fox, y_gdn, w_ssm, w_fox, w_gdn, main, main, main)


def _out_proj_kernel(a_ref, w_ref, x_ref, o_ref, wb_ref):
    @pl.when(pl.program_id(1) == 0)
    def _():
        wb_ref[...] = w_ref[...].astype(BF16)

    o_ref[...] = x_ref[...] + _dot(a_ref[...], wb_ref[...])


def out_proj(a, w, x, layer, tm=TM, tn=512):
    m, k = a.shape
    n = w.shape[2]
    return pl.pallas_call(
        _out_proj_kernel,
        out_shape=jax.ShapeDtypeStruct((m, n), F32),
        grid=(n // tn, m // tm),
        in_specs=[pl.BlockSpec((tm, k), lambda j, i: (i, 0)),
                  pl.BlockSpec((None, k, tn), lambda j, i: (layer, 0, j)),
                  pl.BlockSpec((tm, tn), lambda j, i: (i, j))],
        out_specs=pl.BlockSpec((tm, tn), lambda j, i: (i, j)),
        scratch_shapes=[pltpu.VMEM((k, tn), BF16)],
        compiler_params=_cparams(("arbitrary", "arbitrary")),
        name="out_proj",
    )(a, w, x)


def _residual_matmul_kernel(a_ref, w_ref, x_ref, o_ref):
    kk = pl.program_id(2)

    @pl.when(kk == 0)
    def _():
        o_ref[...] = x_ref[...] + _dot(a_ref[...], w_ref[...])

    @pl.when(kk > 0)
    def _():
        o_ref[...] += _dot(a_ref[...], w_ref[...])


def ffn_down(a, w, x, layer, tm=TM, tn=512, tk=D_FF // 2):
    m, k = a.shape
    n = w.shape[2]
    return pl.pallas_call(
        _residual_matmul_kernel,
        out_shape=jax.ShapeDtypeStruct((m, n), F32),
        grid=(m // tm, n // tn, k // tk),
        in_specs=[pl.BlockSpec((tm, tk), lambda i, j, kk: (i, kk)),
                  pl.BlockSpec((None, tk, tn), lambda i, j, kk: (layer, kk, j)),
                  pl.BlockSpec((tm, tn), lambda i, j, kk: (i, j))],
        out_specs=pl.BlockSpec((tm, tn), lambda i, j, kk: (i, j)),
        compiler_params=_cparams(("parallel", "parallel", "arbitrary")),
        name="ffn_down",
    )(a, w, x)


def _ffn_up_kernel(h_ref, wg_ref, wv_ref, cwg_ref, cwv_ref, cbg_ref, cbv_ref, o_ref,
                   ug_ref, uv_ref, tg_ref, tv_ref, *, tiles_per_seq):
    tm = h_ref.shape[0]
    i = pl.program_id(0)
    j = pl.program_id(1)

    @pl.when(i == 0)
    def _():
        tg_ref[j] = jnp.zeros(tg_ref.shape[1:], F32)
        tv_ref[j] = jnp.zeros(tv_ref.shape[1:], F32)

    first = (i % tiles_per_seq) == 0
    h = h_ref[...]

    def conv_product(u_ref, t_ref, w_ref, cw_ref, cb_ref):
        u_ref[0:HALO, :] = jnp.where(first, 0.0, t_ref[j])
        u_ref[HALO:HALO + tm, :] = _dot(h, w_ref[...])
        t_ref[j] = u_ref[tm:tm + HALO, :]
        win = u_ref[HALO - 8:HALO + tm, :]
        acc = cb_ref[...] + cw_ref[FFN_CONV - 1:FFN_CONV, :] * win[8:, :]
        for back in range(1, FFN_CONV):
            tap = FFN_CONV - 1 - back
            acc = acc + cw_ref[tap:tap + 1, :] * pltpu.roll(win, back, axis=0)[8:, :]
        return acc

    gate = conv_product(ug_ref, tg_ref, wg_ref, cwg_ref, cbg_ref)
    val = conv_product(uv_ref, tv_ref, wv_ref, cwv_ref, cbv_ref)
    o_ref[...] = (_silu(gate) * val).astype(o_ref.dtype)


def ffn_up(h, w_gv, cw_gate, cw_val, cb_gate, cb_val, layer, seq, tm=TM, tn=512):
    m, k = h.shape
    nb = D_FF_PAD // tn
    kern = functools.partial(_ffn_up_kernel, tiles_per_seq=seq // tm)

    def w_spec(half):
        return pl.BlockSpec((None, None, k, tn), lambda i, j: (layer, half, 0, j))

    def col_spec(rows):
        return pl.BlockSpec((rows, tn), lambda i, j: (0, j))

    return pl.pallas_call(
        kern,
        out_shape=jax.ShapeDtypeStruct((m, D_FF), BF16),
        grid=(m // tm, nb),
        in_specs=[pl.BlockSpec((tm, k), lambda i, j: (i, 0)),
                  w_spec(0), w_spec(1), col_spec(FFN_CONV), col_spec(FFN_CONV), col_spec(1), col_spec(1)],
        out_specs=pl.BlockSpec((tm, tn), lambda i, j: (i, j)),
        scratch_shapes=[pltpu.VMEM((HALO + tm, tn), F32)] * 2
                       + [pltpu.VMEM((nb, HALO, tn), F32)] * 2,
        compiler_params=_cparams(("arbitrary", "arbitrary")),
        name="ffn_up",
    )(h, w_gv, w_gv, cw_gate, cw_val, cb_gate, cb_val)


def _w_up_prep_kernel(w_ref, o_ref, *, nb):
    o_ref[...] = jnp.where(pl.program_id(2) < nb, w_ref[...], 0.0).astype(BF16)


def w_up_prep(w_up, tn=256):
    depth, k, _ = w_up.shape
    nb = D_FF // tn
    nbp = D_FF_PAD // tn
    return pl.pallas_call(
        functools.partial(_w_up_prep_kernel, nb=nb),
        out_shape=jax.ShapeDtypeStruct((depth, 2, k, D_FF_PAD), BF16),
        grid=(depth, 2, nbp),
        in_specs=[pl.BlockSpec((None, k, tn), lambda l, half, j: (l, 0, jnp.minimum(half * nb + j, 2 * nb - 1)))],
        out_specs=pl.BlockSpec((None, None, k, tn), lambda l, half, j: (l, half, 0, j)),
        compiler_params=_cparams(("parallel", "parallel", "parallel")),
        name="w_up_prep",
    )(w_up)


def _pad_ffn_conv(conv_w, conv_b):
    pad = D_FF_PAD - D_FF

    def halves(a):
        return tuple(jnp.pad(p, ((0, 0), (0, pad))) for p in (a[:, :D_FF], a[:, D_FF:]))

    return halves(conv_w), halves(conv_b.reshape(1, -1))


def kernel(x, norm_mix, w_in, ssm_conv_w, ssm_conv_b, ssm_dt_bias, ssm_a_log, ssm_d, ssm_norm,
           fox_f_bias, gdn_conv_w, gdn_dt_bias, gdn_a_log, gdn_norm, w_br_ssm, w_br_fox, w_br_gdn,
           w_out, norm_ffn, w_up, ffn_conv_w, ffn_conv_b, w_down, norm_final):
    bsz, seq, d = x.shape
    depth = w_in.shape[0]
    m = bsz * seq
    xf = x.reshape(m, d)
    w_down_b = w_down.astype(BF16)
    w_in_t = jnp.swapaxes(w_in, 1, 2)
    w_gv = w_up_prep(w_up)
    for layer in range(depth):
        h, small = rmsnorm_small(xf, norm_mix[layer], w_in_t, layer)
        main = in_proj_main(h, w_in_t, layer)
        y_ssm = ssd_mixer(main, small, ssm_conv_w[layer], ssm_conv_b[layer], ssm_dt_bias[layer],
                          ssm_a_log[layer], ssm_d[layer], ssm_norm[layer], bsz, seq)
        cum_f = fox_cumulative_log_f(small, fox_f_bias[layer], bsz, seq)
        y_fox = fox_mixer(main, cum_f, bsz, seq)
        y_gdn = gdn_mixer(main, small, gdn_conv_w[layer], gdn_dt_bias[layer], gdn_a_log[layer],
                          gdn_norm[layer], bsz, seq)
        merged = merge_branches(y_ssm, y_fox, y_gdn, w_br_ssm, w_br_fox, w_br_gdn, main, layer)
        xf = out_proj(merged, w_out, xf, layer)
        h2 = rmsnorm(xf, norm_ffn[layer], BF16)
        cw_p, cb_p = _pad_ffn_conv(ffn_conv_w[layer], ffn_conv_b[layer])
        act = ffn_up(h2, w_gv, *cw_p, *cb_p, layer, seq)
        xf = ffn_down(act, w_down_b, xf, layer)
    return rmsnorm(xf, norm_final, F32).reshape(bsz, seq, d)
```

```python
import functools

import jax
import jax.numpy as jnp
from jax import lax
from jax.experimental import pallas as pl
from jax.experimental.pallas import tpu as pltpu

F32 = jnp.float32
BF16 = jnp.bfloat16

D_MODEL = 4096
SSM_INNER = 2048
SSM_HEAD_DIM = 64
SSM_HEADS = 32
SSM_GROUPS = 4
SSM_STATE = 128
SSM_CONV = 4
SSM_XBC = SSM_INNER + 2 * SSM_GROUPS * SSM_STATE
SSM_HPG = SSM_HEADS // SSM_GROUPS
SSM_GW = SSM_HPG * SSM_HEAD_DIM
FOX_HEADS = 8
FOX_HEAD_DIM = 128
FOX_WIDTH = FOX_HEADS * FOX_HEAD_DIM
GDN_HEADS = 8
GDN_HEAD_DIM = 128
GDN_WIDTH = GDN_HEADS * GDN_HEAD_DIM
GDN_CONV = 4
D_FF = 11008
D_FF_PAD = 11264
FFN_CONV = 3
NORM_EPS = 1e-6
LOG2_E = 1.4426950408889634
N_MAIN = 24576
LANES = 128
HALO = 16

OFF_SSM_Z = 0
OFF_SSM_X = 2048
OFF_SSM_B = 4096
OFF_SSM_C = 4608
OFF_FOX_Q = 5120
OFF_FOX_K = 6144
OFF_FOX_V = 7168
OFF_GDN_Q = 8192
OFF_GDN_K = 9216
OFF_GDN_V = 10240
OFF_GDN_Z = 11264
OFF_GATE = 12288
MAIN_SEGMENTS = ((0, 0), (OFF_FOX_Q, SSM_HEADS), (OFF_GDN_Q, SSM_HEADS + FOX_HEADS),
                 (OFF_GDN_Z, SSM_HEADS + FOX_HEADS + 2 * GDN_HEADS))
SMALL_BLOCKS = (OFF_FOX_Q // LANES, (OFF_GDN_Q + SSM_HEADS) // LANES,
                (OFF_GDN_Z + SSM_HEADS + FOX_HEADS) // LANES)
LANE_SSM_DT = 0
LANE_FOX_F = SSM_HEADS
LANE_GDN_A = SSM_HEADS + FOX_HEADS
LANE_GDN_B = SSM_HEADS + FOX_HEADS + GDN_HEADS
N_SMALL = len(SMALL_BLOCKS) * LANES

SSD_CHUNK = 128
GDN_CHUNK = 128
GDN_HB = 8
FOX_TQ = 512
FOX_TK = 512
TM = 1024
VMEM_LIMIT = 56 * 1024 * 1024


def _cparams(sem):
    return pltpu.CompilerParams(dimension_semantics=sem, vmem_limit_bytes=VMEM_LIMIT)


def _sigmoid(x):
    return 1.0 / (1.0 + jnp.exp(-x))


def _silu(x):
    return x * _sigmoid(x)


def _softplus(x):
    return jnp.maximum(x, 0.0) + jnp.log(1.0 + jnp.exp(-jnp.abs(x)))


def _split3(x):
    hi = x.astype(BF16)
    r1 = x - hi.astype(F32)
    mid = r1.astype(BF16)
    lo = (r1 - mid.astype(F32)).astype(BF16)
    return hi, mid, lo


def _dot(a, b):
    return jnp.dot(a, b, preferred_element_type=F32)


def _dot_nt(a, b):
    return lax.dot_general(a, b, (((1,), (1,)), ((), ())), preferred_element_type=F32)


def _dot_sel_r(x, sel):
    hi, mid, lo = _split3(x)
    return (_dot(lo, sel) + _dot(mid, sel)) + _dot(hi, sel)


def _dot_sel_l(sel, x):
    hi, mid, lo = _split3(x)
    return (_dot(sel, lo) + _dot(sel, mid)) + _dot(sel, hi)


def _dot2(a, b):
    ah = a.astype(BF16)
    al = (a - ah.astype(F32)).astype(BF16)
    bh = b.astype(BF16)
    bl = (b - bh.astype(F32)).astype(BF16)
    return (_dot(al, bh) + _dot(ah, bl)) + _dot(ah, bh)


def _lower_tri(n, strict=False):
    r = lax.broadcasted_iota(jnp.int32, (n, n), 0)
    c = lax.broadcasted_iota(jnp.int32, (n, n), 1)
    return (r > c) if strict else (r >= c)


def _at_lanes(v, start):
    return jnp.pad(v, (start, LANES - start - v.shape[0])).reshape(1, LANES)


def _rmsnorm_kernel(x_ref, w_ref, o_ref):
    x = x_ref[...]
    ms = jnp.mean(x * x, axis=-1, keepdims=True)
    o_ref[...] = (x * lax.rsqrt(ms + NORM_EPS) * w_ref[...]).astype(o_ref.dtype)


def rmsnorm(x, w, out_dtype, tm=512):
    m, d = x.shape
    return pl.pallas_call(
        _rmsnorm_kernel,
        out_shape=jax.ShapeDtypeStruct((m, d), out_dtype),
        grid=(m // tm,),
        in_specs=[pl.BlockSpec((tm, d), lambda i: (i, 0)),
                  pl.BlockSpec((1, d), lambda i: (0, 0))],
        out_specs=pl.BlockSpec((tm, d), lambda i: (i, 0)),
        compiler_params=_cparams(("parallel",)),
        name="rmsnorm",
    )(x, w.reshape(1, d))


F32_SUBLANES = 8


def _in_proj_kernel(a_ref, w_hbm, o_ref, wf_ref, wb_ref, sem, *, layer, tn):
    j = pl.program_id(0)
    i = pl.program_id(1)

    def window_copy(blk):
        dropped = 0
        for start, d in MAIN_SEGMENTS[1:]:
            dropped = jnp.where(blk >= start // tn, d // F32_SUBLANES, dropped)
        row = pl.multiple_of((blk * (tn // F32_SUBLANES) + dropped) * F32_SUBLANES, F32_SUBLANES)
        return pltpu.make_async_copy(w_hbm.at[layer, pl.ds(row, tn), :], wf_ref, sem)

    @pl.when((j == 0) & (i == 0))
    def _():
        window_copy(j).start()

    @pl.when(i == 0)
    def _():
        window_copy(j).wait()
        wb_ref[...] = wf_ref[...].astype(BF16)

    @pl.when((i == 1) & (j + 1 < pl.num_programs(0)))
    def _():
        window_copy(j + 1).start()

    o_ref[...] = _dot_nt(a_ref[...], wb_ref[...]).astype(o_ref.dtype)


def in_proj_main(h, w_in_t, layer, tm=TM, tn=1024):
    m, k = h.shape
    assert all(s % tn == 0 and d % F32_SUBLANES == 0 for s, d in MAIN_SEGMENTS)
    assert m // tm >= 2
    return pl.pallas_call(
        functools.partial(_in_proj_kernel, layer=layer, tn=tn),
        out_shape=jax.ShapeDtypeStruct((m, N_MAIN), BF16),
        grid=(N_MAIN // tn, m // tm),
        in_specs=[pl.BlockSpec((tm, k), lambda j, i: (i, 0)),
                  pl.BlockSpec(memory_space=pl.ANY)],
        out_specs=pl.BlockSpec((tm, tn), lambda j, i: (i, j)),
        scratch_shapes=[pltpu.VMEM((tn, k), F32), pltpu.VMEM((tn, k), BF16), pltpu.SemaphoreType.DMA(())],
        compiler_params=_cparams(("arbitrary", "arbitrary")),
        name="in_proj_main",
    )(h, w_in_t)


def _rmsnorm_small_kernel(x_ref, w_ref, w0_ref, w1_ref, w2_ref, o_ref, sm_ref):
    x = x_ref[...]
    ms = jnp.mean(x * x, axis=-1, keepdims=True)
    h = (x * lax.rsqrt(ms + NORM_EPS) * w_ref[...]).astype(o_ref.dtype)
    o_ref[...] = h
    ws = jnp.concatenate([w0_ref[...], w1_ref[...], w2_ref[...]], axis=0).astype(BF16)
    sm_ref[...] = _dot_nt(h, ws)


def rmsnorm_small(x, w, w_in_t, layer, tm=512):
    m, d = x.shape

    def w_spec(blk):
        return pl.BlockSpec((None, LANES, d), lambda i: (layer, blk, 0))

    return pl.pallas_call(
        _rmsnorm_small_kernel,
        out_shape=(jax.ShapeDtypeStruct((m, d), BF16), jax.ShapeDtypeStruct((m, N_SMALL), F32)),
        grid=(m // tm,),
        in_specs=[pl.BlockSpec((tm, d), lambda i: (i, 0)),
                  pl.BlockSpec((1, d), lambda i: (0, 0))] + [w_spec(b) for b in SMALL_BLOCKS],
        out_specs=(pl.BlockSpec((tm, d), lambda i: (i, 0)), pl.BlockSpec((tm, N_SMALL), lambda i: (i, 0))),
        compiler_params=_cparams(("parallel",)),
        name="rmsnorm_small",
    )(x, w.reshape(1, d), w_in_t, w_in_t, w_in_t)


def _ssd_kernel(z_ref, xs_ref, b_ref, c_ref, dt_ref, cw_ref, cb_ref, dtb_ref, alog_ref, dskip_ref,
                nw_ref, ex_ref, ecol_ref, o_ref, ext_ref, state_ref):
    t = z_ref.shape[0]
    n = SSM_STATE
    gw = SSM_GW
    g = SSM_GROUPS
    chunk = pl.program_id(1)

    @pl.when(chunk == 0)
    def _():
        ext_ref[0:8, :] = jnp.zeros((8, SSM_XBC), F32)
        state_ref[...] = jnp.zeros_like(state_ref)

    @pl.when(chunk > 0)
    def _():
        ext_ref[0:8, :] = ext_ref[t:t + 8, :]

    ext_ref[8:8 + t, 0:SSM_INNER] = xs_ref[...].astype(F32)
    ext_ref[8:8 + t, SSM_INNER:SSM_INNER + g * n] = b_ref[...].astype(F32)
    ext_ref[8:8 + t, SSM_INNER + g * n:SSM_XBC] = c_ref[...].astype(F32)

    def conv(lo, hi):
        win = ext_ref[0:8 + t, lo:hi]
        acc = cb_ref[:, lo:hi] + cw_ref[SSM_CONV - 1:SSM_CONV, lo:hi] * win[8:, :]
        for back in range(1, SSM_CONV):
            j = SSM_CONV - 1 - back
            acc = acc + cw_ref[j:j + 1, lo:hi] * pltpu.roll(win, back, axis=0)[8:, :]
        return _silu(acc)

    dt = _softplus(dt_ref[...] + dtb_ref[...])
    a = -jnp.exp(alog_ref[...])
    causal = _lower_tri(t)
    tri = jnp.where(causal, 1.0, 0.0).astype(BF16)
    a_cs = _dot_sel_l(tri, dt * a)
    a_last = a_cs[t - 1:t, :]
    ex = ex_ref[...]
    dt_x = _dot_sel_r(dt, ex)
    ea_x = _dot_sel_r(jnp.exp(a_cs), ex)
    eb_x = _dot_sel_r(jnp.exp(a_last - a_cs), ex)
    a_col = _dot_sel_r(a_cs, ecol_ref[...])
    a_row = a_cs.T
    lane = lax.broadcasted_iota(jnp.int32, (t, 2 * SSM_HEAD_DIM), 1)
    z = z_ref[...].astype(F32)

    outs = []
    for gi in range(g):
        cols = slice(gi * gw, (gi + 1) * gw)
        xs = conv(gi * gw, (gi + 1) * gw)
        bm = conv(SSM_INNER + gi * n, SSM_INNER + (gi + 1) * n)
        cm = conv(SSM_INNER + (g + gi) * n, SSM_INNER + (g + gi + 1) * n)
        xdt = xs * dt_x[:, cols]
        xdt_b = xdt.astype(BF16)
        cm_b = cm.astype(BF16)
        cb = _dot_nt(cm_b, bm.astype(BF16))
        pieces = []
        for p in range(SSM_HPG // 2):
            x_pair = xdt_b[:, p * 128:(p + 1) * 128]
            res = []
            for i in (2 * p, 2 * p + 1):
                hd = gi * SSM_HPG + i
                seg = a_col[:, hd * 128:(hd + 1) * 128] - a_row[hd:hd + 1, :]
                decay = jnp.exp(jnp.where(causal, seg, -jnp.inf))
                res.append(_dot((cb * decay).astype(BF16), x_pair))
            pieces.append(jnp.where(lane < SSM_HEAD_DIM, res[0], res[1]))
        y_diag = jnp.concatenate(pieces, axis=1)

        state = state_ref[gi]
        y_off = ea_x[:, cols] * _dot(cm_b, state.astype(BF16))
        state_ref[gi] = (state * ea_x[t - 1:t, cols]
                         + _dot(bm.T.astype(BF16), (eb_x[:, cols] * xdt).astype(BF16)))

        y = y_diag + y_off + xs * dskip_ref[:, cols]
        y = y * _silu(z[:, cols])
        ms = jnp.mean(y * y, axis=-1, keepdims=True)
        outs.append(y * lax.rsqrt(ms + NORM_EPS) * nw_ref[:, cols])
    o_ref[...] = jnp.concatenate(outs, axis=1).astype(o_ref.dtype)


def ssd_mixer(main, small, conv_w, conv_b, dt_bias, a_log, d_skip, norm_w, bsz, seq):
    t = SSD_CHUNK
    nc = seq // t
    gn = SSM_GROUPS * SSM_STATE

    def row(b, c):
        return b * nc + c

    heads = jnp.arange(LANES)[:, None]
    e_x = (heads == (jnp.arange(SSM_INNER)[None, :] // SSM_HEAD_DIM)).astype(BF16)
    e_col = (heads == (jnp.arange(SSM_HEADS * LANES)[None, :] // LANES)).astype(BF16)

    def full(shape):
        return pl.BlockSpec(shape, lambda b, c: (0,) * len(shape))

    in_specs = [
        pl.BlockSpec((t, SSM_INNER), lambda b, c: (row(b, c), OFF_SSM_Z // SSM_INNER)),
        pl.BlockSpec((t, SSM_INNER), lambda b, c: (row(b, c), OFF_SSM_X // SSM_INNER)),
        pl.BlockSpec((t, gn), lambda b, c: (row(b, c), OFF_SSM_B // gn)),
        pl.BlockSpec((t, gn), lambda b, c: (row(b, c), OFF_SSM_C // gn)),
        pl.BlockSpec((t, LANES), lambda b, c: (row(b, c), 0)),
        full((SSM_CONV, SSM_XBC)), full((1, SSM_XBC)), full((1, LANES)), full((1, LANES)),
        full((1, SSM_INNER)), full((1, SSM_INNER)), full((LANES, SSM_INNER)), full((LANES, SSM_HEADS * LANES)),
    ]
    return pl.pallas_call(
        _ssd_kernel,
        out_shape=jax.ShapeDtypeStruct((bsz * seq, SSM_INNER), BF16),
        grid=(bsz, nc),
        in_specs=in_specs,
        out_specs=pl.BlockSpec((t, SSM_INNER), lambda b, c: (row(b, c), 0)),
        scratch_shapes=[pltpu.VMEM((t + 8, SSM_XBC), F32),
                        pltpu.VMEM((SSM_GROUPS, SSM_STATE, SSM_GW), F32)],
        compiler_params=_cparams(("parallel", "arbitrary")),
        name="ssd_mixer",
    )(main, main, main, main, small, conv_w, conv_b.reshape(1, SSM_XBC),
      _at_lanes(dt_bias, LANE_SSM_DT), _at_lanes(a_log, LANE_SSM_DT),
      jnp.repeat(d_skip, SSM_HEAD_DIM).reshape(1, SSM_INNER), norm_w.reshape(1, SSM_INNER), e_x, e_col)


def _fox_prep_kernel(f_ref, fb_ref, o_ref, carry_ref):
    t = f_ref.shape[0]

    @pl.when(pl.program_id(1) == 0)
    def _():
        carry_ref[...] = jnp.zeros_like(carry_ref)

    log_f = -_softplus(-(f_ref[...] + fb_ref[...]))
    tri = jnp.where(_lower_tri(t), 1.0, 0.0).astype(BF16)
    cs = _dot_sel_l(tri, log_f) + carry_ref[0:1, :]
    carry_ref[...] = jnp.broadcast_to(cs[t - 1:t, :], carry_ref.shape)
    o_ref[...] = cs.T[LANE_FOX_F:LANE_FOX_F + FOX_HEADS, :]


def fox_cumulative_log_f(small, f_bias, bsz, seq, t=512):
    nc = seq // t
    return pl.pallas_call(
        _fox_prep_kernel,
        out_shape=jax.ShapeDtypeStruct((bsz, FOX_HEADS, seq), F32),
        grid=(bsz, nc),
        in_specs=[pl.BlockSpec((t, LANES), lambda b, c: (b * nc + c, 1)),
                  pl.BlockSpec((1, LANES), lambda b, c: (0, 0))],
        out_specs=pl.BlockSpec((None, FOX_HEADS, t), lambda b, c: (b, 0, c)),
        scratch_shapes=[pltpu.VMEM((8, LANES), F32)],
        compiler_params=_cparams(("parallel", "arbitrary")),
        name="fox_cum_log_f",
    )(small, _at_lanes(f_bias, LANE_FOX_F))


def _fox_kernel(q_ref, k_ref, v_ref, cf_ref, o_ref):
    tq = q_ref.shape[0]
    tk = FOX_TK
    d = FOX_HEAD_DIM
    qi = pl.program_id(2)
    q = q_ref[...]
    scale2 = (d ** -0.5) * LOG2_E

    def step(j, carry, masked):
        m, l, acc = carry
        start = pl.multiple_of(j * tk, tk)
        k = k_ref[pl.ds(start, tk), :]
        v = v_ref[pl.ds(start, tk), :]
        s = _dot_nt(q, k) * scale2 - cf_ref[pl.ds(j, 1), :] * LOG2_E
        if masked:
            s = jnp.where(_lower_tri(tq), s, -jnp.inf)
        m_new = jnp.maximum(m, jnp.max(s, axis=-1, keepdims=True))
        alpha = jnp.exp2(m - m_new)
        p = jnp.exp2(s - m_new)
        l = alpha * l + jnp.sum(p, axis=-1, keepdims=True)
        acc = alpha * acc + _dot(p.astype(BF16), v)
        return m_new, l, acc

    m0 = jnp.full((tq, 1), -jnp.inf, F32)
    l0 = jnp.zeros((tq, 1), F32)
    acc0 = jnp.zeros((tq, d), F32)
    carry = lax.fori_loop(0, qi, lambda j, c: step(j, c, False), (m0, l0, acc0))
    _, l, acc = step(qi, carry, True)
    o_ref[...] = (acc / l).astype(o_ref.dtype)


def fox_mixer(main, cum_f, bsz, seq):
    tq = FOX_TQ
    assert FOX_TQ == FOX_TK
    nq = seq // tq
    d = FOX_HEAD_DIM
    cf = cum_f.reshape(bsz * FOX_HEADS, seq // FOX_TK, FOX_TK)
    return pl.pallas_call(
        _fox_kernel,
        out_shape=jax.ShapeDtypeStruct((bsz * seq, FOX_WIDTH), BF16),
        grid=(bsz, FOX_HEADS, nq),
        in_specs=[pl.BlockSpec((tq, d), lambda b, h, i: (b * nq + i, OFF_FOX_Q // d + h)),
                  pl.BlockSpec((seq, d), lambda b, h, i: (b, OFF_FOX_K // d + h)),
                  pl.BlockSpec((seq, d), lambda b, h, i: (b, OFF_FOX_V // d + h)),
                  pl.BlockSpec((None, seq // FOX_TK, FOX_TK), lambda b, h, i: (b * FOX_HEADS + h, 0, 0))],
        out_specs=pl.BlockSpec((tq, d), lambda b, h, i: (b * nq + i, h)),
        compiler_params=_cparams(("parallel", "parallel", "arbitrary")),
        name="fox_attention",
    )(main, main, main, cf)


INV_BASE = 32


def _unit_lower_inverse(a, blk_xor, eye):
    t = a[0].shape[0]
    n = range(len(a))
    pw = [jnp.where(blk_xor == 0, ai, 0.0) for ai in a]
    x = [eye - p for p in pw]
    for _ in range(INV_BASE.bit_length() - 2):
        pw = [_dot2(p, p) for p in pw]
        x = [x[i] + _dot2(x[i], pw[i]) for i in n]
    level = 1
    while INV_BASE * level < t:
        sel = (blk_xor >= level) & (blk_xor < 2 * level)
        xb = [xi.astype(BF16) for xi in x]
        xo = [_dot(xb[i], jnp.where(sel, a[i], 0.0).astype(BF16)).astype(BF16) for i in n]
        x = [x[i] - _dot(xo[i], xb[i]) for i in n]
        level *= 2
    return x


def _gdn_kernel(q_ref, k_ref, v_ref, z_ref, sm_ref, cwq_ref, cwk_ref, cwv_ref, dtb_ref, alog_ref,
                nw_ref, rep_ref, o_ref, ext_ref, state_ref):
    t = q_ref.shape[0]
    d = GDN_HEAD_DIM
    hb = GDN_HB
    w = hb * d
    chunk = pl.program_id(2)

    @pl.when(chunk == 0)
    def _():
        ext_ref[0:8, :] = jnp.zeros((8, 3 * w), F32)
        state_ref[...] = jnp.zeros_like(state_ref)

    @pl.when(chunk > 0)
    def _():
        ext_ref[0:8, :] = ext_ref[t:t + 8, :]

    ext_ref[8:8 + t, 0:w] = q_ref[...].astype(F32)
    ext_ref[8:8 + t, w:2 * w] = k_ref[...].astype(F32)
    ext_ref[8:8 + t, 2 * w:3 * w] = v_ref[...].astype(F32)

    def conv(lo, hi, w_ref):
        win = ext_ref[0:8 + t, lo:hi]
        acc = w_ref[GDN_CONV - 1:GDN_CONV, :] * win[8:, :]
        for back in range(1, GDN_CONV):
            j = GDN_CONV - 1 - back
            acc = acc + w_ref[j:j + 1, :] * pltpu.roll(win, back, axis=0)[8:, :]
        return _silu(acc)

    qc = conv(0, w, cwq_ref)
    kc = conv(w, 2 * w, cwk_ref)
    vc = conv(2 * w, 3 * w, cwv_ref)

    sm = sm_ref[...]
    lane = lax.broadcasted_iota(jnp.int32, sm.shape, 1)
    g_all = -jnp.exp(alog_ref[...]) * _softplus(sm + dtb_ref[...])
    mix = jnp.where(lane < LANE_GDN_B, g_all, _sigmoid(sm))
    rep = _dot_sel_r(mix, rep_ref[...])

    lower = _lower_tri(t)
    strict = _lower_tri(t, strict=True)
    tri = jnp.where(lower, 1.0, 0.0).astype(BF16)
    eye = jnp.where(lower, 1.0, 0.0) - jnp.where(strict, 1.0, 0.0)
    rows = lax.broadcasted_iota(jnp.int32, (t, t), 0)
    cols = lax.broadcasted_iota(jnp.int32, (t, t), 1)
    shift = INV_BASE.bit_length() - 1
    blk_xor = jnp.right_shift(rows, shift) ^ jnp.right_shift(cols, shift)
    z = z_ref[...].astype(F32)
    gc_all = _dot_sel_l(tri, rep[:, 0:w])

    hs = range(hb)
    sls = [slice(h * d, (h + 1) * d) for h in hs]

    def l2n(x):
        return x * lax.rsqrt(jnp.sum(x * x, axis=-1, keepdims=True) + 1e-6)

    q = [l2n(qc[:, sl]) * (d ** -0.5) for sl in sls]
    k = [l2n(kc[:, sl]) for sl in sls]
    beta = [rep[:, (hb + h) * d:(hb + h + 1) * d] for h in hs]
    gc = [gc_all[:, sl] for sl in sls]
    decay = [jnp.exp(jnp.where(lower, g - g.T, -jnp.inf)) for g in gc]
    eg = [jnp.exp(g) for g in gc]
    kb = [k[h] * beta[h] for h in hs]
    k_b = [x.astype(BF16) for x in k]
    a_mat = [jnp.where(strict, _dot_nt(kb[h].astype(BF16), k_b[h]) * decay[h], 0.0) for h in hs]
    attn = [(_dot_nt(q[h].astype(BF16), k_b[h]) * decay[h]).astype(BF16) for h in hs]
    rhs = [jnp.concatenate([vc[:, sls[h]] * beta[h], kb[h] * eg[h]], axis=1).astype(BF16) for h in hs]
    t_inv = _unit_lower_inverse(a_mat, blk_xor, eye)
    sol = [_dot(t_inv[h].astype(BF16), rhs[h]) for h in hs]
    s_old = [state_ref[h] for h in hs]
    s_b = [s.astype(BF16) for s in s_old]
    v_new = [(sol[h][:, :d] - _dot(sol[h][:, d:].astype(BF16), s_b[h])).astype(BF16) for h in hs]
    o = [_dot((q[h] * eg[h]).astype(BF16), s_b[h]) + _dot(attn[h], v_new[h]) for h in hs]
    g_last = [g[t - 1:t, :] for g in gc]
    k_dec = [(k[h] * jnp.exp(g_last[h] - gc[h])).T.astype(BF16) for h in hs]
    for h in hs:
        state_ref[h] = s_old[h] * jnp.exp(g_last[h]) + _dot(k_dec[h], v_new[h])
    nw = nw_ref[...]
    outs = [o[h] * lax.rsqrt(jnp.mean(o[h] * o[h], axis=-1, keepdims=True) + NORM_EPS) * nw
            * _silu(z[:, sls[h]]) for h in hs]
    o_ref[...] = jnp.concatenate(outs, axis=1).astype(o_ref.dtype)


def gdn_mixer(main, small, conv_w, dt_bias, a_log, norm_w, bsz, seq):
    t = GDN_CHUNK
    nc = seq // t
    hb = GDN_HB
    d = GDN_HEAD_DIM
    w = hb * d
    ng = GDN_HEADS // hb

    def row(b, hg, c):
        return b * nc + c

    src = jnp.arange(LANES)[None, :, None]
    col_head = (jnp.arange(2 * w) // d)[None, None, :]
    hg_idx = jnp.arange(ng)[:, None, None]
    want = jnp.where(col_head < hb, LANE_GDN_A + hg_idx * hb + col_head,
                     LANE_GDN_B + hg_idx * hb + (col_head - hb))
    rep = (src == want).astype(BF16)

    in_specs = [
        pl.BlockSpec((t, w), lambda b, hg, c: (row(b, hg, c), OFF_GDN_Q // w + hg)),
        pl.BlockSpec((t, w), lambda b, hg, c: (row(b, hg, c), OFF_GDN_K // w + hg)),
        pl.BlockSpec((t, w), lambda b, hg, c: (row(b, hg, c), OFF_GDN_V // w + hg)),
        pl.BlockSpec((t, w), lambda b, hg, c: (row(b, hg, c), OFF_GDN_Z // w + hg)),
        pl.BlockSpec((t, LANES), lambda b, hg, c: (row(b, hg, c), 2)),
        pl.BlockSpec((GDN_CONV, w), lambda b, hg, c: (0, hg)),
        pl.BlockSpec((GDN_CONV, w), lambda b, hg, c: (0, ng + hg)),
        pl.BlockSpec((GDN_CONV, w), lambda b, hg, c: (0, 2 * ng + hg)),
        pl.BlockSpec((1, LANES), lambda b, hg, c: (0, 0)),
        pl.BlockSpec((1, LANES), lambda b, hg, c: (0, 0)),
        pl.BlockSpec((1, d), lambda b, hg, c: (0, 0)),
        pl.BlockSpec((None, LANES, 2 * w), lambda b, hg, c: (hg, 0, 0)),
    ]
    return pl.pallas_call(
        _gdn_kernel,
        out_shape=jax.ShapeDtypeStruct((bsz * seq, GDN_WIDTH), BF16),
        grid=(bsz, ng, nc),
        in_specs=in_specs,
        out_specs=pl.BlockSpec((t, w), lambda b, hg, c: (row(b, hg, c), hg)),
        scratch_shapes=[pltpu.VMEM((t + 8, 3 * w), F32), pltpu.VMEM((hb, d, d), F32)],
        compiler_params=_cparams(("parallel", "parallel", "arbitrary")),
        name="gdn_mixer",
    )(main, main, main, main, small, conv_w, conv_w, conv_w,
      _at_lanes(dt_bias, LANE_GDN_A), _at_lanes(a_log, LANE_GDN_A), norm_w.reshape(1, d), rep)


def _merge_kernel(ys_ref, yf_ref, yg_ref, ws_ref, wf_ref, wg_ref, gs_ref, gf_ref, gg_ref, o_ref,
                  wsb_ref, wfb_ref, wgb_ref):
    @pl.when(pl.program_id(1) == 0)
    def _():
        wsb_ref[...] = ws_ref[...].astype(BF16)
        wfb_ref[...] = wf_ref[...].astype(BF16)
        wgb_ref[...] = wg_ref[...].astype(BF16)

    acc = _sigmoid(gs_ref[...].astype(F32)) * _dot(ys_ref[...], wsb_ref[...])
    acc = acc + _sigmoid(gf_ref[...].astype(F32)) * _dot(yf_ref[...], wfb_ref[...])
    acc = acc + _sigmoid(gg_ref[...].astype(F32)) * _dot(yg_ref[...], wgb_ref[...])
    o_ref[...] = acc.astype(o_ref.dtype)


def merge_branches(y_ssm, y_fox, y_gdn, w_ssm, w_fox, w_gdn, main, layer, tm=TM, tn=512):
    m = y_ssm.shape[0]
    n = D_MODEL
    gate_blk = OFF_GATE // tn
    per_gate = D_MODEL // tn

    def lhs(width):
        return pl.BlockSpec((tm, width), lambda j, i: (i, 0))

    def rhs(width):
        return pl.BlockSpec((None, width, tn), lambda j, i: (layer, 0, j))

    def gate(idx):
        return pl.BlockSpec((tm, tn), lambda j, i: (i, gate_blk + idx * per_gate + j))

    return pl.pallas_call(
        _merge_kernel,
        out_shape=jax.ShapeDtypeStruct((m, n), BF16),
        grid=(n // tn, m // tm),
        in_specs=[lhs(SSM_INNER), lhs(FOX_WIDTH), lhs(GDN_WIDTH),
                  rhs(SSM_INNER), rhs(FOX_WIDTH), rhs(GDN_WIDTH),
                  gate(0), gate(1), gate(2)],
        out_specs=pl.BlockSpec((tm, tn), lambda j, i: (i, j)),
        scratch_shapes=[pltpu.VMEM((SSM_INNER, tn), BF16), pltpu.VMEM((FOX_WIDTH, tn), BF16),
                        pltpu.VMEM((GDN_WIDTH, tn), BF16)],
        compiler_params=_cparams(("arbitrary", "arbitrary")),
        name="merge_branches",
    )(y_ssm, y_fox, y_gdn, w_ssm, w_fox, w_gdn, main, main, main)


def _out_proj_kernel(a_ref, w_ref, x_ref, o_ref, wb_ref):
    @pl.when(pl.program_id(1) == 0)
    def _():
        wb_ref[...] = w_ref[...].astype(BF16)

    o_ref[...] = x_ref[...] + _dot(a_ref[...], wb_ref[...])


def out_proj(a, w, x, layer, tm=TM, tn=512):
    m, k = a.shape
    n = w.shape[2]
    return pl.pallas_call(
        _out_proj_kernel,
        out_shape=jax.ShapeDtypeStruct((m, n), F32),
        grid=(n // tn, m // tm),
        in_specs=[pl.BlockSpec((tm, k), lambda j, i: (i, 0)),
                  pl.BlockSpec((None, k, tn), lambda j, i: (layer, 0, j)),
                  pl.BlockSpec((tm, tn), lambda j, i: (i, j))],
        out_specs=pl.BlockSpec((tm, tn), lambda j, i: (i, j)),
        scratch_shapes=[pltpu.VMEM((k, tn), BF16)],
        compiler_params=_cparams(("arbitrary", "arbitrary")),
        name="out_proj",
    )(a, w, x)


def _residual_matmul_kernel(a_ref, w_ref, x_ref, o_ref):
    kk = pl.program_id(2)

    @pl.when(kk == 0)
    def _():
        o_ref[...] = x_ref[...] + _dot(a_ref[...], w_ref[...])

    @pl.when(kk > 0)
    def _():
        o_ref[...] += _dot(a_ref[...], w_ref[...])


def ffn_down(a, w, x, layer, tm=TM, tn=512, tk=D_FF // 2):
    m, k = a.shape
    n = w.shape[2]
    return pl.pallas_call(
        _residual_matmul_kernel,
        out_shape=jax.ShapeDtypeStruct((m, n), F32),
        grid=(m // tm, n // tn, k // tk),
        in_specs=[pl.BlockSpec((tm, tk), lambda i, j, kk: (i, kk)),
                  pl.BlockSpec((None, tk, tn), lambda i, j, kk: (layer, kk, j)),
                  pl.BlockSpec((tm, tn), lambda i, j, kk: (i, j))],
        out_specs=pl.BlockSpec((tm, tn), lambda i, j, kk: (i, j)),
        compiler_params=_cparams(("parallel", "parallel", "arbitrary")),
        name="ffn_down",
    )(a, w, x)


def _ffn_up_kernel(h_ref, wg_ref, wv_ref, cwg_ref, cwv_ref, cbg_ref, cbv_ref, o_ref,
                   ug_ref, uv_ref, tg_ref, tv_ref, *, tiles_per_seq):
    tm = h_ref.shape[0]
    i = pl.program_id(0)
    j = pl.program_id(1)

    @pl.when(i == 0)
    def _():
        tg_ref[j] = jnp.zeros(tg_ref.shape[1:], F32)
        tv_ref[j] = jnp.zeros(tv_ref.shape[1:], F32)

    first = (i % tiles_per_seq) == 0
    h = h_ref[...]

    def conv_product(u_ref, t_ref, w_ref, cw_ref, cb_ref):
        u_ref[0:HALO, :] = jnp.where(first, 0.0, t_ref[j])
        u_ref[HALO:HALO + tm, :] = _dot(h, w_ref[...])
        t_ref[j] = u_ref[tm:tm + HALO, :]
        win = u_ref[HALO - 8:HALO + tm, :]
        acc = cb_ref[...] + cw_ref[FFN_CONV - 1:FFN_CONV, :] * win[8:, :]
        for back in range(1, FFN_CONV):
            tap = FFN_CONV - 1 - back
            acc = acc + cw_ref[tap:tap + 1, :] * pltpu.roll(win, back, axis=0)[8:, :]
        return acc

    gate = conv_product(ug_ref, tg_ref, wg_ref, cwg_ref, cbg_ref)
    val = conv_product(uv_ref, tv_ref, wv_ref, cwv_ref, cbv_ref)
    o_ref[...] = (_silu(gate) * val).astype(o_ref.dtype)


def ffn_up(h, w_gv, cw_gate, cw_val, cb_gate, cb_val, layer, seq, tm=TM, tn=512):
    m, k = h.shape
    nb = D_FF_PAD // tn
    kern = functools.partial(_ffn_up_kernel, tiles_per_seq=seq // tm)

    def w_spec(half):
        return pl.BlockSpec((None, None, k, tn), lambda i, j: (layer, half, 0, j))

    def col_spec(rows):
        return pl.BlockSpec((rows, tn), lambda i, j: (0, j))

    return pl.pallas_call(
        kern,
        out_shape=jax.ShapeDtypeStruct((m, D_FF), BF16),
        grid=(m // tm, nb),
        in_specs=[pl.BlockSpec((tm, k), lambda i, j: (i, 0)),
                  w_spec(0), w_spec(1), col_spec(FFN_CONV), col_spec(FFN_CONV), col_spec(1), col_spec(1)],
        out_specs=pl.BlockSpec((tm, tn), lambda i, j: (i, j)),
        scratch_shapes=[pltpu.VMEM((HALO + tm, tn), F32)] * 2
                       + [pltpu.VMEM((nb, HALO, tn), F32)] * 2,
        compiler_params=_cparams(("arbitrary", "arbitrary")),
        name="ffn_up",
    )(h, w_gv, w_gv, cw_gate, cw_val, cb_gate, cb_val)


def _w_up_prep_kernel(w_ref, o_ref, *, nb):
    o_ref[...] = jnp.where(pl.program_id(2) < nb, w_ref[...], 0.0).astype(BF16)


def w_up_prep(w_up, tn=256):
    depth, k, _ = w_up.shape
    nb = D_FF // tn
    nbp = D_FF_PAD // tn
    return pl.pallas_call(
        functools.partial(_w_up_prep_kernel, nb=nb),
        out_shape=jax.ShapeDtypeStruct((depth, 2, k, D_FF_PAD), BF16),
        grid=(depth, 2, nbp),
        in_specs=[pl.BlockSpec((None, k, tn), lambda l, half, j: (l, 0, jnp.minimum(half * nb + j, 2 * nb - 1)))],
        out_specs=pl.BlockSpec((None, None, k, tn), lambda l, half, j: (l, half, 0, j)),
        compiler_params=_cparams(("parallel", "parallel", "parallel")),
        name="w_up_prep",
    )(w_up)


def _pad_ffn_conv(conv_w, conv_b):
    pad = D_FF_PAD - D_FF

    def halves(a):
        return tuple(jnp.pad(p, ((0, 0), (0, pad))) for p in (a[:, :D_FF], a[:, D_FF:]))

    return halves(conv_w), halves(conv_b.reshape(1, -1))


def kernel(x, norm_mix, w_in, ssm_conv_w, ssm_conv_b, ssm_dt_bias, ssm_a_log, ssm_d, ssm_norm,
           fox_f_bias, gdn_conv_w, gdn_dt_bias, gdn_a_log, gdn_norm, w_br_ssm, w_br_fox, w_br_gdn,
           w_out, norm_ffn, w_up, ffn_conv_w, ffn_conv_b, w_down, norm_final):
    bsz, seq, d = x.shape
    depth = w_in.shape[0]
    m = bsz * seq
    xf = x.reshape(m, d)
    w_down_b = w_down.astype(BF16)
    w_in_t = jnp.swapaxes(w_in, 1, 2)
    w_gv = w_up_prep(w_up)
    for layer in range(depth):
        h, small = rmsnorm_small(xf, norm_mix[layer], w_in_t, layer)
        main = in_proj_main(h, w_in_t, layer)
        y_ssm = ssd_mixer(main, small, ssm_conv_w[layer], ssm_conv_b[layer], ssm_dt_bias[layer],
                          ssm_a_log[layer], ssm_d[layer], ssm_norm[layer], bsz, seq)
        cum_f = fox_cumulative_log_f(small, fox_f_bias[layer], bsz, seq)
        y_fox = fox_mixer(main, cum_f, bsz, seq)
        y_gdn = gdn_mixer(main, small, gdn_conv_w[layer], gdn_dt_bias[layer], gdn_a_log[layer],
                          gdn_norm[layer], bsz, seq)
        merged = merge_branches(y_ssm, y_fox, y_gdn, w_br_ssm, w_br_fox, w_br_gdn, main, layer)
        xf = out_proj(merged, w_out, xf, layer)
        h2 = rmsnorm(xf, norm_ffn[layer], BF16)
        cw_p, cb_p = _pad_ffn_conv(ffn_conv_w[layer], ffn_conv_b[layer])
        act = ffn_up(h2, w_gv, *cw_p, *cb_p, layer, seq)
        xf = ffn_down(act, w_down_b, xf, layer)
    return rmsnorm(xf, norm_final, F32).reshape(bsz, seq, d)
```

```python
import functools

import jax
import jax.numpy as jnp
from jax import lax
from jax.experimental import pallas as pl
from jax.experimental.pallas import tpu as pltpu

F32 = jnp.float32
BF16 = jnp.bfloat16

D_MODEL = 4096
SSM_INNER = 2048
SSM_HEAD_DIM = 64
SSM_HEADS = 32
SSM_GROUPS = 4
SSM_STATE = 128
SSM_CONV = 4
SSM_XBC = SSM_INNER + 2 * SSM_GROUPS * SSM_STATE
SSM_HPG = SSM_HEADS // SSM_GROUPS
SSM_GW = SSM_HPG * SSM_HEAD_DIM
FOX_HEADS = 8
FOX_HEAD_DIM = 128
FOX_WIDTH = FOX_HEADS * FOX_HEAD_DIM
GDN_HEADS = 8
GDN_HEAD_DIM = 128
GDN_WIDTH = GDN_HEADS * GDN_HEAD_DIM
GDN_CONV = 4
D_FF = 11008
D_FF_PAD = 11264
FFN_CONV = 3
NORM_EPS = 1e-6
LOG2_E = 1.4426950408889634
N_MAIN = 24576
LANES = 128
HALO = 16

OFF_SSM_Z = 0
OFF_SSM_X = 2048
OFF_SSM_B = 4096
OFF_SSM_C = 4608
OFF_FOX_Q = 5120
OFF_FOX_K = 6144
OFF_FOX_V = 7168
OFF_GDN_Q = 8192
OFF_GDN_K = 9216
OFF_GDN_V = 10240
OFF_GDN_Z = 11264
OFF_GATE = 12288
MAIN_SEGMENTS = ((0, 0), (OFF_FOX_Q, SSM_HEADS), (OFF_GDN_Q, SSM_HEADS + FOX_HEADS),
                 (OFF_GDN_Z, SSM_HEADS + FOX_HEADS + 2 * GDN_HEADS))
SMALL_BLOCKS = (OFF_FOX_Q // LANES, (OFF_GDN_Q + SSM_HEADS) // LANES,
                (OFF_GDN_Z + SSM_HEADS + FOX_HEADS) // LANES)
LANE_SSM_DT = 0
LANE_FOX_F = SSM_HEADS
LANE_GDN_A = SSM_HEADS + FOX_HEADS
LANE_GDN_B = SSM_HEADS + FOX_HEADS + GDN_HEADS
N_SMALL = len(SMALL_BLOCKS) * LANES

SSD_CHUNK = 128
GDN_CHUNK = 128
GDN_HB = 8
FOX_TQ = 1024
FOX_TK = 1024
TM = 1024
VMEM_LIMIT = 56 * 1024 * 1024


def _cparams(sem):
    return pltpu.CompilerParams(dimension_semantics=sem, vmem_limit_bytes=VMEM_LIMIT)


def _sigmoid(x):
    return 1.0 / (1.0 + jnp.exp(-x))


def _silu(x):
    return x * _sigmoid(x)


def _softplus(x):
    return jnp.maximum(x, 0.0) + jnp.log(1.0 + jnp.exp(-jnp.abs(x)))


def _split3(x):
    hi = x.astype(BF16)
    r1 = x - hi.astype(F32)
    mid = r1.astype(BF16)
    lo = (r1 - mid.astype(F32)).astype(BF16)
    return hi, mid, lo


def _dot(a, b):
    return jnp.dot(a, b, preferred_element_type=F32)


def _dot_nt(a, b):
    return lax.dot_general(a, b, (((1,), (1,)), ((), ())), preferred_element_type=F32)


def _dot_sel_r(x, sel):
    hi, mid, lo = _split3(x)
    return (_dot(lo, sel) + _dot(mid, sel)) + _dot(hi, sel)


def _dot_sel_l(sel, x):
    hi, mid, lo = _split3(x)
    return (_dot(sel, lo) + _dot(sel, mid)) + _dot(sel, hi)


def _dot2(a, b):
    ah = a.astype(BF16)
    al = (a - ah.astype(F32)).astype(BF16)
    bh = b.astype(BF16)
    bl = (b - bh.astype(F32)).astype(BF16)
    return (_dot(al, bh) + _dot(ah, bl)) + _dot(ah, bh)


def _lower_tri(n, strict=False):
    r = lax.broadcasted_iota(jnp.int32, (n, n), 0)
    c = lax.broadcasted_iota(jnp.int32, (n, n), 1)
    return (r > c) if strict else (r >= c)


def _at_lanes(v, start):
    return jnp.pad(v, (start, LANES - start - v.shape[0])).reshape(1, LANES)


def _rmsnorm_kernel(x_ref, w_ref, o_ref):
    x = x_ref[...]
    ms = jnp.mean(x * x, axis=-1, keepdims=True)
    o_ref[...] = (x * lax.rsqrt(ms + NORM_EPS) * w_ref[...]).astype(o_ref.dtype)


def rmsnorm(x, w, out_dtype, tm=512):
    m, d = x.shape
    return pl.pallas_call(
        _rmsnorm_kernel,
        out_shape=jax.ShapeDtypeStruct((m, d), out_dtype),
        grid=(m // tm,),
        in_specs=[pl.BlockSpec((tm, d), lambda i: (i, 0)),
                  pl.BlockSpec((1, d), lambda i: (0, 0))],
        out_specs=pl.BlockSpec((tm, d), lambda i: (i, 0)),
        compiler_params=_cparams(("parallel",)),
        name="rmsnorm",
    )(x, w.reshape(1, d))


F32_SUBLANES = 8


def _in_proj_kernel(a_ref, w_hbm, o_ref, wf_ref, wb_ref, sem, *, layer, tn):
    j = pl.program_id(0)
    i = pl.program_id(1)

    def window_copy(blk):
        dropped = 0
        for start, d in MAIN_SEGMENTS[1:]:
            dropped = jnp.where(blk >= start // tn, d // F32_SUBLANES, dropped)
        row = pl.multiple_of((blk * (tn // F32_SUBLANES) + dropped) * F32_SUBLANES, F32_SUBLANES)
        return pltpu.make_async_copy(w_hbm.at[layer, pl.ds(row, tn), :], wf_ref, sem)

    @pl.when((j == 0) & (i == 0))
    def _():
        window_copy(j).start()

    @pl.when(i == 0)
    def _():
        window_copy(j).wait()
        wb_ref[...] = wf_ref[...].astype(BF16)

    @pl.when((i == 1) & (j + 1 < pl.num_programs(0)))
    def _():
        window_copy(j + 1).start()

    o_ref[...] = _dot_nt(a_ref[...], wb_ref[...]).astype(o_ref.dtype)


def in_proj_main(h, w_in_t, layer, tm=TM, tn=1024):
    m, k = h.shape
    assert all(s % tn == 0 and d % F32_SUBLANES == 0 for s, d in MAIN_SEGMENTS)
    assert m // tm >= 2
    return pl.pallas_call(
        functools.partial(_in_proj_kernel, layer=layer, tn=tn),
        out_shape=jax.ShapeDtypeStruct((m, N_MAIN), BF16),
        grid=(N_MAIN // tn, m // tm),
        in_specs=[pl.BlockSpec((tm, k), lambda j, i: (i, 0)),
                  pl.BlockSpec(memory_space=pl.ANY)],
        out_specs=pl.BlockSpec((tm, tn), lambda j, i: (i, j)),
        scratch_shapes=[pltpu.VMEM((tn, k), F32), pltpu.VMEM((tn, k), BF16), pltpu.SemaphoreType.DMA(())],
        compiler_params=_cparams(("arbitrary", "arbitrary")),
        name="in_proj_main",
    )(h, w_in_t)


def _rmsnorm_small_kernel(x_ref, w_ref, w0_ref, w1_ref, w2_ref, o_ref, sm_ref):
    x = x_ref[...]
    ms = jnp.mean(x * x, axis=-1, keepdims=True)
    h = (x * lax.rsqrt(ms + NORM_EPS) * w_ref[...]).astype(o_ref.dtype)
    o_ref[...] = h
    ws = jnp.concatenate([w0_ref[...], w1_ref[...], w2_ref[...]], axis=0).astype(BF16)
    sm_ref[...] = _dot_nt(h, ws)


def rmsnorm_small(x, w, w_in_t, layer, tm=512):
    m, d = x.shape

    def w_spec(blk):
        return pl.BlockSpec((None, LANES, d), lambda i: (layer, blk, 0))

    return pl.pallas_call(
        _rmsnorm_small_kernel,
        out_shape=(jax.ShapeDtypeStruct((m, d), BF16), jax.ShapeDtypeStruct((m, N_SMALL), F32)),
        grid=(m // tm,),
        in_specs=[pl.BlockSpec((tm, d), lambda i: (i, 0)),
                  pl.BlockSpec((1, d), lambda i: (0, 0))] + [w_spec(b) for b in SMALL_BLOCKS],
        out_specs=(pl.BlockSpec((tm, d), lambda i: (i, 0)), pl.BlockSpec((tm, N_SMALL), lambda i: (i, 0))),
        compiler_params=_cparams(("parallel",)),
        name="rmsnorm_small",
    )(x, w.reshape(1, d), w_in_t, w_in_t, w_in_t)


def _ssd_kernel(z_ref, xs_ref, b_ref, c_ref, dt_ref, cw_ref, cb_ref, dtb_ref, alog_ref, dskip_ref,
                nw_ref, ex_ref, ecol_ref, o_ref, ext_ref, state_ref):
    t = z_ref.shape[0]
    n = SSM_STATE
    gw = SSM_GW
    g = SSM_GROUPS
    chunk = pl.program_id(1)

    @pl.when(chunk == 0)
    def _():
        ext_ref[0:8, :] = jnp.zeros((8, SSM_XBC), F32)
        state_ref[...] = jnp.zeros_like(state_ref)

    @pl.when(chunk > 0)
    def _():
        ext_ref[0:8, :] = ext_ref[t:t + 8, :]

    ext_ref[8:8 + t, 0:SSM_INNER] = xs_ref[...].astype(F32)
    ext_ref[8:8 + t, SSM_INNER:SSM_INNER + g * n] = b_ref[...].astype(F32)
    ext_ref[8:8 + t, SSM_INNER + g * n:SSM_XBC] = c_ref[...].astype(F32)

    def conv(lo, hi):
        win = ext_ref[0:8 + t, lo:hi]
        acc = cb_ref[:, lo:hi] + cw_ref[SSM_CONV - 1:SSM_CONV, lo:hi] * win[8:, :]
        for back in range(1, SSM_CONV):
            j = SSM_CONV - 1 - back
            acc = acc + cw_ref[j:j + 1, lo:hi] * pltpu.roll(win, back, axis=0)[8:, :]
        return _silu(acc)

    dt = _softplus(dt_ref[...] + dtb_ref[...])
    a = -jnp.exp(alog_ref[...])
    causal = _lower_tri(t)
    tri = jnp.where(causal, 1.0, 0.0).astype(BF16)
    a_cs = _dot_sel_l(tri, dt * a)
    a_last = a_cs[t - 1:t, :]
    ex = ex_ref[...]
    dt_x = _dot_sel_r(dt, ex)
    ea_x = _dot_sel_r(jnp.exp(a_cs), ex)
    eb_x = _dot_sel_r(jnp.exp(a_last - a_cs), ex)
    a_col = _dot_sel_r(a_cs, ecol_ref[...])
    a_row = a_cs.T
    lane = lax.broadcasted_iota(jnp.int32, (t, 2 * SSM_HEAD_DIM), 1)
    z = z_ref[...].astype(F32)

    outs = []
    for gi in range(g):
        cols = slice(gi * gw, (gi + 1) * gw)
        xs = conv(gi * gw, (gi + 1) * gw)
        bm = conv(SSM_INNER + gi * n, SSM_INNER + (gi + 1) * n)
        cm = conv(SSM_INNER + (g + gi) * n, SSM_INNER + (g + gi + 1) * n)
        xdt = xs * dt_x[:, cols]
        xdt_b = xdt.astype(BF16)
        cm_b = cm.astype(BF16)
        cb = _dot_nt(cm_b, bm.astype(BF16))
        pieces = []
        for p in range(SSM_HPG // 2):
            x_pair = xdt_b[:, p * 128:(p + 1) * 128]
            res = []
            for i in (2 * p, 2 * p + 1):
                hd = gi * SSM_HPG + i
                seg = a_col[:, hd * 128:(hd + 1) * 128] - a_row[hd:hd + 1, :]
                decay = jnp.exp(jnp.where(causal, seg, -jnp.inf))
                res.append(_dot((cb * decay).astype(BF16), x_pair))
            pieces.append(jnp.where(lane < SSM_HEAD_DIM, res[0], res[1]))
        y_diag = jnp.concatenate(pieces, axis=1)

        state = state_ref[gi]
        y_off = ea_x[:, cols] * _dot(cm_b, state.astype(BF16))
        state_ref[gi] = (state * ea_x[t - 1:t, cols]
                         + _dot(bm.T.astype(BF16), (eb_x[:, cols] * xdt).astype(BF16)))

        y = y_diag + y_off + xs * dskip_ref[:, cols]
        y = y * _silu(z[:, cols])
        ms = jnp.mean(y * y, axis=-1, keepdims=True)
        outs.append(y * lax.rsqrt(ms + NORM_EPS) * nw_ref[:, cols])
    o_ref[...] = jnp.concatenate(outs, axis=1).astype(o_ref.dtype)


def ssd_mixer(main, small, conv_w, conv_b, dt_bias, a_log, d_skip, norm_w, bsz, seq):
    t = SSD_CHUNK
    nc = seq // t
    gn = SSM_GROUPS * SSM_STATE

    def row(b, c):
        return b * nc + c

    heads = jnp.arange(LANES)[:, None]
    e_x = (heads == (jnp.arange(SSM_INNER)[None, :] // SSM_HEAD_DIM)).astype(BF16)
    e_col = (heads == (jnp.arange(SSM_HEADS * LANES)[None, :] // LANES)).astype(BF16)

    def full(shape):
        return pl.BlockSpec(shape, lambda b, c: (0,) * len(shape))

    in_specs = [
        pl.BlockSpec((t, SSM_INNER), lambda b, c: (row(b, c), OFF_SSM_Z // SSM_INNER)),
        pl.BlockSpec((t, SSM_INNER), lambda b, c: (row(b, c), OFF_SSM_X // SSM_INNER)),
        pl.BlockSpec((t, gn), lambda b, c: (row(b, c), OFF_SSM_B // gn)),
        pl.BlockSpec((t, gn), lambda b, c: (row(b, c), OFF_SSM_C // gn)),
        pl.BlockSpec((t, LANES), lambda b, c: (row(b, c), 0)),
        full((SSM_CONV, SSM_XBC)), full((1, SSM_XBC)), full((1, LANES)), full((1, LANES)),
        full((1, SSM_INNER)), full((1, SSM_INNER)), full((LANES, SSM_INNER)), full((LANES, SSM_HEADS * LANES)),
    ]
    return pl.pallas_call(
        _ssd_kernel,
        out_shape=jax.ShapeDtypeStruct((bsz * seq, SSM_INNER), BF16),
        grid=(bsz, nc),
        in_specs=in_specs,
        out_specs=pl.BlockSpec((t, SSM_INNER), lambda b, c: (row(b, c), 0)),
        scratch_shapes=[pltpu.VMEM((t + 8, SSM_XBC), F32),
                        pltpu.VMEM((SSM_GROUPS, SSM_STATE, SSM_GW), F32)],
        compiler_params=_cparams(("parallel", "arbitrary")),
        name="ssd_mixer",
    )(main, main, main, main, small, conv_w, conv_b.reshape(1, SSM_XBC),
      _at_lanes(dt_bias, LANE_SSM_DT), _at_lanes(a_log, LANE_SSM_DT),
      jnp.repeat(d_skip, SSM_HEAD_DIM).reshape(1, SSM_INNER), norm_w.reshape(1, SSM_INNER), e_x, e_col)


def _fox_prep_kernel(f_ref, fb_ref, o_ref, carry_ref):
    t = f_ref.shape[0]

    @pl.when(pl.program_id(1) == 0)
    def _():
        carry_ref[...] = jnp.zeros_like(carry_ref)

    log_f = -_softplus(-(f_ref[...] + fb_ref[...]))
    tri = jnp.where(_lower_tri(t), 1.0, 0.0).astype(BF16)
    cs = _dot_sel_l(tri, log_f) + carry_ref[0:1, :]
    carry_ref[...] = jnp.broadcast_to(cs[t - 1:t, :], carry_ref.shape)
    o_ref[...] = cs.T[LANE_FOX_F:LANE_FOX_F + FOX_HEADS, :]


def fox_cumulative_log_f(small, f_bias, bsz, seq, t=512):
    nc = seq // t
    return pl.pallas_call(
        _fox_prep_kernel,
        out_shape=jax.ShapeDtypeStruct((bsz, FOX_HEADS, seq), F32),
        grid=(bsz, nc),
        in_specs=[pl.BlockSpec((t, LANES), lambda b, c: (b * nc + c, 1)),
                  pl.BlockSpec((1, LANES), lambda b, c: (0, 0))],
        out_specs=pl.BlockSpec((None, FOX_HEADS, t), lambda b, c: (b, 0, c)),
        scratch_shapes=[pltpu.VMEM((8, LANES), F32)],
        compiler_params=_cparams(("parallel", "arbitrary")),
        name="fox_cum_log_f",
    )(small, _at_lanes(f_bias, LANE_FOX_F))


def _fox_kernel(q_ref, k_ref, v_ref, cf_ref, o_ref):
    tq = q_ref.shape[0]
    tk = FOX_TK
    d = FOX_HEAD_DIM
    qi = pl.program_id(2)
    q = q_ref[...]
    scale2 = (d ** -0.5) * LOG2_E

    def step(j, carry, masked):
        m, l, acc = carry
        start = pl.multiple_of(j * tk, tk)
        k = k_ref[pl.ds(start, tk), :]
        v = v_ref[pl.ds(start, tk), :]
        s = _dot_nt(q, k) * scale2 - cf_ref[pl.ds(j, 1), :] * LOG2_E
        if masked:
            s = jnp.where(_lower_tri(tq), s, -jnp.inf)
        m_new = jnp.maximum(m, jnp.max(s, axis=-1, keepdims=True))
        alpha = jnp.exp2(m - m_new)
        p = jnp.exp2(s - m_new)
        l = alpha * l + jnp.sum(p, axis=-1, keepdims=True)
        acc = alpha * acc + _dot(p.astype(BF16), v)
        return m_new, l, acc

    m0 = jnp.full((tq, 1), -jnp.inf, F32)
    l0 = jnp.zeros((tq, 1), F32)
    acc0 = jnp.zeros((tq, d), F32)
    carry = lax.fori_loop(0, qi, lambda j, c: step(j, c, False), (m0, l0, acc0))
    _, l, acc = step(qi, carry, True)
    o_ref[...] = (acc / l).astype(o_ref.dtype)


def fox_mixer(main, cum_f, bsz, seq):
    tq = FOX_TQ
    assert FOX_TQ == FOX_TK
    nq = seq // tq
    d = FOX_HEAD_DIM
    cf = cum_f.reshape(bsz * FOX_HEADS, seq // FOX_TK, FOX_TK)
    return pl.pallas_call(
        _fox_kernel,
        out_shape=jax.ShapeDtypeStruct((bsz * seq, FOX_WIDTH), BF16),
        grid=(bsz, FOX_HEADS, nq),
        in_specs=[pl.BlockSpec((tq, d), lambda b, h, i: (b * nq + i, OFF_FOX_Q // d + h)),
                  pl.BlockSpec((seq, d), lambda b, h, i: (b, OFF_FOX_K // d + h)),
                  pl.BlockSpec((seq, d), lambda b, h, i: (b, OFF_FOX_V // d + h)),
                  pl.BlockSpec((None, seq // FOX_TK, FOX_TK), lambda b, h, i: (b * FOX_HEADS + h, 0, 0))],
        out_specs=pl.BlockSpec((tq, d), lambda b, h, i: (b * nq + i, h)),
        compiler_params=_cparams(("parallel", "parallel", "arbitrary")),
        name="fox_attention",
    )(main, main, main, cf)


INV_BASE = 32


def _unit_lower_inverse(a, blk_xor, eye):
    t = a[0].shape[0]
    n = range(len(a))
    pw = [jnp.where(blk_xor == 0, ai, 0.0) for ai in a]
    x = [eye - p for p in pw]
    for _ in range(INV_BASE.bit_length() - 2):
        pw = [_dot2(p, p) for p in pw]
        x = [x[i] + _dot2(x[i], pw[i]) for i in n]
    level = 1
    while INV_BASE * level < t:
        sel = (blk_xor >= level) & (blk_xor < 2 * level)
        xb = [xi.astype(BF16) for xi in x]
        xo = [_dot(xb[i], jnp.where(sel, a[i], 0.0).astype(BF16)).astype(BF16) for i in n]
        x = [x[i] - _dot(xo[i], xb[i]) for i in n]
        level *= 2
    return x


def _gdn_kernel(q_ref, k_ref, v_ref, z_ref, sm_ref, cwq_ref, cwk_ref, cwv_ref, dtb_ref, alog_ref,
                nw_ref, rep_ref, o_ref, ext_ref, state_ref):
    t = q_ref.shape[0]
    d = GDN_HEAD_DIM
    hb = GDN_HB
    w = hb * d
    chunk = pl.program_id(2)

    @pl.when(chunk == 0)
    def _():
        ext_ref[0:8, :] = jnp.zeros((8, 3 * w), F32)
        state_ref[...] = jnp.zeros_like(state_ref)

    @pl.when(chunk > 0)
    def _():
        ext_ref[0:8, :] = ext_ref[t:t + 8, :]

    ext_ref[8:8 + t, 0:w] = q_ref[...].astype(F32)
    ext_ref[8:8 + t, w:2 * w] = k_ref[...].astype(F32)
    ext_ref[8:8 + t, 2 * w:3 * w] = v_ref[...].astype(F32)

    def conv(lo, hi, w_ref):
        win = ext_ref[0:8 + t, lo:hi]
        acc = w_ref[GDN_CONV - 1:GDN_CONV, :] * win[8:, :]
        for back in range(1, GDN_CONV):
            j = GDN_CONV - 1 - back
            acc = acc + w_ref[j:j + 1, :] * pltpu.roll(win, back, axis=0)[8:, :]
        return _silu(acc)

    qc = conv(0, w, cwq_ref)
    kc = conv(w, 2 * w, cwk_ref)
    vc = conv(2 * w, 3 * w, cwv_ref)

    sm = sm_ref[...]
    lane = lax.broadcasted_iota(jnp.int32, sm.shape, 1)
    g_all = -jnp.exp(alog_ref[...]) * _softplus(sm + dtb_ref[...])
    mix = jnp.where(lane < LANE_GDN_B, g_all, _sigmoid(sm))
    rep = _dot_sel_r(mix, rep_ref[...])

    lower = _lower_tri(t)
    strict = _lower_tri(t, strict=True)
    tri = jnp.where(lower, 1.0, 0.0).astype(BF16)
    eye = jnp.where(lower, 1.0, 0.0) - jnp.where(strict, 1.0, 0.0)
    rows = lax.broadcasted_iota(jnp.int32, (t, t), 0)
    cols = lax.broadcasted_iota(jnp.int32, (t, t), 1)
    shift = INV_BASE.bit_length() - 1
    blk_xor = jnp.right_shift(rows, shift) ^ jnp.right_shift(cols, shift)
    z = z_ref[...].astype(F32)
    gc_all = _dot_sel_l(tri, rep[:, 0:w])

    hs = range(hb)
    sls = [slice(h * d, (h + 1) * d) for h in hs]

    def l2n(x):
        return x * lax.rsqrt(jnp.sum(x * x, axis=-1, keepdims=True) + 1e-6)

    q = [l2n(qc[:, sl]) * (d ** -0.5) for sl in sls]
    k = [l2n(kc[:, sl]) for sl in sls]
    beta = [rep[:, (hb + h) * d:(hb + h + 1) * d] for h in hs]
    gc = [gc_all[:, sl] for sl in sls]
    decay = [jnp.exp(jnp.where(lower, g - g.T, -jnp.inf)) for g in gc]
    eg = [jnp.exp(g) for g in gc]
    kb = [k[h] * beta[h] for h in hs]
    k_b = [x.astype(BF16) for x in k]
    a_mat = [jnp.where(strict, _dot_nt(kb[h].astype(BF16), k_b[h]) * decay[h], 0.0) for h in hs]
    attn = [(_dot_nt(q[h].astype(BF16), k_b[h]) * decay[h]).astype(BF16) for h in hs]
    rhs = [jnp.concatenate([vc[:, sls[h]] * beta[h], kb[h] * eg[h]], axis=1).astype(BF16) for h in hs]
    t_inv = _unit_lower_inverse(a_mat, blk_xor, eye)
    sol = [_dot(t_inv[h].astype(BF16), rhs[h]) for h in hs]
    s_old = [state_ref[h] for h in hs]
    s_b = [s.astype(BF16) for s in s_old]
    v_new = [(sol[h][:, :d] - _dot(sol[h][:, d:].astype(BF16), s_b[h])).astype(BF16) for h in hs]
    o = [_dot((q[h] * eg[h]).astype(BF16), s_b[h]) + _dot(attn[h], v_new[h]) for h in hs]
    g_last = [g[t - 1:t, :] for g in gc]
    k_dec = [(k[h] * jnp.exp(g_last[h] - gc[h])).T.astype(BF16) for h in hs]
    for h in hs:
        state_ref[h] = s_old[h] * jnp.exp(g_last[h]) + _dot(k_dec[h], v_new[h])
    nw = nw_ref[...]
    outs = [o[h] * lax.rsqrt(jnp.mean(o[h] * o[h], axis=-1, keepdims=True) + NORM_EPS) * nw
            * _silu(z[:, sls[h]]) for h in hs]
    o_ref[...] = jnp.concatenate(outs, axis=1).astype(o_ref.dtype)


def gdn_mixer(main, small, conv_w, dt_bias, a_log, norm_w, bsz, seq):
    t = GDN_CHUNK
    nc = seq // t
    hb = GDN_HB
    d = GDN_HEAD_DIM
    w = hb * d
    ng = GDN_HEADS // hb

    def row(b, hg, c):
        return b * nc + c

    src = jnp.arange(LANES)[None, :, None]
    col_head = (jnp.arange(2 * w) // d)[None, None, :]
    hg_idx = jnp.arange(ng)[:, None, None]
    want = jnp.where(col_head < hb, LANE_GDN_A + hg_idx * hb + col_head,
                     LANE_GDN_B + hg_idx * hb + (col_head - hb))
    rep = (src == want).astype(BF16)

    in_specs = [
        pl.BlockSpec((t, w), lambda b, hg, c: (row(b, hg, c), OFF_GDN_Q // w + hg)),
        pl.BlockSpec((t, w), lambda b, hg, c: (row(b, hg, c), OFF_GDN_K // w + hg)),
        pl.BlockSpec((t, w), lambda b, hg, c: (row(b, hg, c), OFF_GDN_V // w + hg)),
        pl.BlockSpec((t, w), lambda b, hg, c: (row(b, hg, c), OFF_GDN_Z // w + hg)),
        pl.BlockSpec((t, LANES), lambda b, hg, c: (row(b, hg, c), 2)),
        pl.BlockSpec((GDN_CONV, w), lambda b, hg, c: (0, hg)),
        pl.BlockSpec((GDN_CONV, w), lambda b, hg, c: (0, ng + hg)),
        pl.BlockSpec((GDN_CONV, w), lambda b, hg, c: (0, 2 * ng + hg)),
        pl.BlockSpec((1, LANES), lambda b, hg, c: (0, 0)),
        pl.BlockSpec((1, LANES), lambda b, hg, c: (0, 0)),
        pl.BlockSpec((1, d), lambda b, hg, c: (0, 0)),
        pl.BlockSpec((None, LANES, 2 * w), lambda b, hg, c: (hg, 0, 0)),
    ]
    return pl.pallas_call(
        _gdn_kernel,
        out_shape=jax.ShapeDtypeStruct((bsz * seq, GDN_WIDTH), BF16),
        grid=(bsz, ng, nc),
        in_specs=in_specs,
        out_specs=pl.BlockSpec((t, w), lambda b, hg, c: (row(b, hg, c), hg)),
        scratch_shapes=[pltpu.VMEM((t + 8, 3 * w), F32), pltpu.VMEM((hb, d, d), F32)],
        compiler_params=_cparams(("parallel", "parallel", "arbitrary")),
        name="gdn_mixer",
    )(main, main, main, main, small, conv_w, conv_w, conv_w,
      _at_lanes(dt_bias, LANE_GDN_A), _at_lanes(a_log, LANE_GDN_A), norm_w.reshape(1, d), rep)


def _merge_kernel(ys_ref, yf_ref, yg_ref, ws_ref, wf_ref, wg_ref, gs_ref, gf_ref, gg_ref, o_ref,
                  wsb_ref, wfb_ref, wgb_ref):
    @pl.when(pl.program_id(1) == 0)
    def _():
        wsb_ref[...] = ws_ref[...].astype(BF16)
        wfb_ref[...] = wf_ref[...].astype(BF16)
        wgb_ref[...] = wg_ref[...].astype(BF16)

    acc = _sigmoid(gs_ref[...].astype(F32)) * _dot(ys_ref[...], wsb_ref[...])
    acc = acc + _sigmoid(gf_ref[...].astype(F32)) * _dot(yf_ref[...], wfb_ref[...])
    acc = acc + _sigmoid(gg_ref[...].astype(F32)) * _dot(yg_ref[...], wgb_ref[...])
    o_ref[...] = acc.astype(o_ref.dtype)


def merge_branches(y_ssm, y_fox, y_gdn, w_ssm, w_fox, w_gdn, main, layer, tm=TM, tn=512):
    m = y_ssm.shape[0]
    n = D_MODEL
    gate_blk = OFF_GATE // tn
    per_gate = D_MODEL // tn

    def lhs(width):
        return pl.BlockSpec((tm, width), lambda j, i: (i, 0))

    def rhs(width):
        return pl.BlockSpec((None, width, tn), lambda j, i: (layer, 0, j))

    def gate(idx):
        return pl.BlockSpec((tm, tn), lambda j, i: (i, gate_blk + idx * per_gate + j))

    return pl.pallas_call(
        _merge_kernel,
        out_shape=jax.ShapeDtypeStruct((m, n), BF16),
        grid=(n // tn, m // tm),
        in_specs=[lhs(SSM_INNER), lhs(FOX_WIDTH), lhs(GDN_WIDTH),
                  rhs(SSM_INNER), rhs(FOX_WIDTH), rhs(GDN_WIDTH),
                  gate(0), gate(1), gate(2)],
        out_specs=pl.BlockSpec((tm, tn), lambda j, i: (i, j)),
        scratch_shapes=[pltpu.VMEM((SSM_INNER, tn), BF16), pltpu.VMEM((FOX_WIDTH, tn), BF16),
                        pltpu.VMEM((GDN_WIDTH, tn), BF16)],
        compiler_params=_cparams(("arbitrary", "arbitrary")),
        name="merge_branches",
    )(y_ssm, y_fox, y_gdn, w_ssm, w_fox, w_gdn, main, main, main)


def _out_proj_kernel(a_ref, w_ref, x_ref, o_ref, wb_ref):
    @pl.when(pl.program_id(1) == 0)
    def _():
        wb_ref[...] = w_ref[...].astype(BF16)

    o_ref[...] = x_ref[...] + _dot(a_ref[...], wb_ref[...])


def out_proj(a, w, x, layer, tm=TM, tn=512):
    m, k = a.shape
    n = w.shape[2]
    return pl.pallas_call(
        _out_proj_kernel,
        out_shape=jax.ShapeDtypeStruct((m, n), F32),
        grid=(n // tn, m // tm),
        in_specs=[pl.BlockSpec((tm, k), lambda j, i: (i, 0)),
                  pl.BlockSpec((None, k, tn), lambda j, i: (layer, 0, j)),
                  pl.BlockSpec((tm, tn), lambda j, i: (i, j))],
        out_specs=pl.BlockSpec((tm, tn), lambda j, i: (i, j)),
        scratch_shapes=[pltpu.VMEM((k, tn), BF16)],
        compiler_params=_cparams(("arbitrary", "arbitrary")),
        name="out_proj",
    )(a, w, x)


def _residual_matmul_kernel(a_ref, w_ref, x_ref, o_ref):
    kk = pl.program_id(2)

    @pl.when(kk == 0)
    def _():
        o_ref[...] = x_ref[...] + _dot(a_ref[...], w_ref[...])

    @pl.when(kk > 0)
    def _():
        o_ref[...] += _dot(a_ref[...], w_ref[...])


def ffn_down(a, w, x, layer, tm=TM, tn=512, tk=D_FF // 2):
    m, k = a.shape
    n = w.shape[2]
    return pl.pallas_call(
        _residual_matmul_kernel,
        out_shape=jax.ShapeDtypeStruct((m, n), F32),
        grid=(m // tm, n // tn, k // tk),
        in_specs=[pl.BlockSpec((tm, tk), lambda i, j, kk: (i, kk)),
                  pl.BlockSpec((None, tk, tn), lambda i, j, kk: (layer, kk, j)),
                  pl.BlockSpec((tm, tn), lambda i, j, kk: (i, j))],
        out_specs=pl.BlockSpec((tm, tn), lambda i, j, kk: (i, j)),
        compiler_params=_cparams(("parallel", "parallel", "arbitrary")),
        name="ffn_down",
    )(a, w, x)


def _ffn_up_kernel(h_ref, wg_ref, wv_ref, cwg_ref, cwv_ref, cbg_ref, cbv_ref, o_ref,
                   ug_ref, uv_ref, tg_ref, tv_ref, *, tiles_per_seq):
    tm = h_ref.shape[0]
    i = pl.program_id(0)
    j = pl.program_id(1)

    @pl.when(i == 0)
    def _():
        tg_ref[j] = jnp.zeros(tg_ref.shape[1:], F32)
        tv_ref[j] = jnp.zeros(tv_ref.shape[1:], F32)

    first = (i % tiles_per_seq) == 0
    h = h_ref[...]

    def conv_product(u_ref, t_ref, w_ref, cw_ref, cb_ref):
        u_ref[0:HALO, :] = jnp.where(first, 0.0, t_ref[j])
        u_ref[HALO:HALO + tm, :] = _dot(h, w_ref[...])
        t_ref[j] = u_ref[tm:tm + HALO, :]
        win = u_ref[HALO - 8:HALO + tm, :]
        acc = cb_ref[...] + cw_ref[FFN_CONV - 1:FFN_CONV, :] * win[8:, :]
        for back in range(1, FFN_CONV):
            tap = FFN_CONV - 1 - back
            acc = acc + cw_ref[tap:tap + 1, :] * pltpu.roll(win, back, axis=0)[8:, :]
        return acc

    gate = conv_product(ug_ref, tg_ref, wg_ref, cwg_ref, cbg_ref)
    val = conv_product(uv_ref, tv_ref, wv_ref, cwv_ref, cbv_ref)
    o_ref[...] = (_silu(gate) * val).astype(o_ref.dtype)


def ffn_up(h, w_gv, cw_gate, cw_val, cb_gate, cb_val, layer, seq, tm=TM, tn=512):
    m, k = h.shape
    nb = D_FF_PAD // tn
    kern = functools.partial(_ffn_up_kernel, tiles_per_seq=seq // tm)

    def w_spec(half):
        return pl.BlockSpec((None, None, k, tn), lambda i, j: (layer, half, 0, j))

    def col_spec(rows):
        return pl.BlockSpec((rows, tn), lambda i, j: (0, j))

    return pl.pallas_call(
        kern,
        out_shape=jax.ShapeDtypeStruct((m, D_FF), BF16),
        grid=(m // tm, nb),
        in_specs=[pl.BlockSpec((tm, k), lambda i, j: (i, 0)),
                  w_spec(0), w_spec(1), col_spec(FFN_CONV), col_spec(FFN_CONV), col_spec(1), col_spec(1)],
        out_specs=pl.BlockSpec((tm, tn), lambda i, j: (i, j)),
        scratch_shapes=[pltpu.VMEM((HALO + tm, tn), F32)] * 2
                       + [pltpu.VMEM((nb, HALO, tn), F32)] * 2,
        compiler_params=_cparams(("arbitrary", "arbitrary")),
        name="ffn_up",
    )(h, w_gv, w_gv, cw_gate, cw_val, cb_gate, cb_val)


def _w_up_prep_kernel(w_ref, o_ref, *, nb):
    o_ref[...] = jnp.where(pl.program_id(2) < nb, w_ref[...], 0.0).astype(BF16)


def w_up_prep(w_up, tn=256):
    depth, k, _ = w_up.shape
    nb = D_FF // tn
    nbp = D_FF_PAD // tn
    return pl.pallas_call(
        functools.partial(_w_up_prep_kernel, nb=nb),
        out_shape=jax.ShapeDtypeStruct((depth, 2, k, D_FF_PAD), BF16),
        grid=(depth, 2, nbp),
        in_specs=[pl.BlockSpec((None, k, tn), lambda l, half, j: (l, 0, jnp.minimum(half * nb + j, 2 * nb - 1)))],
        out_specs=pl.BlockSpec((None, None, k, tn), lambda l, half, j: (l, half, 0, j)),
        compiler_params=_cparams(("parallel", "parallel", "parallel")),
        name="w_up_prep",
    )(w_up)


def _pad_ffn_conv(conv_w, conv_b):
    pad = D_FF_PAD - D_FF

    def halves(a):
        return tuple(jnp.pad(p, ((0, 0), (0, pad))) for p in (a[:, :D_FF], a[:, D_FF:]))

    return halves(conv_w), halves(conv_b.reshape(1, -1))


def kernel(x, norm_mix, w_in, ssm_conv_w, ssm_conv_b, ssm_dt_bias, ssm_a_log, ssm_d, ssm_norm,
           fox_f_bias, gdn_conv_w, gdn_dt_bias, gdn_a_log, gdn_norm, w_br_ssm, w_br_fox, w_br_gdn,
           w_out, norm_ffn, w_up, ffn_conv_w, ffn_conv_b, w_down, norm_final):
    bsz, seq, d = x.shape
    depth = w_in.shape[0]
    m = bsz * seq
    xf = x.reshape(m, d)
    w_down_b = w_down.astype(BF16)
    w_in_t = jnp.swapaxes(w_in, 1, 2)
    w_gv = w_up_prep(w_up)
    for layer in range(depth):
        h, small = rmsnorm_small(xf, norm_mix[layer], w_in_t, layer)
        main = in_proj_main(h, w_in_t, layer)
        y_ssm = ssd_mixer(main, small, ssm_conv_w[layer], ssm_conv_b[layer], ssm_dt_bias[layer],
                          ssm_a_log[layer], ssm_d[layer], ssm_norm[layer], bsz, seq)
        cum_f = fox_cumulative_log_f(small, fox_f_bias[layer], bsz, seq)
        y_fox = fox_mixer(main, cum_f, bsz, seq)
        y_gdn = gdn_mixer(main, small, gdn_conv_w[layer], gdn_dt_bias[layer], gdn_a_log[layer],
                          gdn_norm[layer], bsz, seq)
        merged = merge_branches(y_ssm, y_fox, y_gdn, w_br_ssm, w_br_fox, w_br_gdn, main, layer)
        xf = out_proj(merged, w_out, xf, layer)
        h2 = rmsnorm(xf, norm_ffn[layer], BF16)
        cw_p, cb_p = _pad_ffn_conv(ffn_conv_w[layer], ffn_conv_b[layer])
        act = ffn_up(h2, w_gv, *cw_p, *cb_p, layer, seq)
        xf = ffn_down(act, w_down_b, xf, layer)
    return rmsnorm(xf, norm_final, F32).reshape(bsz, seq, d)
```
